```python
import math
import jax, jax.numpy as jnp
from jax import lax
import numpy as np

D_MODEL = 1024
BATCH = 4
SEQ = 4096
DEPTH = 1
DEC_BATCH = 16
DEC_SEQ = 64
PAST_LEN = 4096

CHUNK = 64
N_META = 16
QBLOCK = 128
EPS = 1e-6
D_SSM = 2 * D_MODEL
SSM_HEAD_DIM = 64
SSM_HEADS = D_SSM // SSM_HEAD_DIM
SSM_GROUPS = 8
SSM_HPG = SSM_HEADS // SSM_GROUPS
D_STATE = 128
CONV_W = 4
D_CONV = D_SSM + 2 * SSM_GROUPS * D_STATE
ATTN_HEADS = 16
ATTN_HEAD_DIM = 64
D_ATTN = ATTN_HEADS * ATTN_HEAD_DIM
D_FF = 2816
OFF_XBC = D_SSM
OFF_DT = OFF_XBC + D_CONV
OFF_Q = OFF_DT + SSM_HEADS
OFF_K = OFF_Q + D_ATTN
OFF_V = OFF_K + D_ATTN
OFF_F = OFF_V + D_ATTN
OFF_GS = OFF_F + ATTN_HEADS
OFF_GA = OFF_GS + D_MODEL
D_IN_PROJ = OFF_GA + D_MODEL

kernel_name = 'fox_ssd_gated_hybrid_stream'


def rmsnorm(x, g):
    xf = x.astype(jnp.float32)
    y = xf * lax.rsqrt(jnp.mean(xf * xf, axis=-1, keepdims=True) + EPS)
    return (y * g.astype(jnp.float32)).astype(x.dtype)


def swiglu(x, w_in, w_out):
    a, b = jnp.split(x @ w_in, 2, axis=-1)
    return (jax.nn.silu(a) * b) @ w_out


def pad_time(a, front, back):
    cfg = [(0, 0)] * a.ndim
    cfg[1] = (front, back)
    return jnp.pad(a, cfg)


def causal_conv(xbc, hist, w, b):
    L = xbc.shape[1]
    xp = jnp.concatenate([hist.astype(xbc.dtype), xbc], axis=1)
    out = b + w[0] * xp[:, 0:L]
    for j in range(1, CONV_W):
        out = out + w[j] * xp[:, j:j + L]
    return jax.nn.silu(out), xp[:, -(CONV_W - 1):]


def ssd_chunked(x, dt, a, bmat, cmat, init):
    b, L = x.shape[:2]
    nc = L // CHUNK
    xd = (x * dt[..., None]).reshape(b, nc, CHUNK, SSM_GROUPS, SSM_HPG, SSM_HEAD_DIM)
    adt = (a * dt).reshape(b, nc, CHUNK, SSM_GROUPS, SSM_HPG)
    a_cs = jnp.cumsum(jnp.moveaxis(adt, 2, -1), axis=-1)
    bc = bmat.reshape(b, nc, CHUNK, SSM_GROUPS, D_STATE)
    cc = cmat.reshape(b, nc, CHUNK, SSM_GROUPS, D_STATE)
    seg = a_cs[..., :, None] - a_cs[..., None, :]
    causal = jnp.tril(jnp.ones((CHUNK, CHUNK), bool))
    ldec = jnp.exp(jnp.where(causal, seg, -jnp.inf))
    cb = jnp.einsum('bclgn,bcsgn->bcgls', cc, bc)
    y_diag = jnp.einsum('bcgls,bcgrls,bcsgrp->bclgrp', cb, ldec, xd)
    decay_st = jnp.exp(a_cs[..., -1:] - a_cs)
    st = jnp.einsum('bcsgn,bcgrs,bcsgrp->bcgrpn', bc, decay_st, xd)
    chunk_dec = jnp.exp(a_cs[..., -1])

    def step(carry, inp):
        s_c, d_c = inp
        return carry * d_c[..., None, None] + s_c, carry

    init_g = init.astype(jnp.float32).reshape(b, SSM_GROUPS, SSM_HPG, SSM_HEAD_DIM, D_STATE)
    final, prev = lax.scan(step, init_g, (jnp.moveaxis(st, 1, 0), jnp.moveaxis(chunk_dec, 1, 0)))
    prev = jnp.moveaxis(prev, 0, 1)
    y_off = jnp.einsum('bclgn,bcgrpn,bcgrl->bclgrp', cc, prev, jnp.exp(a_cs))
    y = (y_diag + y_off).reshape(b, L, SSM_HEADS, SSM_HEAD_DIM)
    return y, final.reshape(b, SSM_HEADS, SSM_HEAD_DIM, D_STATE)


def ssm_branch(z, xbc, dt_raw, conv_hist, ssm_init, front, back, lw):
    b, L = z.shape[:2]
    f32 = jnp.float32
    xbc, conv_state = causal_conv(xbc, conv_hist, lw['conv_w'], lw['conv_b'])
    xs, bm, cm = jnp.split(xbc, [D_SSM, D_SSM + SSM_GROUPS * D_STATE], axis=-1)
    xs = xs.reshape(b, L, SSM_HEADS, SSM_HEAD_DIM).astype(f32)
    bm = bm.reshape(b, L, SSM_GROUPS, D_STATE).astype(f32)
    cm = cm.reshape(b, L, SSM_GROUPS, D_STATE).astype(f32)
    dt = jax.nn.softplus(dt_raw.astype(f32) + lw['dt_bias'].astype(f32))
    a = -jnp.exp(lw['a_log'].astype(f32))
    y, final = ssd_chunked(pad_time(xs, front, back), pad_time(dt, front, back), a,
                           pad_time(bm, front, back), pad_time(cm, front, back), ssm_init)
    y = y[:, front:front + L] + lw['d_skip'].astype(f32)[:, None] * xs
    y = y.reshape(b, L, D_SSM) * jax.nn.silu(z.astype(f32))
    yg = y.reshape(b, L, SSM_GROUPS, D_SSM // SSM_GROUPS)
    yg = yg * lax.rsqrt(jnp.mean(yg * yg, axis=-1, keepdims=True) + EPS)
    y = yg.reshape(b, L, D_SSM) * lw['ssm_norm'].astype(f32)
    return y.astype(z.dtype), conv_state, final.astype(z.dtype)


def fox_attend(q, k, v, cq, ck, q_pos, k_pos):
    b, Lq = q.shape[:2]
    nb = -(-Lq // QBLOCK)
    pad = nb * QBLOCK - Lq
    qb = jnp.moveaxis(pad_time(q, 0, pad).reshape(b, nb, QBLOCK, ATTN_HEADS, ATTN_HEAD_DIM), 1, 0)
    cqb = jnp.moveaxis(pad_time(cq, 0, pad).reshape(b, nb, QBLOCK, ATTN_HEADS), 1, 0)
    pb = jnp.pad(q_pos, (0, pad), mode='edge').reshape(nb, QBLOCK)
    ckh = jnp.moveaxis(ck, 1, 2)
    scale = ATTN_HEAD_DIM ** -0.5

    def block(args):
        qi, cqi, pi = args
        s = jnp.einsum('bqhd,bkhd->bhqk', qi, k, preferred_element_type=jnp.float32) * scale
        s = s + jnp.moveaxis(cqi, 1, 2)[..., None] - ckh[:, :, None, :]
        s = jnp.where(k_pos[None, :] <= pi[:, None], s, -jnp.inf)
        p = jax.nn.softmax(s, axis=-1)
        return jnp.einsum('bhqk,bkhd->bqhd', p.astype(v.dtype), v)

    out = lax.map(block, (qb, cqb, pb))
    return jnp.moveaxis(out, 0, 1).reshape(b, nb * QBLOCK, ATTN_HEADS, ATTN_HEAD_DIM)[:, :Lq]


def pre_mix(h, lw):
    h = h + 0.5 * swiglu(rmsnorm(h, lw['norm_ffn1']), lw['ffn1_w_in'], lw['ffn1_w_out'])
    return h, rmsnorm(h, lw['norm_mix'])


def split_kvf(p, f_bias):
    b, L = p.shape[:2]
    k, v, f = jnp.split(p, [D_ATTN, 2 * D_ATTN], axis=-1)
    logf = jax.nn.log_sigmoid(f.astype(jnp.float32) + f_bias.astype(jnp.float32))
    return (k.reshape(b, L, ATTN_HEADS, ATTN_HEAD_DIM), v.reshape(b, L, ATTN_HEADS, ATTN_HEAD_DIM), logf)


def empty_history(b, dtype):
    return (jnp.zeros((b, CONV_W - 1, D_CONV), dtype),
            jnp.zeros((b, SSM_HEADS, SSM_HEAD_DIM, D_STATE), jnp.float32),
            jnp.zeros((b, 0, ATTN_HEADS, ATTN_HEAD_DIM), dtype),
            jnp.zeros((b, 0, ATTN_HEADS, ATTN_HEAD_DIM), dtype),
            jnp.zeros((b, 0, ATTN_HEADS), jnp.float32))


def layer(h, lw, conv_hist, ssm_init, k_pre, v_pre, logf_pre, front, back):
    b, L = h.shape[:2]
    h, u = pre_mix(h, lw)
    proj = u @ lw['w_in']
    z, xbc, dt_raw, q, kvf, g_ssm, g_attn = jnp.split(proj, [OFF_XBC, OFF_DT, OFF_Q, OFF_K, OFF_GS, OFF_GA], axis=-1)
    k, v, logf = split_kvf(kvf, lw['f_bias'])
    q = q.reshape(b, L, ATTN_HEADS, ATTN_HEAD_DIM)
    o_ssm, conv_state, ssm_state = ssm_branch(z, xbc, dt_raw, conv_hist, ssm_init, front, back, lw)
    P = k_pre.shape[1]
    k_all = jnp.concatenate([k_pre.astype(k.dtype), k], axis=1)
    v_all = jnp.concatenate([v_pre.astype(v.dtype), v], axis=1)
    c_all = jnp.cumsum(jnp.concatenate([logf_pre.astype(jnp.float32), logf], axis=1), axis=1)
    o_attn = fox_attend(q, k_all, v_all, c_all[:, P:], c_all, P + jnp.arange(L), jnp.arange(P + L))
    merged = (jax.nn.sigmoid(g_ssm) * (o_ssm @ lw['w_br_ssm'])
              + jax.nn.sigmoid(g_attn) * (o_attn.reshape(b, L, D_ATTN) @ lw['w_br_attn']))
    h = h + merged @ lw['w_out']
    h = h + 0.5 * swiglu(rmsnorm(h, lw['norm_ffn2']), lw['ffn2_w_in'], lw['ffn2_w_out'])
    return h, k, v, logf, ssm_state, conv_state


def setup_inputs(seed: int = 0) -> dict:
    key = jax.random.key(seed)
    ks = jax.random.split(key, 27)
    f32 = jnp.float32

    def nrm(k, shape, scale=1.0):
        return jax.random.normal(k, shape, f32) * scale

    def gain(k, shape):
        return 1.0 + 0.02 * jax.random.normal(k, shape, f32)

    dt0 = jnp.exp(jax.random.uniform(ks[15], (DEPTH, SSM_HEADS), f32, math.log(1e-3), math.log(1e-1)))
    return {
        'x_prompt': nrm(ks[0], (BATCH, SEQ, D_MODEL)),
        'x_sample': nrm(ks[1], (DEC_BATCH, DEC_SEQ, D_MODEL)),
        'cache_k': nrm(ks[2], (DEPTH, DEC_BATCH, PAST_LEN, ATTN_HEADS, ATTN_HEAD_DIM)),
        'cache_v': nrm(ks[3], (DEPTH, DEC_BATCH, PAST_LEN, ATTN_HEADS, ATTN_HEAD_DIM)),
        'cache_logf': jax.nn.log_sigmoid(2.0 + nrm(ks[4], (DEPTH, DEC_BATCH, PAST_LEN, ATTN_HEADS))),
        'state_ssm': nrm(ks[5], (DEPTH, DEC_BATCH, SSM_HEADS, SSM_HEAD_DIM, D_STATE), 0.5),
        'state_conv': nrm(ks[6], (DEPTH, DEC_BATCH, CONV_W - 1, D_CONV)),
        'meta_tokens': nrm(ks[7], (N_META, D_MODEL)),
        'norm_ffn1': gain(ks[8], (DEPTH, D_MODEL)),
        'ffn1_w_in': nrm(ks[9], (DEPTH, D_MODEL, 2 * D_FF), D_MODEL ** -0.5),
        'ffn1_w_out': nrm(ks[10], (DEPTH, D_FF, D_MODEL), D_FF ** -0.5),
        'norm_mix': gain(ks[11], (DEPTH, D_MODEL)),
        'w_in': nrm(ks[12], (DEPTH, D_MODEL, D_IN_PROJ), D_MODEL ** -0.5),
        'conv_w': nrm(ks[13], (DEPTH, CONV_W, D_CONV), CONV_W ** -0.5),
        'conv_b': nrm(ks[14], (DEPTH, D_CONV), 0.02),
        'dt_bias': dt0 + jnp.log(-jnp.expm1(-dt0)),
        'a_log': jnp.log(jax.random.uniform(ks[16], (DEPTH, SSM_HEADS), f32, 1.0, 16.0)),
        'd_skip': gain(ks[17], (DEPTH, SSM_HEADS)),
        'f_bias': jax.random.uniform(ks[18], (DEPTH, ATTN_HEADS), f32, 0.5, 3.0),
        'ssm_norm': gain(ks[19], (DEPTH, D_SSM)),
        'w_br_ssm': nrm(ks[20], (DEPTH, D_SSM, D_MODEL), D_SSM ** -0.5),
        'w_br_attn': nrm(ks[21], (DEPTH, D_ATTN, D_MODEL), D_ATTN ** -0.5),
        'w_out': nrm(ks[22], (DEPTH, D_MODEL, D_MODEL), D_MODEL ** -0.5),
        'norm_ffn2': gain(ks[23], (DEPTH, D_MODEL)),
        'ffn2_w_in': nrm(ks[24], (DEPTH, D_MODEL, 2 * D_FF), D_MODEL ** -0.5),
        'ffn2_w_out': nrm(ks[25], (DEPTH, D_FF, D_MODEL), D_FF ** -0.5),
        'norm_final': gain(ks[26], (D_MODEL,)),
    }


def reference(x_prompt, x_sample, cache_k, cache_v, cache_logf, state_ssm, state_conv, meta_tokens,
              norm_ffn1, ffn1_w_in, ffn1_w_out, norm_mix, w_in, conv_w, conv_b, dt_bias, a_log, d_skip,
              f_bias, ssm_norm, w_br_ssm, w_br_attn, w_out, norm_ffn2, ffn2_w_in, ffn2_w_out, norm_final):
    bp = x_prompt.shape[0]
    bs = x_sample.shape[0]
    hp = jnp.concatenate([jnp.broadcast_to(meta_tokens[None].astype(x_prompt.dtype), (bp, N_META, D_MODEL)), x_prompt], axis=1)
    hs = x_sample
    meta_h = meta_tokens[None]
    st_p = []
    st_s = []
    for l in range(DEPTH):
        lw = {'norm_ffn1': norm_ffn1[l], 'ffn1_w_in': ffn1_w_in[l], 'ffn1_w_out': ffn1_w_out[l],
              'norm_mix': norm_mix[l], 'w_in': w_in[l], 'conv_w': conv_w[l], 'conv_b': conv_b[l],
              'dt_bias': dt_bias[l], 'a_log': a_log[l], 'd_skip': d_skip[l], 'f_bias': f_bias[l],
              'ssm_norm': ssm_norm[l], 'w_br_ssm': w_br_ssm[l], 'w_br_attn': w_br_attn[l], 'w_out': w_out[l],
              'norm_ffn2': norm_ffn2[l], 'ffn2_w_in': ffn2_w_in[l], 'ffn2_w_out': ffn2_w_out[l]}
        hp, kp, vp, lfp, sp, cp = layer(hp, lw, *empty_history(bp, hp.dtype), (-hp.shape[1]) % CHUNK, 0)
        st_p.append((kp, vp, lfp, sp, cp))
        _, u_m = pre_mix(meta_h, lw)
        k_m, v_m, lf_m = split_kvf(u_m @ lw['w_in'][:, OFF_K:OFF_GS], lw['f_bias'])
        k_pre = jnp.concatenate([jnp.broadcast_to(k_m, (bs,) + k_m.shape[1:]).astype(cache_k.dtype), cache_k[l]], axis=1)
        v_pre = jnp.concatenate([jnp.broadcast_to(v_m, (bs,) + v_m.shape[1:]).astype(cache_v.dtype), cache_v[l]], axis=1)
        lf_pre = jnp.concatenate([jnp.broadcast_to(lf_m, (bs,) + lf_m.shape[1:]), cache_logf[l].astype(jnp.float32)], axis=1)
        hs, ks_, vs_, lfs, ss, cs = layer(hs, lw, state_conv[l], state_ssm[l], k_pre, v_pre, lf_pre, 0, (-hs.shape[1]) % CHUNK)
        st_s.append((ks_, vs_, lfs, ss, cs))
        if l + 1 < DEPTH:
            meta_h = layer(meta_h, lw, *empty_history(1, meta_h.dtype), (-N_META) % CHUNK, 0)[0]
    y_prompt = rmsnorm(hp[:, N_META:], norm_final)
    y_sample = rmsnorm(hs, norm_final)
    k_prompt = jnp.stack([s[0] for s in st_p])
    v_prompt = jnp.stack([s[1] for s in st_p])
    logf_prompt = jnp.stack([s[2] for s in st_p])
    ssm_prompt = jnp.stack([s[3] for s in st_p])
    conv_prompt = jnp.stack([s[4] for s in st_p])
    k_sample = jnp.stack([s[0] for s in st_s])
    v_sample = jnp.stack([s[1] for s in st_s])
    logf_sample = jnp.stack([s[2] for s in st_s])
    ssm_sample = jnp.stack([s[3] for s in st_s])
    conv_sample = jnp.stack([s[4] for s in st_s])
    return (y_prompt, y_sample, k_prompt, v_prompt, logf_prompt, ssm_prompt, conv_prompt,
            k_sample, v_sample, logf_sample, ssm_sample, conv_sample)
```

```python
import functools

import numpy as np
import jax
import jax.numpy as jnp
from jax import lax
from jax.experimental import pallas as pl
from jax.experimental.pallas import tpu as pltpu

F32 = jnp.float32
BF16 = jnp.bfloat16

D_MODEL = 1024
N_META = 16
CHUNK = 64
EPS = 1e-6
D_SSM = 2048
SSM_HEADS = 32
SSM_HEAD_DIM = 64
SSM_GROUPS = 8
D_STATE = 128
CONV_W = 4
D_CONV = D_SSM + 2 * SSM_GROUPS * D_STATE
ATTN_HEADS = 16
ATTN_HEAD_DIM = 64
D_ATTN = ATTN_HEADS * ATTN_HEAD_DIM
D_FF = 2816
HEAD_PAIRS = ATTN_HEADS // 2
LANES = 128
GROUP_W = D_SSM // SSM_GROUPS

OFF_XBC = D_SSM
OFF_DT = OFF_XBC + D_CONV
OFF_Q = OFF_DT + SSM_HEADS
OFF_K = OFF_Q + D_ATTN
OFF_V = OFF_K + D_ATTN
OFF_F = OFF_V + D_ATTN
OFF_GS = OFF_F + ATTN_HEADS
OFF_GA = OFF_GS + D_MODEL

P_XBC = 0
P_Z = D_CONV
P_Q = P_Z + D_SSM
P_K = P_Q + D_ATTN
P_V = P_K + D_ATTN
P_GS = P_V + D_ATTN
P_GA = P_GS + D_MODEL
P_TOTAL = P_GA + D_MODEL
S_DT = 0
S_F = SSM_HEADS

VMEM_LIMIT = 56 * 1024 * 1024

FF_CHUNK = D_FF // 2
PROJ_TN = P_TOTAL // 4
ATT_T = 512
CUMSUM_BLK = 256


def _dot(a, b):
    return jnp.dot(a, b, preferred_element_type=F32)


def _dot_nt(a, b):
    return lax.dot_general(a, b, (((1,), (1,)), ((), ())), preferred_element_type=F32)


def _dot_tn(a, b):
    return lax.dot_general(a, b, (((0,), (0,)), ((), ())), preferred_element_type=F32)


def _split3(x):
    hi = x.astype(BF16)
    r = x - hi.astype(F32)
    mid = r.astype(BF16)
    lo = (r - mid.astype(F32)).astype(BF16)
    return hi, mid, lo


def _rmsnorm(x, g):
    return x * lax.rsqrt(jnp.mean(x * x, axis=-1, keepdims=True) + EPS) * g


def _softplus(x):
    return jnp.maximum(x, 0.0) + jnp.log1p(jnp.exp(-jnp.abs(x)))


def _params(*sem):
    return pltpu.CompilerParams(dimension_semantics=sem, vmem_limit_bytes=VMEM_LIMIT)


def _ffn_kernel(x_ref, g1_ref, wa_ref, wb_ref, wo_ref, g2_ref, *refs, emit_h):
    if emit_h:
        h_ref, u_ref, xn_sc, acc_sc = refs
    else:
        u_ref, xn_sc, acc_sc = refs
    c = pl.program_id(1)

    @pl.when(c == 0)
    def _():
        xn_sc[...] = _rmsnorm(x_ref[...], g1_ref[...]).astype(BF16)
        acc_sc[...] = jnp.zeros_like(acc_sc)

    xn = xn_sc[...]
    a = _dot(xn, wa_ref[...])
    b = _dot(xn, wb_ref[...])
    g = (a * jax.nn.sigmoid(a) * b).astype(BF16)
    acc_sc[...] += _dot(g, wo_ref[...])

    @pl.when(c == pl.num_programs(1) - 1)
    def _():
        h = x_ref[...] + 0.5 * acc_sc[...]
        if emit_h:
            h_ref[...] = h
        u_ref[...] = _rmsnorm(h, g2_ref[...]).astype(u_ref.dtype)


def _ffn(x, g1, w_in, w_out, g2, *, tm, emit_h, u_dtype):
    rows = x.shape[0]
    nfc = D_FF // FF_CHUNK
    row_spec = pl.BlockSpec((tm, D_MODEL), lambda i, c: (i, 0))
    vec_spec = pl.BlockSpec((1, D_MODEL), lambda i, c: (0, 0))
    out_shape = [jax.ShapeDtypeStruct((rows, D_MODEL), u_dtype)]
    out_specs = [row_spec]
    if emit_h:
        out_shape = [jax.ShapeDtypeStruct((rows, D_MODEL), F32)] + out_shape
        out_specs = [row_spec] + out_specs
    return pl.pallas_call(
        functools.partial(_ffn_kernel, emit_h=emit_h),
        grid=(rows // tm, nfc),
        in_specs=[
            row_spec, vec_spec,
            pl.BlockSpec((D_MODEL, FF_CHUNK), lambda i, c: (0, c)),
            pl.BlockSpec((D_MODEL, FF_CHUNK), lambda i, c: (0, nfc + c)),
            pl.BlockSpec((FF_CHUNK, D_MODEL), lambda i, c: (c, 0)),
            vec_spec,
        ],
        out_specs=out_specs,
        out_shape=out_shape,
        scratch_shapes=[pltpu.VMEM((tm, D_MODEL), BF16), pltpu.VMEM((tm, D_MODEL), F32)],
        compiler_params=_params("parallel", "arbitrary"),
        name="ffn",
    )(x, g1, w_in, w_in, w_out, g2)


def _proj_kernel(u_ref, w_ref, o_ref):
    o_ref[...] = _dot(u_ref[...], w_ref[...])


def _proj(u, w, *, tm):
    rows = u.shape[0]
    return pl.pallas_call(
        _proj_kernel,
        grid=(P_TOTAL // PROJ_TN, rows // tm),
        in_specs=[pl.BlockSpec((tm, D_MODEL), lambda j, i: (i, 0)),
                  pl.BlockSpec((D_MODEL, PROJ_TN), lambda j, i: (0, j))],
        out_specs=pl.BlockSpec((tm, PROJ_TN), lambda j, i: (i, j)),
        out_shape=jax.ShapeDtypeStruct((rows, P_TOTAL), F32),
        compiler_params=_params("parallel", "parallel"),
        name="proj",
    )(u, w)


def _small_kernel(u_ref, w_ref, b_ref, o_ref):
    x = _dot(u_ref[...], w_ref[...]) + b_ref[...]
    lane = lax.broadcasted_iota(jnp.int32, (1, LANES), 1)
    sgn = jnp.where(lane < S_F, 1.0, -1.0)
    o_ref[...] = sgn * _softplus(sgn * x)


def _small(u, w, b, *, tm):
    rows = u.shape[0]
    return pl.pallas_call(
        _small_kernel,
        grid=(rows // tm,),
        in_specs=[pl.BlockSpec((tm, D_MODEL), lambda i: (i, 0)),
                  pl.BlockSpec((D_MODEL, LANES), lambda i: (0, 0)),
                  pl.BlockSpec((1, LANES), lambda i: (0, 0))],
        out_specs=pl.BlockSpec((tm, LANES), lambda i: (i, 0)),
        out_shape=jax.ShapeDtypeStruct((rows, LANES), F32),
        compiler_params=_params("parallel"),
        name="small",
    )(u, w, b)


HIST0 = 8 - (CONV_W - 1)


def _ssd_kernel(xbc_ref, z_ref, dt_ref, s0_ref, h0_ref, cw_ref, cb_ref, alog_ref, dskip_ref, norm_ref,
                e_ref, o_ref, sT_ref, hT_ref, st_sc, xp_sc, y_sc, *, n_pad):
    T = CHUNK
    c = pl.program_id(1)

    @pl.when(c == 0)
    def _():
        st_sc[...] = s0_ref[...]
        xp_sc[HIST0:8, :] = h0_ref[...]

    xraw = xbc_ref[...]
    xp_sc[8:8 + T, :] = xraw
    cw = cw_ref[...]
    conv = cb_ref[...] + cw[0:1] * xp_sc[HIST0:HIST0 + T, :]
    conv = conv + cw[1:2] * xp_sc[HIST0 + 1:HIST0 + 1 + T, :]
    conv = conv + cw[2:3] * xp_sc[HIST0 + 2:HIST0 + 2 + T, :]
    conv = conv + cw[3:4] * xraw
    xp_sc[HIST0:8, :] = xraw[T - (CONV_W - 1):T, :]
    xc = conv * jax.nn.sigmoid(conv)
    dtv = dt_ref[:, S_DT:S_DT + SSM_HEADS]
    if n_pad:
        valid = lax.broadcasted_iota(jnp.int32, (T, 1), 0) >= n_pad
        xc = jnp.where(valid, xc, 0.0)
        dtv = jnp.where(valid, dtv, 0.0)

    a = -jnp.exp(alog_ref[...])
    adt = a * dtv
    row = lax.broadcasted_iota(jnp.int32, (T, T), 0)
    col = lax.broadcasted_iota(jnp.int32, (T, T), 1)
    causal = col <= row
    tril = jnp.where(causal, 1.0, 0.0).astype(BF16)
    r32 = lax.broadcasted_iota(jnp.int32, (SSM_HEADS, SSM_HEADS), 0)
    c32 = lax.broadcasted_iota(jnp.int32, (SSM_HEADS, SSM_HEADS), 1)
    eye = jnp.where(r32 == c32, 1.0, 0.0).astype(BF16)
    expand = e_ref[...]

    a_cs = sum(_dot(tril, piece) for piece in _split3(adt))
    a_cs_t = sum(_dot_nt(eye, piece) for piece in _split3(a_cs))
    dt_e = sum(_dot(piece, expand) for piece in _split3(dtv))
    acs_e = sum(_dot(piece, expand) for piece in _split3(a_cs))

    xs = xc[:, :D_SSM]
    bm = xc[:, D_SSM:D_SSM + SSM_GROUPS * D_STATE].astype(BF16)
    cm = xc[:, D_SSM + SSM_GROUPS * D_STATE:].astype(BF16)
    xd = xs * dt_e
    al_e = acs_e[T - 1:T, :]
    xdw = (xd * jnp.exp(al_e - acs_e)).astype(BF16)
    cdec = jnp.exp(al_e)
    eacs = jnp.exp(acs_e)
    lane = lax.broadcasted_iota(jnp.int32, (1, D_SSM), 1)
    lo = (lane & ATTN_HEAD_DIM) == 0
    xd_lo = jnp.where(lo, xd, 0.0).astype(BF16)
    xd_hi = jnp.where(lo, 0.0, xd).astype(BF16)

    for g in range(SSM_GROUPS):
        gs = slice(g * GROUP_W, (g + 1) * GROUP_W)
        cmg = cm[:, g * D_STATE:(g + 1) * D_STATE]
        bmg = bm[:, g * D_STATE:(g + 1) * D_STATE]
        cb = _dot_nt(cmg, bmg)
        stg = st_sc[:, gs]
        yoff = _dot(cmg, stg.astype(BF16))
        for rr in range(2):
            pair = g * 2 + rr
            ps = slice(pair * LANES, (pair + 1) * LANES)
            ydiag = None
            for hh in range(2):
                h = pair * 2 + hh
                seg = a_cs[:, h:h + 1] - a_cs_t[h:h + 1, :]
                m = (cb * jnp.exp(jnp.where(causal, seg, -jnp.inf))).astype(BF16)
                d = _dot(m, (xd_lo, xd_hi)[hh][:, ps])
                ydiag = d if ydiag is None else ydiag + d
            y_sc[:, ps] = ydiag + yoff[:, rr * LANES:(rr + 1) * LANES] * eacs[:, ps]
        st_sc[:, gs] = stg * cdec[:, gs] + _dot_tn(bmg, xdw[:, gs])

    y = y_sc[...] + dskip_ref[...] * xs
    zz = z_ref[...]
    y = y * (zz * jax.nn.sigmoid(zz))
    for g in range(SSM_GROUPS):
        gs = slice(g * GROUP_W, (g + 1) * GROUP_W)
        yg = y[:, gs]
        yg = yg * lax.rsqrt(jnp.mean(yg * yg, axis=-1, keepdims=True) + EPS)
        o_ref[:, gs] = (yg * norm_ref[:, gs]).astype(o_ref.dtype)

    @pl.when(c == pl.num_programs(1) - 1)
    def _():
        sT_ref[...] = st_sc[...]
        hT_ref[...] = xp_sc[HIST0:8, :]


def _ssd(proj3, small3, s0, h0, cw, cb, alog, dskip, norm, expand, *, nb, b_off, shared_init, n_pad):
    nc = proj3.shape[1] // CHUNK
    init = (lambda b, c: (0, 0, 0)) if shared_init else (lambda b, c: (b, 0, 0))
    const = lambda b, c: (0, 0)
    return pl.pallas_call(
        functools.partial(_ssd_kernel, n_pad=n_pad),
        grid=(nb, nc),
        in_specs=[
            pl.BlockSpec((None, CHUNK, D_CONV), lambda b, c: (b + b_off, c, P_XBC // D_CONV)),
            pl.BlockSpec((None, CHUNK, D_SSM), lambda b, c: (b + b_off, c, P_Z // D_SSM)),
            pl.BlockSpec((None, CHUNK, LANES), lambda b, c: (b + b_off, c, 0)),
            pl.BlockSpec((None, D_STATE, D_SSM), init),
            pl.BlockSpec((None, CONV_W - 1, D_CONV), init),
            pl.BlockSpec((CONV_W, D_CONV), const),
            pl.BlockSpec((1, D_CONV), const),
            pl.BlockSpec((1, SSM_HEADS), const),
            pl.BlockSpec((1, D_SSM), const),
            pl.BlockSpec((1, D_SSM), const),
            pl.BlockSpec((SSM_HEADS, D_SSM), const),
        ],
        out_specs=[
            pl.BlockSpec((None, CHUNK, D_SSM), lambda b, c: (b, c, 0)),
            pl.BlockSpec((None, D_STATE, D_SSM), lambda b, c: (b, 0, 0)),
            pl.BlockSpec((None, CONV_W - 1, D_CONV), lambda b, c: (b, 0, 0)),
        ],
        out_shape=[
            jax.ShapeDtypeStruct((nb, nc * CHUNK, D_SSM), BF16),
            jax.ShapeDtypeStruct((nb, D_STATE, D_SSM), F32),
            jax.ShapeDtypeStruct((nb, CONV_W - 1, D_CONV), F32),
        ],
        scratch_shapes=[
            pltpu.VMEM((D_STATE, D_SSM), F32),
            pltpu.VMEM((8 + CHUNK, D_CONV), F32),
            pltpu.VMEM((CHUNK, D_SSM), F32),
        ],
        compiler_params=_params("parallel", "arbitrary"),
        name="ssd",
    )(proj3, proj3, small3, s0, h0, cw, cb, alog, dskip, norm, expand)


def _cumsum_kernel(x_ref, o_ref):
    n = x_ref.shape[-1] // CUMSUM_BLK
    r = lax.broadcasted_iota(jnp.int32, (CUMSUM_BLK, CUMSUM_BLK), 0)
    c = lax.broadcasted_iota(jnp.int32, (CUMSUM_BLK, CUMSUM_BLK), 1)
    upper = jnp.where(r <= c, 1.0, 0.0).astype(BF16)
    carry = jnp.zeros((ATTN_HEADS, 1), F32)
    for i in range(n):
        blk = slice(i * CUMSUM_BLK, (i + 1) * CUMSUM_BLK)
        cs = sum(_dot(piece, upper) for piece in _split3(x_ref[:, blk])) + carry
        o_ref[:, blk] = cs
        carry = cs[:, CUMSUM_BLK - 1:CUMSUM_BLK]


def _cumsum_t(x_t):
    b, h, l = x_t.shape
    spec = pl.BlockSpec((None, h, l), lambda i: (i, 0, 0))
    return pl.pallas_call(
        _cumsum_kernel, grid=(b,), in_specs=[spec], out_specs=spec,
        out_shape=jax.ShapeDtypeStruct(x_t.shape, F32),
        compiler_params=_params("parallel"), name="cumsum",
    )(x_t)


def _cum_logf(lf_all):
    l = lf_all.shape[1]
    lp = -(-l // CUMSUM_BLK) * CUMSUM_BLK
    x_t = jnp.pad(jnp.transpose(lf_all, (0, 2, 1)), ((0, 0), (0, 0), (0, lp - l)))
    return _cumsum_t(x_t)[:, :, :l]


def _pick_head_col(cq_tile, head):
    hl = lax.broadcasted_iota(jnp.int32, (1, ATTN_HEADS), 1)
    return jnp.sum(jnp.where(hl == head, cq_tile, 0.0), axis=1, keepdims=True)


def _attn_prompt_kernel(it_ref, jt_ref, q_ref, k_ref, v_ref, km_ref, vm_ref, cq_ref, ck_ref, ckm_ref,
                        o_ref, q_sc, cq_sc, m_sc, l_sc, acc_sc):
    p = pl.program_id(1)
    t = pl.program_id(2)
    i = it_ref[t]
    j = jt_ref[t]
    lane = lax.broadcasted_iota(jnp.int32, (1, LANES), 1)
    lo = lane < ATTN_HEAD_DIM

    @pl.when(j == 0)
    def _():
        q2 = q_ref[...] * (ATTN_HEAD_DIM ** -0.5)
        q_sc[0] = jnp.where(lo, q2, 0.0).astype(BF16)
        q_sc[1] = jnp.where(lo, 0.0, q2).astype(BF16)
        cqt = cq_ref[...]
        cq_sc[0] = _pick_head_col(cqt, 2 * p)
        cq_sc[1] = _pick_head_col(cqt, 2 * p + 1)
        km = km_ref[...].astype(BF16)
        vm = vm_ref[...]
        vmh = (jnp.where(lo, vm, 0.0).astype(BF16), jnp.where(lo, 0.0, vm).astype(BF16))
        ckm = ckm_ref[...]
        acc = None
        for hh in range(2):
            s = _dot_nt(q_sc[hh], km) + cq_sc[hh] - ckm[hh:hh + 1, :]
            m = jnp.max(s, axis=1, keepdims=True)
            pr = jnp.exp(s - m)
            m_sc[hh] = m
            l_sc[hh] = jnp.sum(pr, axis=1, keepdims=True)
            d = _dot(pr.astype(BF16), vmh[hh])
            acc = d if acc is None else acc + d
        acc_sc[...] = acc

    def step(diagonal):
        k2 = k_ref[...].astype(BF16)
        v2 = v_ref[...]
        vh = (jnp.where(lo, v2, 0.0).astype(BF16), jnp.where(lo, 0.0, v2).astype(BF16))
        ck = ck_ref[...]
        if diagonal:
            row = lax.broadcasted_iota(jnp.int32, (ATT_T, ATT_T), 0)
            col = lax.broadcasted_iota(jnp.int32, (ATT_T, ATT_T), 1)
            visible = col <= row
        alphas = []
        contrib = None
        for hh in range(2):
            s = _dot_nt(q_sc[hh], k2) + cq_sc[hh] - ck[hh:hh + 1, :]
            if diagonal:
                s = jnp.where(visible, s, -jnp.inf)
            m_old = m_sc[hh]
            m_new = jnp.maximum(m_old, jnp.max(s, axis=1, keepdims=True))
            alpha = jnp.exp(m_old - m_new)
            pr = jnp.exp(s - m_new)
            l_sc[hh] = alpha * l_sc[hh] + jnp.sum(pr, axis=1, keepdims=True)
            m_sc[hh] = m_new
            d = _dot(pr.astype(BF16), vh[hh])
            contrib = d if contrib is None else contrib + d
            alphas.append(alpha)
        acc_sc[...] = acc_sc[...] * jnp.where(lo, alphas[0], alphas[1]) + contrib

    @pl.when(j < i)
    def _():
        step(False)

    @pl.when(j == i)
    def _():
        step(True)
        o_ref[...] = (acc_sc[...] / jnp.where(lo, l_sc[0], l_sc[1])).astype(o_ref.dtype)


def _attn_prompt(proj3, proj_meta, meta_blk, cq, ck_t, ckm_t):
    nb, l, _ = proj3.shape
    nt = l // ATT_T
    it = np.array([i for i in range(nt) for _ in range(i + 1)], np.int32)
    jt = np.array([j for i in range(nt) for j in range(i + 1)], np.int32)
    lb = lambda off: off // LANES
    grid_spec = pltpu.PrefetchScalarGridSpec(
        num_scalar_prefetch=2,
        grid=(nb, HEAD_PAIRS, len(it)),
        in_specs=[
            pl.BlockSpec((None, ATT_T, LANES), lambda b, p, t, it, jt: (b, it[t], lb(P_Q) + p)),
            pl.BlockSpec((None, ATT_T, LANES), lambda b, p, t, it, jt: (b, jt[t], lb(P_K) + p)),
            pl.BlockSpec((None, ATT_T, LANES), lambda b, p, t, it, jt: (b, jt[t], lb(P_V) + p)),
            pl.BlockSpec((N_META, LANES), lambda b, p, t, it, jt: (meta_blk, lb(P_K) + p)),
            pl.BlockSpec((N_META, LANES), lambda b, p, t, it, jt: (meta_blk, lb(P_V) + p)),
            pl.BlockSpec((None, ATT_T, ATTN_HEADS), lambda b, p, t, it, jt: (b, it[t], 0)),
            pl.BlockSpec((None, None, 2, ATT_T), lambda b, p, t, it, jt: (b, p, 0, jt[t])),
            pl.BlockSpec((None, None, 2, N_META), lambda b, p, t, it, jt: (b, p, 0, 0)),
        ],
        out_specs=pl.BlockSpec((None, ATT_T, LANES), lambda b, p, t, it, jt: (b, it[t], p)),
        scratch_shapes=[
            pltpu.VMEM((2, ATT_T, LANES), BF16),
            pltpu.VMEM((2, ATT_T, 1), F32),
            pltpu.VMEM((2, ATT_T, 1), F32),
            pltpu.VMEM((2, ATT_T, 1), F32),
            pltpu.VMEM((ATT_T, LANES), F32),
        ],
    )
    return pl.pallas_call(
        _attn_prompt_kernel,
        grid_spec=grid_spec,
        out_shape=jax.ShapeDtypeStruct((nb, l, D_ATTN), BF16),
        compiler_params=_params("parallel", "parallel", "arbitrary"),
        name="attn_prompt",
    )(jnp.asarray(it), jnp.asarray(jt), proj3, proj3, proj3, proj_meta, proj_meta, cq, ck_t, ckm_t)


def _attn_sample_kernel(q_ref, kn_ref, vn_ref, km_ref, vm_ref, kc_ref, vc_ref, cq_ref, ckm_ref, ckc_ref,
                        ckn_ref, o_ref):
    p = pl.program_id(1)
    T = CHUNK
    lane = lax.broadcasted_iota(jnp.int32, (1, LANES), 1)
    lo = lane < ATTN_HEAD_DIM
    q2 = q_ref[...] * (ATTN_HEAD_DIM ** -0.5)
    cqt = cq_ref[...]
    km = km_ref[...].astype(BF16)
    kc = kc_ref[...].astype(BF16)
    kn = kn_ref[...].astype(BF16)
    vm = vm_ref[...]
    vc = vc_ref[...]
    vn = vn_ref[...]
    row = lax.broadcasted_iota(jnp.int32, (T, T), 0)
    col = lax.broadcasted_iota(jnp.int32, (T, T), 1)
    visible = col <= row
    out = None
    for hh in range(2):
        sel = lo if hh == 0 else jnp.logical_not(lo)
        qh = jnp.where(sel, q2, 0.0).astype(BF16)
        cq = _pick_head_col(cqt, 2 * p + hh)
        s_m = _dot_nt(qh, km) + cq - ckm_ref[hh:hh + 1, :]
        s_c = _dot_nt(qh, kc) + cq - ckc_ref[hh:hh + 1, :]
        s_n = jnp.where(visible, _dot_nt(qh, kn) + cq - ckn_ref[hh:hh + 1, :], -jnp.inf)
        m = jnp.maximum(jnp.maximum(jnp.max(s_m, axis=1, keepdims=True), jnp.max(s_c, axis=1, keepdims=True)),
                        jnp.max(s_n, axis=1, keepdims=True))
        p_m = jnp.exp(s_m - m)
        p_c = jnp.exp(s_c - m)
        p_n = jnp.exp(s_n - m)
        l = (jnp.sum(p_m, axis=1, keepdims=True) + jnp.sum(p_c, axis=1, keepdims=True)
             + jnp.sum(p_n, axis=1, keepdims=True))
        o = (_dot(p_m.astype(BF16), jnp.where(sel, vm, 0.0).astype(BF16))
             + _dot(p_c.astype(BF16), jnp.where(sel, vc, 0.0).astype(BF16))
             + _dot(p_n.astype(BF16), jnp.where(sel, vn, 0.0).astype(BF16))) / l
        out = o if out is None else out + o
    o_ref[...] = out.astype(o_ref.dtype)


def _attn_sample(proj3, proj2, meta_blk, cache_k, cache_v, cq, ckm_t, ckc_t, ckn_t, *, nb):
    past = cache_k.shape[1]
    lb = lambda off: off // LANES
    return pl.pallas_call(
        _attn_sample_kernel,
        grid=(nb, HEAD_PAIRS),
        in_specs=[
            pl.BlockSpec((None, CHUNK, LANES), lambda b, p: (b, 0, lb(P_Q) + p)),
            pl.BlockSpec((None, CHUNK, LANES), lambda b, p: (b, 0, lb(P_K) + p)),
            pl.BlockSpec((None, CHUNK, LANES), lambda b, p: (b, 0, lb(P_V) + p)),
            pl.BlockSpec((N_META, LANES), lambda b, p: (meta_blk, lb(P_K) + p)),
            pl.BlockSpec((N_META, LANES), lambda b, p: (meta_blk, lb(P_V) + p)),
            pl.BlockSpec((None, past, LANES), lambda b, p: (b, 0, p)),
            pl.BlockSpec((None, past, LANES), lambda b, p: (b, 0, p)),
            pl.BlockSpec((None, CHUNK, ATTN_HEADS), lambda b, p: (b, 0, 0)),
            pl.BlockSpec((None, None, 2, N_META), lambda b, p: (b, p, 0, 0)),
            pl.BlockSpec((None, None, 2, past), lambda b, p: (b, p, 0, 0)),
            pl.BlockSpec((None, None, 2, CHUNK), lambda b, p: (b, p, 0, 0)),
        ],
        out_specs=pl.BlockSpec((None, CHUNK, LANES), lambda b, p: (b, 0, p)),
        out_shape=jax.ShapeDtypeStruct((nb, CHUNK, D_ATTN), BF16),
        compiler_params=_params("parallel", "parallel"),
        name="attn_sample",
    )(proj3, proj3, proj3, proj2, proj2, cache_k, cache_v, cq, ckm_t, ckc_t, ckn_t)


def _merge_kernel(os_ref, oa_ref, gs_ref, ga_ref, h_ref, wbs_ref, wba_ref, wo_ref, o_ref):
    bs = _dot(os_ref[...], wbs_ref[...])
    ba = _dot(oa_ref[...], wba_ref[...])
    merged = jax.nn.sigmoid(gs_ref[...]) * bs + jax.nn.sigmoid(ga_ref[...]) * ba
    o_ref[...] = h_ref[...] + _dot(merged.astype(BF16), wo_ref[...])


def _merge(o_ssm, o_attn, proj, h, w_bs, w_ba, w_o, *, tm):
    rows = o_ssm.shape[0]
    const = lambda i: (0, 0)
    return pl.pallas_call(
        _merge_kernel,
        grid=(rows // tm,),
        in_specs=[
            pl.BlockSpec((tm, D_SSM), lambda i: (i, 0)),
            pl.BlockSpec((tm, D_ATTN), lambda i: (i, 0)),
            pl.BlockSpec((tm, D_MODEL), lambda i: (i, P_GS // D_MODEL)),
            pl.BlockSpec((tm, D_MODEL), lambda i: (i, P_GA // D_MODEL)),
            pl.BlockSpec((tm, D_MODEL), lambda i: (i, 0)),
            pl.BlockSpec((D_SSM, D_MODEL), const),
            pl.BlockSpec((D_ATTN, D_MODEL), const),
            pl.BlockSpec((D_MODEL, D_MODEL), const),
        ],
        out_specs=pl.BlockSpec((tm, D_MODEL), lambda i: (i, 0)),
        out_shape=jax.ShapeDtypeStruct((rows, D_MODEL), F32),
        compiler_params=_params("parallel"),
        name="merge",
    )(o_ssm, o_attn, proj, proj, h, w_bs, w_ba, w_o)


def kernel(x_prompt, x_sample, cache_k, cache_v, cache_logf, state_ssm, state_conv, meta_tokens, norm_ffn1, ffn1_w_in, ffn1_w_out, norm_mix, w_in, conv_w, conv_b, dt_bias, a_log, d_skip, f_bias, ssm_norm, w_br_ssm, w_br_attn, w_out, norm_ffn2, ffn2_w_in, ffn2_w_out, norm_final):
    assert norm_ffn1.shape[0] == 1, "single-layer trunk"
    bp, seq, _ = x_prompt.shape
    bs, dseq, _ = x_sample.shape
    assert dseq == CHUNK and seq % ATT_T == 0
    n_s = bs * dseq
    rows_small = n_s + 2 * CHUNK
    meta_row0 = n_s + CHUNK - N_META
    meta_blk = meta_row0 // N_META

    row = lambda v: v.reshape(1, -1).astype(F32)
    w = w_in[0]
    w_main = jnp.concatenate([w[:, OFF_XBC:OFF_DT], w[:, :OFF_XBC], w[:, OFF_Q:OFF_F], w[:, OFF_GS:]],
                             axis=1).astype(BF16)
    pad_small = LANES - SSM_HEADS - ATTN_HEADS
    w_small = jnp.concatenate([w[:, OFF_DT:OFF_Q], w[:, OFF_F:OFF_GS], jnp.zeros((D_MODEL, pad_small), F32)],
                              axis=1).astype(BF16)
    b_small = jnp.concatenate([dt_bias[0], f_bias[0], jnp.zeros((pad_small,), F32)]).reshape(1, LANES)
    w1i, w1o = ffn1_w_in[0].astype(BF16), ffn1_w_out[0].astype(BF16)
    w2i, w2o = ffn2_w_in[0].astype(BF16), ffn2_w_out[0].astype(BF16)
    w_bs, w_ba, w_o = w_br_ssm[0].astype(BF16), w_br_attn[0].astype(BF16), w_out[0].astype(BF16)
    expand = jnp.asarray(np.repeat(np.eye(SSM_HEADS, dtype=np.float32), SSM_HEAD_DIM, axis=1), BF16)
    ssd_consts = (conv_w[0], row(conv_b[0]), row(a_log[0]), row(jnp.repeat(d_skip[0], SSM_HEAD_DIM)),
                  row(ssm_norm[0]), expand)

    def front(x, tm_ffn, tm_proj):
        h1, u = _ffn(x, row(norm_ffn1[0]), w1i, w1o, row(norm_mix[0]), tm=tm_ffn, emit_h=True, u_dtype=BF16)
        return h1, _proj(u, w_main, tm=tm_proj), _small(u, w_small, b_small, tm=tm_proj)

    xp = x_prompt.reshape(bp * seq, D_MODEL)
    h1_p, pm_p, ps_p = front(xp, 512, 1024)
    x_small = jnp.concatenate([x_sample.reshape(n_s, D_MODEL), jnp.zeros((CHUNK - N_META, D_MODEL), F32),
                               meta_tokens, jnp.zeros((CHUNK, D_MODEL), F32)], axis=0)
    h1_s, pm_s, ps_s = front(x_small, rows_small // 2, rows_small // 2)

    pm_p3 = pm_p.reshape(bp, seq, P_TOTAL)
    ps_p3 = ps_p.reshape(bp, seq, LANES)
    pm_s3 = pm_s.reshape(rows_small // CHUNK, CHUNK, P_TOTAL)
    ps_s3 = ps_s.reshape(rows_small // CHUNK, CHUNK, LANES)

    zero_state = jnp.zeros((1, D_STATE, D_SSM), F32)
    zero_hist = jnp.zeros((1, CONV_W - 1, D_CONV), F32)
    _, st_m, hist_m = _ssd(pm_s3, ps_s3, zero_state, zero_hist, *ssd_consts,
                           nb=1, b_off=bs, shared_init=True, n_pad=CHUNK - N_META)
    o_ssm_p, st_p, hist_p = _ssd(pm_p3, ps_p3, st_m, hist_m, *ssd_consts,
                                 nb=bp, b_off=0, shared_init=True, n_pad=0)
    st_s0 = jnp.transpose(state_ssm[0].reshape(bs, D_SSM, D_STATE), (0, 2, 1))
    o_ssm_s, st_s, hist_s = _ssd(pm_s3, ps_s3, st_s0, state_conv[0], *ssd_consts,
                                 nb=bs, b_off=0, shared_init=False, n_pad=0)

    lf_m = ps_s[meta_row0:meta_row0 + N_META, S_F:S_F + ATTN_HEADS]
    lf_p = ps_p3[:, :, S_F:S_F + ATTN_HEADS]
    lf_s = ps_s[:n_s, S_F:S_F + ATTN_HEADS].reshape(bs, dseq, ATTN_HEADS)
    logf_prompt = jnp.concatenate([jnp.broadcast_to(lf_m[None], (bp, N_META, ATTN_HEADS)), lf_p], axis=1)
    lf_all_s = jnp.concatenate([jnp.broadcast_to(lf_m[None], (bs, N_META, ATTN_HEADS)),
                                cache_logf[0].astype(F32), lf_s], axis=1)
    past = cache_logf.shape[2]
    pairs = lambda c: c.reshape(c.shape[0], HEAD_PAIRS, 2, c.shape[2])
    c_p = _cum_logf(logf_prompt)
    c_s = _cum_logf(lf_all_s)

    o_attn_p = _attn_prompt(pm_p3, pm_s, meta_blk,
                            jnp.transpose(c_p[:, :, N_META:], (0, 2, 1)),
                            pairs(c_p[:, :, N_META:]), pairs(c_p[:, :, :N_META]))
    o_attn_s = _attn_sample(pm_s3, pm_s, meta_blk,
                            cache_k[0].reshape(bs, past, D_ATTN), cache_v[0].reshape(bs, past, D_ATTN),
                            jnp.transpose(c_s[:, :, N_META + past:], (0, 2, 1)),
                            pairs(c_s[:, :, :N_META]), pairs(c_s[:, :, N_META:N_META + past]),
                            pairs(c_s[:, :, N_META + past:]), nb=bs)

    def back(o_ssm, o_attn, pm, h1, tm):
        h2 = _merge(o_ssm, o_attn, pm, h1, w_bs, w_ba, w_o, tm=tm)
        return _ffn(h2, row(norm_ffn2[0]), w2i, w2o, row(norm_final), tm=tm, emit_h=False, u_dtype=F32)[0]

    y_prompt = back(o_ssm_p.reshape(bp * seq, D_SSM), o_attn_p.reshape(bp * seq, D_ATTN), pm_p, h1_p, 512)
    y_sample = back(o_ssm_s.reshape(n_s, D_SSM), o_attn_s.reshape(n_s, D_ATTN), pm_s, h1_s, 512)

    k_m = pm_s[meta_row0:meta_row0 + N_META, P_K:P_K + D_ATTN]
    v_m = pm_s[meta_row0:meta_row0 + N_META, P_V:P_V + D_ATTN]
    heads = lambda a: a.reshape(1, a.shape[0], a.shape[1], ATTN_HEADS, ATTN_HEAD_DIM)
    k_prompt = heads(jnp.concatenate([jnp.broadcast_to(k_m[None], (bp, N_META, D_ATTN)),
                                      pm_p3[:, :, P_K:P_K + D_ATTN]], axis=1))
    v_prompt = heads(jnp.concatenate([jnp.broadcast_to(v_m[None], (bp, N_META, D_ATTN)),
                                      pm_p3[:, :, P_V:P_V + D_ATTN]], axis=1))
    k_sample = heads(pm_s[:n_s, P_K:P_K + D_ATTN].reshape(bs, dseq, D_ATTN))
    v_sample = heads(pm_s[:n_s, P_V:P_V + D_ATTN].reshape(bs, dseq, D_ATTN))
    state_out = lambda st: jnp.transpose(st, (0, 2, 1)).reshape(1, st.shape[0], SSM_HEADS, SSM_HEAD_DIM, D_STATE)
    return (y_prompt.reshape(bp, seq, D_MODEL), y_sample.reshape(bs, dseq, D_MODEL),
            k_prompt, v_prompt, logf_prompt[None], state_out(st_p), hist_p[None],
            k_sample, v_sample, lf_s[None], state_out(st_s), hist_s[None])
```

```python
import functools

import numpy as np
import jax
import jax.numpy as jnp
from jax import lax
from jax.experimental import pallas as pl
from jax.experimental.pallas import tpu as pltpu

F32 = jnp.float32
BF16 = jnp.bfloat16

D_MODEL = 1024
N_META = 16
CHUNK = 64
EPS = 1e-6
D_SSM = 2048
SSM_HEADS = 32
SSM_HEAD_DIM = 64
SSM_GROUPS = 8
D_STATE = 128
CONV_W = 4
D_CONV = D_SSM + 2 * SSM_GROUPS * D_STATE
ATTN_HEADS = 16
ATTN_HEAD_DIM = 64
D_ATTN = ATTN_HEADS * ATTN_HEAD_DIM
D_FF = 2816
HEAD_PAIRS = ATTN_HEADS // 2
LANES = 128
GROUP_W = D_SSM // SSM_GROUPS

OFF_XBC = D_SSM
OFF_DT = OFF_XBC + D_CONV
OFF_Q = OFF_DT + SSM_HEADS
OFF_K = OFF_Q + D_ATTN
OFF_V = OFF_K + D_ATTN
OFF_F = OFF_V + D_ATTN
OFF_GS = OFF_F + ATTN_HEADS
OFF_GA = OFF_GS + D_MODEL

P_XBC = 0
P_Z = D_CONV
P_Q = P_Z + D_SSM
P_K = P_Q + D_ATTN
P_V = P_K + D_ATTN
P_GS = P_V + D_ATTN
P_GA = P_GS + D_MODEL
P_TOTAL = P_GA + D_MODEL
S_DT = 0
S_F = SSM_HEADS

VMEM_LIMIT = 56 * 1024 * 1024

FF_CHUNK = D_FF // 2
PROJ_TN = P_TOTAL // 4
ATT_T = 512
CUMSUM_BLK = 256


def _dot(a, b):
    return jnp.dot(a, b, preferred_element_type=F32)


def _dot_nt(a, b):
    return lax.dot_general(a, b, (((1,), (1,)), ((), ())), preferred_element_type=F32)


def _dot_tn(a, b):
    return lax.dot_general(a, b, (((0,), (0,)), ((), ())), preferred_element_type=F32)


def _split3(x):
    hi = x.astype(BF16)
    r = x - hi.astype(F32)
    mid = r.astype(BF16)
    lo = (r - mid.astype(F32)).astype(BF16)
    return hi, mid, lo


def _rmsnorm(x, g):
    return x * lax.rsqrt(jnp.mean(x * x, axis=-1, keepdims=True) + EPS) * g


def _softplus(x):
    return jnp.maximum(x, 0.0) + jnp.log1p(jnp.exp(-jnp.abs(x)))


def _params(*sem):
    return pltpu.CompilerParams(dimension_semantics=sem, vmem_limit_bytes=VMEM_LIMIT)


def _ffn_kernel(x_ref, g1_ref, wa_ref, wb_ref, wo_ref, g2_ref, *refs, emit_h):
    if emit_h:
        h_ref, u_ref, xn_sc, acc_sc = refs
    else:
        u_ref, xn_sc, acc_sc = refs
    c = pl.program_id(1)

    @pl.when(c == 0)
    def _():
        xn_sc[...] = _rmsnorm(x_ref[...], g1_ref[...]).astype(BF16)
        acc_sc[...] = jnp.zeros_like(acc_sc)

    xn = xn_sc[...]
    a = _dot(xn, wa_ref[...])
    b = _dot(xn, wb_ref[...])
    g = (a * jax.nn.sigmoid(a) * b).astype(BF16)
    acc_sc[...] += _dot(g, wo_ref[...])

    @pl.when(c == pl.num_programs(1) - 1)
    def _():
        h = x_ref[...] + 0.5 * acc_sc[...]
        if emit_h:
            h_ref[...] = h
        u_ref[...] = _rmsnorm(h, g2_ref[...]).astype(u_ref.dtype)


def _ffn(x, g1, w_in, w_out, g2, *, tm, emit_h, u_dtype):
    rows = x.shape[0]
    nfc = D_FF // FF_CHUNK
    row_spec = pl.BlockSpec((tm, D_MODEL), lambda i, c: (i, 0))
    vec_spec = pl.BlockSpec((1, D_MODEL), lambda i, c: (0, 0))
    out_shape = [jax.ShapeDtypeStruct((rows, D_MODEL), u_dtype)]
    out_specs = [row_spec]
    if emit_h:
        out_shape = [jax.ShapeDtypeStruct((rows, D_MODEL), F32)] + out_shape
        out_specs = [row_spec] + out_specs
    return pl.pallas_call(
        functools.partial(_ffn_kernel, emit_h=emit_h),
        grid=(rows // tm, nfc),
        in_specs=[
            row_spec, vec_spec,
            pl.BlockSpec((D_MODEL, FF_CHUNK), lambda i, c: (0, c)),
            pl.BlockSpec((D_MODEL, FF_CHUNK), lambda i, c: (0, nfc + c)),
            pl.BlockSpec((FF_CHUNK, D_MODEL), lambda i, c: (c, 0)),
            vec_spec,
        ],
        out_specs=out_specs,
        out_shape=out_shape,
        scratch_shapes=[pltpu.VMEM((tm, D_MODEL), BF16), pltpu.VMEM((tm, D_MODEL), F32)],
        compiler_params=_params("parallel", "arbitrary"),
        name="ffn",
    )(x, g1, w_in, w_in, w_out, g2)


def _proj_kernel(u_ref, w_ref, o_ref):
    o_ref[...] = _dot(u_ref[...], w_ref[...])


def _proj(u, w, *, tm):
    rows = u.shape[0]
    return pl.pallas_call(
        _proj_kernel,
        grid=(P_TOTAL // PROJ_TN, rows // tm),
        in_specs=[pl.BlockSpec((tm, D_MODEL), lambda j, i: (i, 0)),
                  pl.BlockSpec((D_MODEL, PROJ_TN), lambda j, i: (0, j))],
        out_specs=pl.BlockSpec((tm, PROJ_TN), lambda j, i: (i, j)),
        out_shape=jax.ShapeDtypeStruct((rows, P_TOTAL), F32),
        compiler_params=_params("parallel", "parallel"),
        name="proj",
    )(u, w)


def _small_kernel(u_ref, w_ref, b_ref, o_ref):
    x = _dot(u_ref[...], w_ref[...]) + b_ref[...]
    lane = lax.broadcasted_iota(jnp.int32, (1, LANES), 1)
    sgn = jnp.where(lane < S_F, 1.0, -1.0)
    o_ref[...] = sgn * _softplus(sgn * x)


def _small(u, w, b, *, tm):
    rows = u.shape[0]
    return pl.pallas_call(
        _small_kernel,
        grid=(rows // tm,),
        in_specs=[pl.BlockSpec((tm, D_MODEL), lambda i: (i, 0)),
                  pl.BlockSpec((D_MODEL, LANES), lambda i: (0, 0)),
                  pl.BlockSpec((1, LANES), lambda i: (0, 0))],
        out_specs=pl.BlockSpec((tm, LANES), lambda i: (i, 0)),
        out_shape=jax.ShapeDtypeStruct((rows, LANES), F32),
        compiler_params=_params("parallel"),
        name="small",
    )(u, w, b)


HIST0 = 8 - (CONV_W - 1)


def _ssd_kernel(xbc_ref, z_ref, dt_ref, s0_ref, h0_ref, cw_ref, cb_ref, alog_ref, dskip_ref, norm_ref,
                e_ref, o_ref, sT_ref, hT_ref, st_sc, xp_sc, y_sc, *, n_pad):
    T = CHUNK
    c = pl.program_id(1)

    @pl.when(c == 0)
    def _():
        st_sc[...] = s0_ref[...]
        xp_sc[HIST0:8, :] = h0_ref[...]

    xraw = xbc_ref[...]
    xp_sc[8:8 + T, :] = xraw
    cw = cw_ref[...]
    conv = cb_ref[...] + cw[0:1] * xp_sc[HIST0:HIST0 + T, :]
    conv = conv + cw[1:2] * xp_sc[HIST0 + 1:HIST0 + 1 + T, :]
    conv = conv + cw[2:3] * xp_sc[HIST0 + 2:HIST0 + 2 + T, :]
    conv = conv + cw[3:4] * xraw
    xp_sc[HIST0:8, :] = xraw[T - (CONV_W - 1):T, :]
    xc = conv * jax.nn.sigmoid(conv)
    dtv = dt_ref[:, S_DT:S_DT + SSM_HEADS]
    if n_pad:
        valid = lax.broadcasted_iota(jnp.int32, (T, 1), 0) >= n_pad
        xc = jnp.where(valid, xc, 0.0)
        dtv = jnp.where(valid, dtv, 0.0)

    a = -jnp.exp(alog_ref[...])
    adt = a * dtv
    row = lax.broadcasted_iota(jnp.int32, (T, T), 0)
    col = lax.broadcasted_iota(jnp.int32, (T, T), 1)
    causal = col <= row
    tril = jnp.where(causal, 1.0, 0.0).astype(BF16)
    r32 = lax.broadcasted_iota(jnp.int32, (SSM_HEADS, SSM_HEADS), 0)
    c32 = lax.broadcasted_iota(jnp.int32, (SSM_HEADS, SSM_HEADS), 1)
    eye = jnp.where(r32 == c32, 1.0, 0.0).astype(BF16)
    expand = e_ref[...]

    a_cs = sum(_dot(tril, piece) for piece in _split3(adt))
    a_cs_t = sum(_dot_nt(eye, piece) for piece in _split3(a_cs))
    dt_e = sum(_dot(piece, expand) for piece in _split3(dtv))
    acs_e = sum(_dot(piece, expand) for piece in _split3(a_cs))

    xs = xc[:, :D_SSM]
    bm = xc[:, D_SSM:D_SSM + SSM_GROUPS * D_STATE].astype(BF16)
    cm = xc[:, D_SSM + SSM_GROUPS * D_STATE:].astype(BF16)
    xd = xs * dt_e
    al_e = acs_e[T - 1:T, :]
    xdw = (xd * jnp.exp(al_e - acs_e)).astype(BF16)
    cdec = jnp.exp(al_e)
    eacs = jnp.exp(acs_e)
    lane = lax.broadcasted_iota(jnp.int32, (1, D_SSM), 1)
    lo = (lane & ATTN_HEAD_DIM) == 0
    xd_lo = jnp.where(lo, xd, 0.0).astype(BF16)
    xd_hi = jnp.where(lo, 0.0, xd).astype(BF16)

    for g in range(SSM_GROUPS):
        gs = slice(g * GROUP_W, (g + 1) * GROUP_W)
        cmg = cm[:, g * D_STATE:(g + 1) * D_STATE]
        bmg = bm[:, g * D_STATE:(g + 1) * D_STATE]
        cb = _dot_nt(cmg, bmg)
        stg = st_sc[:, gs]
        yoff = _dot(cmg, stg.astype(BF16))
        for rr in range(2):
            pair = g * 2 + rr
            ps = slice(pair * LANES, (pair + 1) * LANES)
            ydiag = None
            for hh in range(2):
                h = pair * 2 + hh
                seg = a_cs[:, h:h + 1] - a_cs_t[h:h + 1, :]
                m = (cb * jnp.exp(jnp.where(causal, seg, -jnp.inf))).astype(BF16)
                d = _dot(m, (xd_lo, xd_hi)[hh][:, ps])
                ydiag = d if ydiag is None else ydiag + d
            y_sc[:, ps] = ydiag + yoff[:, rr * LANES:(rr + 1) * LANES] * eacs[:, ps]
        st_sc[:, gs] = stg * cdec[:, gs] + _dot_tn(bmg, xdw[:, gs])

    y = y_sc[...] + dskip_ref[...] * xs
    zz = z_ref[...]
    y = y * (zz * jax.nn.sigmoid(zz))
    for g in range(SSM_GROUPS):
        gs = slice(g * GROUP_W, (g + 1) * GROUP_W)
        yg = y[:, gs]
        yg = yg * lax.rsqrt(jnp.mean(yg * yg, axis=-1, keepdims=True) + EPS)
        o_ref[:, gs] = (yg * norm_ref[:, gs]).astype(o_ref.dtype)

    @pl.when(c == pl.num_programs(1) - 1)
    def _():
        sT_ref[...] = st_sc[...]
        hT_ref[...] = xp_sc[HIST0:8, :]


def _ssd(proj3, small3, s0, h0, cw, cb, alog, dskip, norm, expand, *, nb, b_off, shared_init, n_pad):
    nc = proj3.shape[1] // CHUNK
    init = (lambda b, c: (0, 0, 0)) if shared_init else (lambda b, c: (b, 0, 0))
    const = lambda b, c: (0, 0)
    return pl.pallas_call(
        functools.partial(_ssd_kernel, n_pad=n_pad),
        grid=(nb, nc),
        in_specs=[
            pl.BlockSpec((None, CHUNK, D_CONV), lambda b, c: (b + b_off, c, P_XBC // D_CONV)),
            pl.BlockSpec((None, CHUNK, D_SSM), lambda b, c: (b + b_off, c, P_Z // D_SSM)),
            pl.BlockSpec((None, CHUNK, LANES), lambda b, c: (b + b_off, c, 0)),
            pl.BlockSpec((None, D_STATE, D_SSM), init),
            pl.BlockSpec((None, CONV_W - 1, D_CONV), init),
            pl.BlockSpec((CONV_W, D_CONV), const),
            pl.BlockSpec((1, D_CONV), const),
            pl.BlockSpec((1, SSM_HEADS), const),
            pl.BlockSpec((1, D_SSM), const),
            pl.BlockSpec((1, D_SSM), const),
            pl.BlockSpec((SSM_HEADS, D_SSM), const),
        ],
        out_specs=[
            pl.BlockSpec((None, CHUNK, D_SSM), lambda b, c: (b, c, 0)),
            pl.BlockSpec((None, D_STATE, D_SSM), lambda b, c: (b, 0, 0)),
            pl.BlockSpec((None, CONV_W - 1, D_CONV), lambda b, c: (b, 0, 0)),
        ],
        out_shape=[
            jax.ShapeDtypeStruct((nb, nc * CHUNK, D_SSM), BF16),
            jax.ShapeDtypeStruct((nb, D_STATE, D_SSM), F32),
            jax.ShapeDtypeStruct((nb, CONV_W - 1, D_CONV), F32),
        ],
        scratch_shapes=[
            pltpu.VMEM((D_STATE, D_SSM), F32),
            pltpu.VMEM((8 + CHUNK, D_CONV), F32),
            pltpu.VMEM((CHUNK, D_SSM), F32),
        ],
        compiler_params=_params("parallel", "arbitrary"),
        name="ssd",
    )(proj3, proj3, small3, s0, h0, cw, cb, alog, dskip, norm, expand)


def _cumsum_kernel(x_ref, o_ref):
    n = x_ref.shape[-1] // CUMSUM_BLK
    r = lax.broadcasted_iota(jnp.int32, (CUMSUM_BLK, CUMSUM_BLK), 0)
    c = lax.broadcasted_iota(jnp.int32, (CUMSUM_BLK, CUMSUM_BLK), 1)
    upper = jnp.where(r <= c, 1.0, 0.0).astype(BF16)
    carry = jnp.zeros((ATTN_HEADS, 1), F32)
    for i in range(n):
        blk = slice(i * CUMSUM_BLK, (i + 1) * CUMSUM_BLK)
        cs = sum(_dot(piece, upper) for piece in _split3(x_ref[:, blk])) + carry
        o_ref[:, blk] = cs
        carry = cs[:, CUMSUM_BLK - 1:CUMSUM_BLK]


def _cumsum_t(x_t):
    b, h, l = x_t.shape
    spec = pl.BlockSpec((None, h, l), lambda i: (i, 0, 0))
    return pl.pallas_call(
        _cumsum_kernel, grid=(b,), in_specs=[spec], out_specs=spec,
        out_shape=jax.ShapeDtypeStruct(x_t.shape, F32),
        compiler_params=_params("parallel"), name="cumsum",
    )(x_t)


def _cum_logf(lf_all):
    l = lf_all.shape[1]
    lp = -(-l // CUMSUM_BLK) * CUMSUM_BLK
    x_t = jnp.pad(jnp.transpose(lf_all, (0, 2, 1)), ((0, 0), (0, 0), (0, lp - l)))
    return _cumsum_t(x_t)[:, :, :l]


def _pick_head_col(cq_tile, head):
    hl = lax.broadcasted_iota(jnp.int32, (1, ATTN_HEADS), 1)
    return jnp.sum(jnp.where(hl == head, cq_tile, 0.0), axis=1, keepdims=True)


BIAS_PIECES = 3


def _bias_lane0(head):
    return ATTN_HEAD_DIM if head % 2 == 0 else 0


def _bias_placement():
    sq = np.zeros((BIAS_PIECES, ATTN_HEADS, ATTN_HEADS * LANES), np.float32)
    sk = np.zeros_like(sq)
    one_q = np.zeros((1, ATTN_HEADS * LANES), np.float32)
    one_k = np.zeros_like(one_q)
    for h in range(ATTN_HEADS):
        base = h * LANES + _bias_lane0(h)
        for piece in range(BIAS_PIECES):
            sk[piece, h, base + piece] = -1.0
            one_q[0, base + piece] = 1.0
            sq[piece, h, base + BIAS_PIECES + piece] = 1.0
            one_k[0, base + BIAS_PIECES + piece] = 1.0
    return jnp.asarray(sq, BF16), jnp.asarray(sk, BF16), jnp.asarray(one_q), jnp.asarray(one_k)


def _attn_prep_kernel(q_ref, k_ref, v_ref, c_ref, sq_ref, sk_ref, oneq_ref, onek_ref, qo_ref, ko_ref, vt_ref):
    pieces = _split3(c_ref[...])
    bias_q = sum(_dot(pc, sq_ref[n]) for n, pc in enumerate(pieces)) + oneq_ref[...]
    bias_k = sum(_dot(pc, sk_ref[n]) for n, pc in enumerate(pieces)) + onek_ref[...]
    lane = lax.broadcasted_iota(jnp.int32, (1, LANES), 1)
    lo = lane < ATTN_HEAD_DIM
    for p in range(HEAD_PAIRS):
        ps = slice(p * LANES, (p + 1) * LANES)
        q2 = q_ref[:, ps] * (ATTN_HEAD_DIM ** -0.5)
        k2 = k_ref[:, ps]
        for hh in range(2):
            h = 2 * p + hh
            hs = slice(h * LANES, (h + 1) * LANES)
            sel = lo if hh == 0 else jnp.logical_not(lo)
            qo_ref[h] = (jnp.where(sel, q2, 0.0) + bias_q[:, hs]).astype(BF16)
            ko_ref[h] = (jnp.where(sel, k2, 0.0) + bias_k[:, hs]).astype(BF16)
        v_t = jnp.transpose(v_ref[:, ps]).astype(BF16)
        vt_ref[2 * p] = v_t[:ATTN_HEAD_DIM]
        vt_ref[2 * p + 1] = v_t[ATTN_HEAD_DIM:]


def _attn_prep(proj3, c_rows, consts, *, nb, b_off, tile):
    l = c_rows.shape[1]
    col = lambda off: off // D_ATTN
    sq, sk, one_q, one_k = consts
    c3 = lambda b, i: (0, 0, 0)
    c2 = lambda b, i: (0, 0)
    return pl.pallas_call(
        _attn_prep_kernel,
        grid=(nb, l // tile),
        in_specs=[
            pl.BlockSpec((None, tile, D_ATTN), lambda b, i: (b + b_off, i, col(P_Q))),
            pl.BlockSpec((None, tile, D_ATTN), lambda b, i: (b + b_off, i, col(P_K))),
            pl.BlockSpec((None, tile, D_ATTN), lambda b, i: (b + b_off, i, col(P_V))),
            pl.BlockSpec((None, tile, ATTN_HEADS), lambda b, i: (b, i, 0)),
            pl.BlockSpec(sq.shape, c3), pl.BlockSpec(sk.shape, c3),
            pl.BlockSpec(one_q.shape, c2), pl.BlockSpec(one_k.shape, c2),
        ],
        out_specs=[
            pl.BlockSpec((None, ATTN_HEADS, tile, LANES), lambda b, i: (b, 0, i, 0)),
            pl.BlockSpec((None, ATTN_HEADS, tile, LANES), lambda b, i: (b, 0, i, 0)),
            pl.BlockSpec((None, ATTN_HEADS, ATTN_HEAD_DIM, tile), lambda b, i: (b, 0, 0, i)),
        ],
        out_shape=[
            jax.ShapeDtypeStruct((nb, ATTN_HEADS, l, LANES), BF16),
            jax.ShapeDtypeStruct((nb, ATTN_HEADS, l, LANES), BF16),
            jax.ShapeDtypeStruct((nb, ATTN_HEADS, ATTN_HEAD_DIM, l), BF16),
        ],
        compiler_params=_params("parallel", "parallel"),
        name="attn_prep",
    )(proj3, proj3, proj3, c_rows, sq, sk, one_q, one_k)


QBLK = 256


def _attn_prompt_kernel(it_ref, jt_ref, q_ref, k_ref, vt_ref, km_ref, vtm_ref, o_ref, m_sc, l_sc, acc_sc,
                        *, meta_lo, meta_hi):
    t = pl.program_id(1)
    i = it_ref[t]
    j = jt_ref[t]
    nqb = ATT_T // QBLK

    def q_block(h, r):
        return q_ref[h, r * QBLK:(r + 1) * QBLK, :]

    def head_update(h, scores, first):
        if not first:
            m_old_all, l_old_all, acc_old_all = m_sc[h], l_sc[h], acc_sc[h]
        m_out, l_out, acc_out = [], [], []
        for r, (s_t, v_t) in enumerate(scores):
            qs = slice(r * QBLK, (r + 1) * QBLK)
            mx = jnp.max(s_t, axis=0, keepdims=True)
            if first:
                m_new = mx
                pr = jnp.exp(s_t - m_new)
                l_new = jnp.sum(pr, axis=0, keepdims=True)
                acc_new = _dot(v_t, pr.astype(BF16))
            else:
                m_old = m_old_all[:, qs]
                m_new = jnp.maximum(m_old, mx)
                alpha = jnp.exp(m_old - m_new)
                pr = jnp.exp(s_t - m_new)
                l_new = alpha * l_old_all[:, qs] + jnp.sum(pr, axis=0, keepdims=True)
                acc_new = acc_old_all[:, qs] * alpha + _dot(v_t, pr.astype(BF16))
            m_out.append(m_new)
            l_out.append(l_new)
            acc_out.append(acc_new)
        m_sc[h] = jnp.concatenate(m_out, axis=1)
        l_sc[h] = jnp.concatenate(l_out, axis=1)
        acc_sc[h] = jnp.concatenate(acc_out, axis=1)

    def run(score_fn, first):
        nxt = [score_fn(0, r) for r in range(nqb)]
        for h in range(ATTN_HEADS):
            cur = nxt
            if h + 1 < ATTN_HEADS:
                nxt = [score_fn(h + 1, r) for r in range(nqb)]
            head_update(h, cur, first)

    @pl.when(j == 0)
    def _():
        krow = lax.broadcasted_iota(jnp.int32, (km_ref.shape[1], 1), 0)
        is_meta = jnp.logical_and(krow >= meta_lo, krow < meta_hi)
        run(lambda h, r: (jnp.where(is_meta, _dot_nt(km_ref[h], q_block(h, r)), -jnp.inf), vtm_ref[h]), True)

    @pl.when(j < i)
    def _():
        run(lambda h, r: (_dot_nt(k_ref[h], q_block(h, r)), vt_ref[h]), False)

    @pl.when(j == i)
    def _():
        def diag_scores(h, r):
            nk = (r + 1) * QBLK
            s_t = _dot_nt(k_ref[h, :nk, :], q_block(h, r))
            krow = lax.broadcasted_iota(jnp.int32, (nk, QBLK), 0)
            qcol = lax.broadcasted_iota(jnp.int32, (nk, QBLK), 1) + r * QBLK
            return jnp.where(krow <= qcol, s_t, -jnp.inf), vt_ref[h, :, :nk]

        run(diag_scores, False)
        for p in range(HEAD_PAIRS):
            o_t = jnp.concatenate([acc_sc[2 * p] / l_sc[2 * p], acc_sc[2 * p + 1] / l_sc[2 * p + 1]], axis=0)
            o_ref[:, p * LANES:(p + 1) * LANES] = jnp.transpose(o_t).astype(o_ref.dtype)


def _attn_prompt(q_aug, k_aug, v_t, km_aug, vm_t, *, meta_lo):
    nb, _, l, _ = q_aug.shape
    nt = l // ATT_T
    it = np.array([i for i in range(nt) for _ in range(i + 1)], np.int32)
    jt = np.array([j for i in range(nt) for j in range(i + 1)], np.int32)
    mrows = km_aug.shape[2]
    grid_spec = pltpu.PrefetchScalarGridSpec(
        num_scalar_prefetch=2,
        grid=(nb, len(it)),
        in_specs=[
            pl.BlockSpec((None, ATTN_HEADS, ATT_T, LANES), lambda b, t, it, jt: (b, 0, it[t], 0)),
            pl.BlockSpec((None, ATTN_HEADS, ATT_T, LANES), lambda b, t, it, jt: (b, 0, jt[t], 0)),
            pl.BlockSpec((None, ATTN_HEADS, ATTN_HEAD_DIM, ATT_T), lambda b, t, it, jt: (b, 0, 0, jt[t])),
            pl.BlockSpec((None, ATTN_HEADS, mrows, LANES), lambda b, t, it, jt: (0, 0, 0, 0)),
            pl.BlockSpec((None, ATTN_HEADS, ATTN_HEAD_DIM, mrows), lambda b, t, it, jt: (0, 0, 0, 0)),
        ],
        out_specs=pl.BlockSpec((None, ATT_T, D_ATTN), lambda b, t, it, jt: (b, it[t], 0)),
        scratch_shapes=[
            pltpu.VMEM((ATTN_HEADS, 1, ATT_T), F32),
            pltpu.VMEM((ATTN_HEADS, 1, ATT_T), F32),
            pltpu.VMEM((ATTN_HEADS, ATTN_HEAD_DIM, ATT_T), F32),
        ],
    )
    return pl.pallas_call(
        functools.partial(_attn_prompt_kernel, meta_lo=meta_lo, meta_hi=meta_lo + N_META),
        grid_spec=grid_spec,
        out_shape=jax.ShapeDtypeStruct((nb, l, D_ATTN), BF16),
        compiler_params=_params("parallel", "arbitrary"),
        name="attn_prompt",
    )(jnp.asarray(it), jnp.asarray(jt), q_aug, k_aug, v_t, km_aug, vm_t)


def _attn_sample_kernel(q_ref, kn_ref, vn_ref, km_ref, vm_ref, kc_ref, vc_ref, cq_ref, ckm_ref, ckc_ref,
                        ckn_ref, o_ref):
    p = pl.program_id(1)
    T = CHUNK
    lane = lax.broadcasted_iota(jnp.int32, (1, LANES), 1)
    lo = lane < ATTN_HEAD_DIM
    q2 = q_ref[...] * (ATTN_HEAD_DIM ** -0.5)
    cqt = cq_ref[...]
    km = km_ref[...].astype(BF16)
    kc = kc_ref[...].astype(BF16)
    kn = kn_ref[...].astype(BF16)
    vm = vm_ref[...]
    vc = vc_ref[...]
    vn = vn_ref[...]
    row = lax.broadcasted_iota(jnp.int32, (T, T), 0)
    col = lax.broadcasted_iota(jnp.int32, (T, T), 1)
    visible = col <= row
    out = None
    for hh in range(2):
        sel = lo if hh == 0 else jnp.logical_not(lo)
        qh = jnp.where(sel, q2, 0.0).astype(BF16)
        cq = _pick_head_col(cqt, 2 * p + hh)
        s_m = _dot_nt(qh, km) + cq - ckm_ref[hh:hh + 1, :]
        s_c = _dot_nt(qh, kc) + cq - ckc_ref[hh:hh + 1, :]
        s_n = jnp.where(visible, _dot_nt(qh, kn) + cq - ckn_ref[hh:hh + 1, :], -jnp.inf)
        m = jnp.maximum(jnp.maximum(jnp.max(s_m, axis=1, keepdims=True), jnp.max(s_c, axis=1, keepdims=True)),
                        jnp.max(s_n, axis=1, keepdims=True))
        p_m = jnp.exp(s_m - m)
        p_c = jnp.exp(s_c - m)
        p_n = jnp.exp(s_n - m)
        l = (jnp.sum(p_m, axis=1, keepdims=True) + jnp.sum(p_c, axis=1, keepdims=True)
             + jnp.sum(p_n, axis=1, keepdims=True))
        o = (_dot(p_m.astype(BF16), jnp.where(sel, vm, 0.0).astype(BF16))
             + _dot(p_c.astype(BF16), jnp.where(sel, vc, 0.0).astype(BF16))
             + _dot(p_n.astype(BF16), jnp.where(sel, vn, 0.0).astype(BF16))) / l
        out = o if out is None else out + o
    o_ref[...] = out.astype(o_ref.dtype)


def _attn_sample(proj3, proj2, meta_blk, cache_k, cache_v, cq, ckm_t, ckc_t, ckn_t, *, nb):
    past = cache_k.shape[1]
    lb = lambda off: off // LANES
    return pl.pallas_call(
        _attn_sample_kernel,
        grid=(nb, HEAD_PAIRS),
        in_specs=[
            pl.BlockSpec((None, CHUNK, LANES), lambda b, p: (b, 0, lb(P_Q) + p)),
            pl.BlockSpec((None, CHUNK, LANES), lambda b, p: (b, 0, lb(P_K) + p)),
            pl.BlockSpec((None, CHUNK, LANES), lambda b, p: (b, 0, lb(P_V) + p)),
            pl.BlockSpec((N_META, LANES), lambda b, p: (meta_blk, lb(P_K) + p)),
            pl.BlockSpec((N_META, LANES), lambda b, p: (meta_blk, lb(P_V) + p)),
            pl.BlockSpec((None, past, LANES), lambda b, p: (b, 0, p)),
            pl.BlockSpec((None, past, LANES), lambda b, p: (b, 0, p)),
            pl.BlockSpec((None, CHUNK, ATTN_HEADS), lambda b, p: (b, 0, 0)),
            pl.BlockSpec((None, None, 2, N_META), lambda b, p: (b, p, 0, 0)),
            pl.BlockSpec((None, None, 2, past), lambda b, p: (b, p, 0, 0)),
            pl.BlockSpec((None, None, 2, CHUNK), lambda b, p: (b, p, 0, 0)),
        ],
        out_specs=pl.BlockSpec((None, CHUNK, LANES), lambda b, p: (b, 0, p)),
        out_shape=jax.ShapeDtypeStruct((nb, CHUNK, D_ATTN), BF16),
        compiler_params=_params("parallel", "parallel"),
        name="attn_sample",
    )(proj3, proj3, proj3, proj2, proj2, cache_k, cache_v, cq, ckm_t, ckc_t, ckn_t)


def _merge_kernel(os_ref, oa_ref, gs_ref, ga_ref, h_ref, wbs_ref, wba_ref, wo_ref, o_ref):
    bs = _dot(os_ref[...], wbs_ref[...])
    ba = _dot(oa_ref[...], wba_ref[...])
    merged = jax.nn.sigmoid(gs_ref[...]) * bs + jax.nn.sigmoid(ga_ref[...]) * ba
    o_ref[...] = h_ref[...] + _dot(merged.astype(BF16), wo_ref[...])


def _merge(o_ssm, o_attn, proj, h, w_bs, w_ba, w_o, *, tm):
    rows = o_ssm.shape[0]
    const = lambda i: (0, 0)
    return pl.pallas_call(
        _merge_kernel,
        grid=(rows // tm,),
        in_specs=[
            pl.BlockSpec((tm, D_SSM), lambda i: (i, 0)),
            pl.BlockSpec((tm, D_ATTN), lambda i: (i, 0)),
            pl.BlockSpec((tm, D_MODEL), lambda i: (i, P_GS // D_MODEL)),
            pl.BlockSpec((tm, D_MODEL), lambda i: (i, P_GA // D_MODEL)),
            pl.BlockSpec((tm, D_MODEL), lambda i: (i, 0)),
            pl.BlockSpec((D_SSM, D_MODEL), const),
            pl.BlockSpec((D_ATTN, D_MODEL), const),
            pl.BlockSpec((D_MODEL, D_MODEL), const),
        ],
        out_specs=pl.BlockSpec((tm, D_MODEL), lambda i: (i, 0)),
        out_shape=jax.ShapeDtypeStruct((rows, D_MODEL), F32),
        compiler_params=_params("parallel"),
        name="merge",
    )(o_ssm, o_attn, proj, proj, h, w_bs, w_ba, w_o)


def kernel(x_prompt, x_sample, cache_k, cache_v, cache_logf, state_ssm, state_conv, meta_tokens, norm_ffn1, ffn1_w_in, ffn1_w_out, norm_mix, w_in, conv_w, conv_b, dt_bias, a_log, d_skip, f_bias, ssm_norm, w_br_ssm, w_br_attn, w_out, norm_ffn2, ffn2_w_in, ffn2_w_out, norm_final):
    assert norm_ffn1.shape[0] == 1, "single-layer trunk"
    bp, seq, _ = x_prompt.shape
    bs, dseq, _ = x_sample.shape
    assert dseq == CHUNK and seq % ATT_T == 0
    n_s = bs * dseq
    rows_small = n_s + 2 * CHUNK
    meta_row0 = n_s + CHUNK - N_META
    meta_blk = meta_row0 // N_META

    row = lambda v: v.reshape(1, -1).astype(F32)
    w = w_in[0]
    w_main = jnp.concatenate([w[:, OFF_XBC:OFF_DT], w[:, :OFF_XBC], w[:, OFF_Q:OFF_F], w[:, OFF_GS:]],
                             axis=1).astype(BF16)
    pad_small = LANES - SSM_HEADS - ATTN_HEADS
    w_small = jnp.concatenate([w[:, OFF_DT:OFF_Q], w[:, OFF_F:OFF_GS], jnp.zeros((D_MODEL, pad_small), F32)],
                              axis=1).astype(BF16)
    b_small = jnp.concatenate([dt_bias[0], f_bias[0], jnp.zeros((pad_small,), F32)]).reshape(1, LANES)
    w1i, w1o = ffn1_w_in[0].astype(BF16), ffn1_w_out[0].astype(BF16)
    w2i, w2o = ffn2_w_in[0].astype(BF16), ffn2_w_out[0].astype(BF16)
    w_bs, w_ba, w_o = w_br_ssm[0].astype(BF16), w_br_attn[0].astype(BF16), w_out[0].astype(BF16)
    expand = jnp.asarray(np.repeat(np.eye(SSM_HEADS, dtype=np.float32), SSM_HEAD_DIM, axis=1), BF16)
    ssd_consts = (conv_w[0], row(conv_b[0]), row(a_log[0]), row(jnp.repeat(d_skip[0], SSM_HEAD_DIM)),
                  row(ssm_norm[0]), expand)

    def front(x, tm_ffn, tm_proj):
        h1, u = _ffn(x, row(norm_ffn1[0]), w1i, w1o, row(norm_mix[0]), tm=tm_ffn, emit_h=True, u_dtype=BF16)
        return h1, _proj(u, w_main, tm=tm_proj), _small(u, w_small, b_small, tm=tm_proj)

    xp = x_prompt.reshape(bp * seq, D_MODEL)
    h1_p, pm_p, ps_p = front(xp, 512, 1024)
    x_small = jnp.concatenate([x_sample.reshape(n_s, D_MODEL), jnp.zeros((CHUNK - N_META, D_MODEL), F32),
                               meta_tokens, jnp.zeros((CHUNK, D_MODEL), F32)], axis=0)
    h1_s, pm_s, ps_s = front(x_small, rows_small // 2, rows_small // 2)

    pm_p3 = pm_p.reshape(bp, seq, P_TOTAL)
    ps_p3 = ps_p.reshape(bp, seq, LANES)
    pm_s3 = pm_s.reshape(rows_small // CHUNK, CHUNK, P_TOTAL)
    ps_s3 = ps_s.reshape(rows_small // CHUNK, CHUNK, LANES)

    zero_state = jnp.zeros((1, D_STATE, D_SSM), F32)
    zero_hist = jnp.zeros((1, CONV_W - 1, D_CONV), F32)
    _, st_m, hist_m = _ssd(pm_s3, ps_s3, zero_state, zero_hist, *ssd_consts,
                           nb=1, b_off=bs, shared_init=True, n_pad=CHUNK - N_META)
    o_ssm_p, st_p, hist_p = _ssd(pm_p3, ps_p3, st_m, hist_m, *ssd_consts,
                                 nb=bp, b_off=0, shared_init=True, n_pad=0)
    st_s0 = jnp.transpose(state_ssm[0].reshape(bs, D_SSM, D_STATE), (0, 2, 1))
    o_ssm_s, st_s, hist_s = _ssd(pm_s3, ps_s3, st_s0, state_conv[0], *ssd_consts,
                                 nb=bs, b_off=0, shared_init=False, n_pad=0)

    lf_m = ps_s[meta_row0:meta_row0 + N_META, S_F:S_F + ATTN_HEADS]
    lf_p = ps_p3[:, :, S_F:S_F + ATTN_HEADS]
    lf_s = ps_s[:n_s, S_F:S_F + ATTN_HEADS].reshape(bs, dseq, ATTN_HEADS)
    logf_prompt = jnp.concatenate([jnp.broadcast_to(lf_m[None], (bp, N_META, ATTN_HEADS)), lf_p], axis=1)
    lf_all_s = jnp.concatenate([jnp.broadcast_to(lf_m[None], (bs, N_META, ATTN_HEADS)),
                                cache_logf[0].astype(F32), lf_s], axis=1)
    past = cache_logf.shape[2]
    pairs = lambda c: c.reshape(c.shape[0], HEAD_PAIRS, 2, c.shape[2])
    c_p = _cum_logf(logf_prompt)
    c_s = _cum_logf(lf_all_s)

    prep_consts = _bias_placement()
    q_aug, k_aug, v_t = _attn_prep(pm_p3, jnp.transpose(c_p[:, :, N_META:], (0, 2, 1)), prep_consts,
                                   nb=bp, b_off=0, tile=ATT_T)
    meta_tile = 2 * CHUNK
    meta_lo = meta_row0 - n_s
    c_meta = jnp.pad(jnp.transpose(c_p[:1, :, :N_META], (0, 2, 1)),
                     ((0, 0), (meta_lo, meta_tile - meta_lo - N_META), (0, 0)))
    _, km_aug, vm_t = _attn_prep(pm_s.reshape(rows_small // meta_tile, meta_tile, P_TOTAL), c_meta, prep_consts,
                                 nb=1, b_off=n_s // meta_tile, tile=meta_tile)
    o_attn_p = _attn_prompt(q_aug, k_aug, v_t, km_aug, vm_t, meta_lo=meta_lo)
    o_attn_s = _attn_sample(pm_s3, pm_s, meta_blk,
                            cache_k[0].reshape(bs, past, D_ATTN), cache_v[0].reshape(bs, past, D_ATTN),
                            jnp.transpose(c_s[:, :, N_META + past:], (0, 2, 1)),
                            pairs(c_s[:, :, :N_META]), pairs(c_s[:, :, N_META:N_META + past]),
                            pairs(c_s[:, :, N_META + past:]), nb=bs)

    def back(o_ssm, o_attn, pm, h1, tm):
        h2 = _merge(o_ssm, o_attn, pm, h1, w_bs, w_ba, w_o, tm=tm)
        return _ffn(h2, row(norm_ffn2[0]), w2i, w2o, row(norm_final), tm=tm, emit_h=False, u_dtype=F32)[0]

    y_prompt = back(o_ssm_p.reshape(bp * seq, D_SSM), o_attn_p.reshape(bp * seq, D_ATTN), pm_p, h1_p, 512)
    y_sample = back(o_ssm_s.reshape(n_s, D_SSM), o_attn_s.reshape(n_s, D_ATTN), pm_s, h1_s, 512)

    k_m = pm_s[meta_row0:meta_row0 + N_META, P_K:P_K + D_ATTN]
    v_m = pm_s[meta_row0:meta_row0 + N_META, P_V:P_V + D_ATTN]
    heads = lambda a: a.reshape(1, a.shape[0], a.shape[1], ATTN_HEADS, ATTN_HEAD_DIM)
    k_prompt = heads(jnp.concatenate([jnp.broadcast_to(k_m[None], (bp, N_META, D_ATTN)),
                                      pm_p3[:, :, P_K:P_K + D_ATTN]], axis=1))
    v_prompt = heads(jnp.concatenate([jnp.broadcast_to(v_m[None], (bp, N_META, D_ATTN)),
                                      pm_p3[:, :, P_V:P_V + D_ATTN]], axis=1))
    k_sample = heads(pm_s[:n_s, P_K:P_K + D_ATTN].reshape(bs, dseq, D_ATTN))
    v_sample = heads(pm_s[:n_s, P_V:P_V + D_ATTN].reshape(bs, dseq, D_ATTN))
    state_out = lambda st: jnp.transpose(st, (0, 2, 1)).reshape(1, st.shape[0], SSM_HEADS, SSM_HEAD_DIM, D_STATE)
    return (y_prompt.reshape(bp, seq, D_MODEL), y_sample.reshape(bs, dseq, D_MODEL),
            k_prompt, v_prompt, logf_prompt[None], state_out(st_p), hist_p[None],
            k_sample, v_sample, lf_s[None], state_out(st_s), hist_s[None])
```

```python
import functools

import numpy as np
import jax
import jax.numpy as jnp
from jax import lax
from jax.experimental import pallas as pl
from jax.experimental.pallas import tpu as pltpu

F32 = jnp.float32
BF16 = jnp.bfloat16

D_MODEL = 1024
N_META = 16
CHUNK = 64
EPS = 1e-6
D_SSM = 2048
SSM_HEADS = 32
SSM_HEAD_DIM = 64
SSM_GROUPS = 8
D_STATE = 128
CONV_W = 4
D_CONV = D_SSM + 2 * SSM_GROUPS * D_STATE
ATTN_HEADS = 16
ATTN_HEAD_DIM = 64
D_ATTN = ATTN_HEADS * ATTN_HEAD_DIM
D_FF = 2816
HEAD_PAIRS = ATTN_HEADS // 2
LANES = 128
GROUP_W = D_SSM // SSM_GROUPS

OFF_XBC = D_SSM
OFF_DT = OFF_XBC + D_CONV
OFF_Q = OFF_DT + SSM_HEADS
OFF_K = OFF_Q + D_ATTN
OFF_V = OFF_K + D_ATTN
OFF_F = OFF_V + D_ATTN
OFF_GS = OFF_F + ATTN_HEADS
OFF_GA = OFF_GS + D_MODEL

P_XBC = 0
P_Z = D_CONV
P_Q = P_Z + D_SSM
P_K = P_Q + D_ATTN
P_V = P_K + D_ATTN
P_GS = P_V + D_ATTN
P_GA = P_GS + D_MODEL
P_TOTAL = P_GA + D_MODEL
S_DT = 0
S_F = SSM_HEADS

VMEM_LIMIT = 56 * 1024 * 1024

FF_CHUNK = D_FF // 2
PROJ_TN = P_TOTAL // 4
ATT_T = 512
CUMSUM_BLK = 256


def _dot(a, b):
    return jnp.dot(a, b, preferred_element_type=F32)


def _dot_nt(a, b):
    return lax.dot_general(a, b, (((1,), (1,)), ((), ())), preferred_element_type=F32)


def _dot_tn(a, b):
    return lax.dot_general(a, b, (((0,), (0,)), ((), ())), preferred_element_type=F32)


def _split3(x):
    hi = x.astype(BF16)
    r = x - hi.astype(F32)
    mid = r.astype(BF16)
    lo = (r - mid.astype(F32)).astype(BF16)
    return hi, mid, lo


def _rmsnorm(x, g):
    return x * lax.rsqrt(jnp.mean(x * x, axis=-1, keepdims=True) + EPS) * g


def _softplus(x):
    return jnp.maximum(x, 0.0) + jnp.log1p(jnp.exp(-jnp.abs(x)))


def _params(*sem):
    return pltpu.CompilerParams(dimension_semantics=sem, vmem_limit_bytes=VMEM_LIMIT)


def _ffn_kernel(x_ref, g1_ref, wa_ref, wb_ref, wo_ref, g2_ref, *refs, emit_h):
    if emit_h:
        h_ref, u_ref, xn_sc, acc_sc = refs
    else:
        u_ref, xn_sc, acc_sc = refs
    c = pl.program_id(1)

    @pl.when(c == 0)
    def _():
        xn_sc[...] = _rmsnorm(x_ref[...], g1_ref[...]).astype(BF16)
        acc_sc[...] = jnp.zeros_like(acc_sc)

    xn = xn_sc[...]
    a = _dot(xn, wa_ref[...])
    b = _dot(xn, wb_ref[...])
    g = (a * jax.nn.sigmoid(a) * b).astype(BF16)
    acc_sc[...] += _dot(g, wo_ref[...])

    @pl.when(c == pl.num_programs(1) - 1)
    def _():
        h = x_ref[...] + 0.5 * acc_sc[...]
        if emit_h:
            h_ref[...] = h
        u_ref[...] = _rmsnorm(h, g2_ref[...]).astype(u_ref.dtype)


def _ffn(x, g1, w_in, w_out, g2, *, tm, emit_h, u_dtype):
    rows = x.shape[0]
    nfc = D_FF // FF_CHUNK
    row_spec = pl.BlockSpec((tm, D_MODEL), lambda i, c: (i, 0))
    vec_spec = pl.BlockSpec((1, D_MODEL), lambda i, c: (0, 0))
    out_shape = [jax.ShapeDtypeStruct((rows, D_MODEL), u_dtype)]
    out_specs = [row_spec]
    if emit_h:
        out_shape = [jax.ShapeDtypeStruct((rows, D_MODEL), F32)] + out_shape
        out_specs = [row_spec] + out_specs
    return pl.pallas_call(
        functools.partial(_ffn_kernel, emit_h=emit_h),
        grid=(rows // tm, nfc),
        in_specs=[
            row_spec, vec_spec,
            pl.BlockSpec((D_MODEL, FF_CHUNK), lambda i, c: (0, c)),
            pl.BlockSpec((D_MODEL, FF_CHUNK), lambda i, c: (0, nfc + c)),
            pl.BlockSpec((FF_CHUNK, D_MODEL), lambda i, c: (c, 0)),
            vec_spec,
        ],
        out_specs=out_specs,
        out_shape=out_shape,
        scratch_shapes=[pltpu.VMEM((tm, D_MODEL), BF16), pltpu.VMEM((tm, D_MODEL), F32)],
        compiler_params=_params("parallel", "arbitrary"),
        name="ffn",
    )(x, g1, w_in, w_in, w_out, g2)


def _proj_kernel(u_ref, w_ref, o_ref):
    o_ref[...] = _dot(u_ref[...], w_ref[...])


def _proj(u, w, *, tm):
    rows = u.shape[0]
    return pl.pallas_call(
        _proj_kernel,
        grid=(P_TOTAL // PROJ_TN, rows // tm),
        in_specs=[pl.BlockSpec((tm, D_MODEL), lambda j, i: (i, 0)),
                  pl.BlockSpec((D_MODEL, PROJ_TN), lambda j, i: (0, j))],
        out_specs=pl.BlockSpec((tm, PROJ_TN), lambda j, i: (i, j)),
        out_shape=jax.ShapeDtypeStruct((rows, P_TOTAL), F32),
        compiler_params=_params("parallel", "parallel"),
        name="proj",
    )(u, w)


def _small_kernel(u_ref, w_ref, b_ref, o_ref):
    x = _dot(u_ref[...], w_ref[...]) + b_ref[...]
    lane = lax.broadcasted_iota(jnp.int32, (1, LANES), 1)
    sgn = jnp.where(lane < S_F, 1.0, -1.0)
    o_ref[...] = sgn * _softplus(sgn * x)


def _small(u, w, b, *, tm):
    rows = u.shape[0]
    return pl.pallas_call(
        _small_kernel,
        grid=(rows // tm,),
        in_specs=[pl.BlockSpec((tm, D_MODEL), lambda i: (i, 0)),
                  pl.BlockSpec((D_MODEL, LANES), lambda i: (0, 0)),
                  pl.BlockSpec((1, LANES), lambda i: (0, 0))],
        out_specs=pl.BlockSpec((tm, LANES), lambda i: (i, 0)),
        out_shape=jax.ShapeDtypeStruct((rows, LANES), F32),
        compiler_params=_params("parallel"),
        name="small",
    )(u, w, b)


HIST0 = 8 - (CONV_W - 1)


def _ssd_kernel(xbc_ref, z_ref, dt_ref, s0_ref, h0_ref, cw_ref, cb_ref, alog_ref, dskip_ref, norm_ref,
                e_ref, o_ref, sT_ref, hT_ref, st_sc, xp_sc, y_sc, *, n_pad):
    T = CHUNK
    c = pl.program_id(1)

    @pl.when(c == 0)
    def _():
        st_sc[...] = s0_ref[...]
        xp_sc[HIST0:8, :] = h0_ref[...]

    xraw = xbc_ref[...]
    xp_sc[8:8 + T, :] = xraw
    cw = cw_ref[...]
    conv = cb_ref[...] + cw[0:1] * xp_sc[HIST0:HIST0 + T, :]
    conv = conv + cw[1:2] * xp_sc[HIST0 + 1:HIST0 + 1 + T, :]
    conv = conv + cw[2:3] * xp_sc[HIST0 + 2:HIST0 + 2 + T, :]
    conv = conv + cw[3:4] * xraw
    xp_sc[HIST0:8, :] = xraw[T - (CONV_W - 1):T, :]
    xc = conv * jax.nn.sigmoid(conv)
    dtv = dt_ref[:, S_DT:S_DT + SSM_HEADS]
    if n_pad:
        valid = lax.broadcasted_iota(jnp.int32, (T, 1), 0) >= n_pad
        xc = jnp.where(valid, xc, 0.0)
        dtv = jnp.where(valid, dtv, 0.0)

    a = -jnp.exp(alog_ref[...])
    adt = a * dtv
    row = lax.broadcasted_iota(jnp.int32, (T, T), 0)
    col = lax.broadcasted_iota(jnp.int32, (T, T), 1)
    causal = col <= row
    tril = jnp.where(causal, 1.0, 0.0).astype(BF16)
    r32 = lax.broadcasted_iota(jnp.int32, (SSM_HEADS, SSM_HEADS), 0)
    c32 = lax.broadcasted_iota(jnp.int32, (SSM_HEADS, SSM_HEADS), 1)
    eye = jnp.where(r32 == c32, 1.0, 0.0).astype(BF16)
    expand = e_ref[...]

    a_cs = sum(_dot(tril, piece) for piece in _split3(adt))
    a_cs_t = sum(_dot_nt(eye, piece) for piece in _split3(a_cs))
    dt_e = sum(_dot(piece, expand) for piece in _split3(dtv))
    acs_e = sum(_dot(piece, expand) for piece in _split3(a_cs))

    xs = xc[:, :D_SSM]
    bm = xc[:, D_SSM:D_SSM + SSM_GROUPS * D_STATE].astype(BF16)
    cm = xc[:, D_SSM + SSM_GROUPS * D_STATE:].astype(BF16)
    xd = xs * dt_e
    al_e = acs_e[T - 1:T, :]
    xdw = (xd * jnp.exp(al_e - acs_e)).astype(BF16)
    cdec = jnp.exp(al_e)
    eacs = jnp.exp(acs_e)
    lane = lax.broadcasted_iota(jnp.int32, (1, D_SSM), 1)
    lo = (lane & ATTN_HEAD_DIM) == 0
    xd_lo = jnp.where(lo, xd, 0.0).astype(BF16)
    xd_hi = jnp.where(lo, 0.0, xd).astype(BF16)

    for g in range(SSM_GROUPS):
        gs = slice(g * GROUP_W, (g + 1) * GROUP_W)
        cmg = cm[:, g * D_STATE:(g + 1) * D_STATE]
        bmg = bm[:, g * D_STATE:(g + 1) * D_STATE]
        cb = _dot_nt(cmg, bmg)
        stg = st_sc[:, gs]
        yoff = _dot(cmg, stg.astype(BF16))
        for rr in range(2):
            pair = g * 2 + rr
            ps = slice(pair * LANES, (pair + 1) * LANES)
            ydiag = None
            for hh in range(2):
                h = pair * 2 + hh
                seg = a_cs[:, h:h + 1] - a_cs_t[h:h + 1, :]
                m = (cb * jnp.exp(jnp.where(causal, seg, -jnp.inf))).astype(BF16)
                d = _dot(m, (xd_lo, xd_hi)[hh][:, ps])
                ydiag = d if ydiag is None else ydiag + d
            y_sc[:, ps] = ydiag + yoff[:, rr * LANES:(rr + 1) * LANES] * eacs[:, ps]
        st_sc[:, gs] = stg * cdec[:, gs] + _dot_tn(bmg, xdw[:, gs])

    y = y_sc[...] + dskip_ref[...] * xs
    zz = z_ref[...]
    y = y * (zz * jax.nn.sigmoid(zz))
    for g in range(SSM_GROUPS):
        gs = slice(g * GROUP_W, (g + 1) * GROUP_W)
        yg = y[:, gs]
        yg = yg * lax.rsqrt(jnp.mean(yg * yg, axis=-1, keepdims=True) + EPS)
        o_ref[:, gs] = (yg * norm_ref[:, gs]).astype(o_ref.dtype)

    @pl.when(c == pl.num_programs(1) - 1)
    def _():
        sT_ref[...] = st_sc[...]
        hT_ref[...] = xp_sc[HIST0:8, :]


def _ssd(proj3, small3, s0, h0, cw, cb, alog, dskip, norm, expand, *, nb, b_off, shared_init, n_pad):
    nc = proj3.shape[1] // CHUNK
    init = (lambda b, c: (0, 0, 0)) if shared_init else (lambda b, c: (b, 0, 0))
    const = lambda b, c: (0, 0)
    return pl.pallas_call(
        functools.partial(_ssd_kernel, n_pad=n_pad),
        grid=(nb, nc),
        in_specs=[
            pl.BlockSpec((None, CHUNK, D_CONV), lambda b, c: (b + b_off, c, P_XBC // D_CONV)),
            pl.BlockSpec((None, CHUNK, D_SSM), lambda b, c: (b + b_off, c, P_Z // D_SSM)),
            pl.BlockSpec((None, CHUNK, LANES), lambda b, c: (b + b_off, c, 0)),
            pl.BlockSpec((None, D_STATE, D_SSM), init),
            pl.BlockSpec((None, CONV_W - 1, D_CONV), init),
            pl.BlockSpec((CONV_W, D_CONV), const),
            pl.BlockSpec((1, D_CONV), const),
            pl.BlockSpec((1, SSM_HEADS), const),
            pl.BlockSpec((1, D_SSM), const),
            pl.BlockSpec((1, D_SSM), const),
            pl.BlockSpec((SSM_HEADS, D_SSM), const),
        ],
        out_specs=[
            pl.BlockSpec((None, CHUNK, D_SSM), lambda b, c: (b, c, 0)),
            pl.BlockSpec((None, D_STATE, D_SSM), lambda b, c: (b, 0, 0)),
            pl.BlockSpec((None, CONV_W - 1, D_CONV), lambda b, c: (b, 0, 0)),
        ],
        out_shape=[
            jax.ShapeDtypeStruct((nb, nc * CHUNK, D_SSM), BF16),
            jax.ShapeDtypeStruct((nb, D_STATE, D_SSM), F32),
            jax.ShapeDtypeStruct((nb, CONV_W - 1, D_CONV), F32),
        ],
        scratch_shapes=[
            pltpu.VMEM((D_STATE, D_SSM), F32),
            pltpu.VMEM((8 + CHUNK, D_CONV), F32),
            pltpu.VMEM((CHUNK, D_SSM), F32),
        ],
        compiler_params=_params("parallel", "arbitrary"),
        name="ssd",
    )(proj3, proj3, small3, s0, h0, cw, cb, alog, dskip, norm, expand)


def _cumsum_kernel(x_ref, o_ref):
    n = x_ref.shape[-1] // CUMSUM_BLK
    r = lax.broadcasted_iota(jnp.int32, (CUMSUM_BLK, CUMSUM_BLK), 0)
    c = lax.broadcasted_iota(jnp.int32, (CUMSUM_BLK, CUMSUM_BLK), 1)
    upper = jnp.where(r <= c, 1.0, 0.0).astype(BF16)
    carry = jnp.zeros((ATTN_HEADS, 1), F32)
    for i in range(n):
        blk = slice(i * CUMSUM_BLK, (i + 1) * CUMSUM_BLK)
        cs = sum(_dot(piece, upper) for piece in _split3(x_ref[:, blk])) + carry
        o_ref[:, blk] = cs
        carry = cs[:, CUMSUM_BLK - 1:CUMSUM_BLK]


def _cumsum_t(x_t):
    b, h, l = x_t.shape
    spec = pl.BlockSpec((None, h, l), lambda i: (i, 0, 0))
    return pl.pallas_call(
        _cumsum_kernel, grid=(b,), in_specs=[spec], out_specs=spec,
        out_shape=jax.ShapeDtypeStruct(x_t.shape, F32),
        compiler_params=_params("parallel"), name="cumsum",
    )(x_t)


def _cum_logf(lf_all):
    l = lf_all.shape[1]
    lp = -(-l // CUMSUM_BLK) * CUMSUM_BLK
    x_t = jnp.pad(jnp.transpose(lf_all, (0, 2, 1)), ((0, 0), (0, 0), (0, lp - l)))
    return _cumsum_t(x_t)[:, :, :l]


BIAS_PIECES = 3


def _bias_lane0(head):
    return ATTN_HEAD_DIM if head % 2 == 0 else 0


def _bias_placement():
    sq = np.zeros((BIAS_PIECES, ATTN_HEADS, ATTN_HEADS * LANES), np.float32)
    sk = np.zeros_like(sq)
    one_q = np.zeros((1, ATTN_HEADS * LANES), np.float32)
    one_k = np.zeros_like(one_q)
    for h in range(ATTN_HEADS):
        base = h * LANES + _bias_lane0(h)
        for piece in range(BIAS_PIECES):
            sk[piece, h, base + piece] = -1.0
            one_q[0, base + piece] = 1.0
            sq[piece, h, base + BIAS_PIECES + piece] = 1.0
            one_k[0, base + BIAS_PIECES + piece] = 1.0
    return jnp.asarray(sq, BF16), jnp.asarray(sk, BF16), jnp.asarray(one_q), jnp.asarray(one_k)


def _attn_prep_kernel(q_ref, k_ref, v_ref, c_ref, sq_ref, sk_ref, oneq_ref, onek_ref, qo_ref, ko_ref, vt_ref):
    pieces = _split3(c_ref[...])
    bias_q = sum(_dot(pc, sq_ref[n]) for n, pc in enumerate(pieces)) + oneq_ref[...]
    bias_k = sum(_dot(pc, sk_ref[n]) for n, pc in enumerate(pieces)) + onek_ref[...]
    lane = lax.broadcasted_iota(jnp.int32, (1, LANES), 1)
    lo = lane < ATTN_HEAD_DIM
    for p in range(HEAD_PAIRS):
        ps = slice(p * LANES, (p + 1) * LANES)
        q2 = q_ref[:, ps] * (ATTN_HEAD_DIM ** -0.5)
        k2 = k_ref[:, ps]
        for hh in range(2):
            h = 2 * p + hh
            hs = slice(h * LANES, (h + 1) * LANES)
            sel = lo if hh == 0 else jnp.logical_not(lo)
            qo_ref[h] = (jnp.where(sel, q2, 0.0) + bias_q[:, hs]).astype(BF16)
            ko_ref[h] = (jnp.where(sel, k2, 0.0) + bias_k[:, hs]).astype(BF16)
        v_t = jnp.transpose(v_ref[:, ps]).astype(BF16)
        vt_ref[2 * p] = v_t[:ATTN_HEAD_DIM]
        vt_ref[2 * p + 1] = v_t[ATTN_HEAD_DIM:]


def _attn_prep(proj3, c_rows, consts, *, nb, b_off, tile):
    l = c_rows.shape[1]
    col = lambda off: off // D_ATTN
    sq, sk, one_q, one_k = consts
    c3 = lambda b, i: (0, 0, 0)
    c2 = lambda b, i: (0, 0)
    return pl.pallas_call(
        _attn_prep_kernel,
        grid=(nb, l // tile),
        in_specs=[
            pl.BlockSpec((None, tile, D_ATTN), lambda b, i: (b + b_off, i, col(P_Q))),
            pl.BlockSpec((None, tile, D_ATTN), lambda b, i: (b + b_off, i, col(P_K))),
            pl.BlockSpec((None, tile, D_ATTN), lambda b, i: (b + b_off, i, col(P_V))),
            pl.BlockSpec((None, tile, ATTN_HEADS), lambda b, i: (b, i, 0)),
            pl.BlockSpec(sq.shape, c3), pl.BlockSpec(sk.shape, c3),
            pl.BlockSpec(one_q.shape, c2), pl.BlockSpec(one_k.shape, c2),
        ],
        out_specs=[
            pl.BlockSpec((None, ATTN_HEADS, tile, LANES), lambda b, i: (b, 0, i, 0)),
            pl.BlockSpec((None, ATTN_HEADS, tile, LANES), lambda b, i: (b, 0, i, 0)),
            pl.BlockSpec((None, ATTN_HEADS, ATTN_HEAD_DIM, tile), lambda b, i: (b, 0, 0, i)),
        ],
        out_shape=[
            jax.ShapeDtypeStruct((nb, ATTN_HEADS, l, LANES), BF16),
            jax.ShapeDtypeStruct((nb, ATTN_HEADS, l, LANES), BF16),
            jax.ShapeDtypeStruct((nb, ATTN_HEADS, ATTN_HEAD_DIM, l), BF16),
        ],
        compiler_params=_params("parallel", "parallel"),
        name="attn_prep",
    )(proj3, proj3, proj3, c_rows, sq, sk, one_q, one_k)


QBLK = 256


def _attn_prompt_kernel(it_ref, jt_ref, q_ref, k_ref, vt_ref, km_ref, vtm_ref, o_ref, m_sc, l_sc, acc_sc,
                        *, meta_lo, meta_hi):
    t = pl.program_id(1)
    i = it_ref[t]
    j = jt_ref[t]
    nqb = ATT_T // QBLK

    def q_block(h, r):
        return q_ref[h, r * QBLK:(r + 1) * QBLK, :]

    def head_update(h, scores, first):
        if not first:
            m_old_all, l_old_all, acc_old_all = m_sc[h], l_sc[h], acc_sc[h]
        m_out, l_out, acc_out = [], [], []
        for r, (s_t, v_t) in enumerate(scores):
            qs = slice(r * QBLK, (r + 1) * QBLK)
            mx = jnp.max(s_t, axis=0, keepdims=True)
            if first:
                m_new = mx
                pr = jnp.exp(s_t - m_new)
                l_new = jnp.sum(pr, axis=0, keepdims=True)
                acc_new = _dot(v_t, pr.astype(BF16))
            else:
                m_old = m_old_all[:, qs]
                m_new = jnp.maximum(m_old, mx)
                alpha = jnp.exp(m_old - m_new)
                pr = jnp.exp(s_t - m_new)
                l_new = alpha * l_old_all[:, qs] + jnp.sum(pr, axis=0, keepdims=True)
                acc_new = acc_old_all[:, qs] * alpha + _dot(v_t, pr.astype(BF16))
            m_out.append(m_new)
            l_out.append(l_new)
            acc_out.append(acc_new)
        m_sc[h] = jnp.concatenate(m_out, axis=1)
        l_sc[h] = jnp.concatenate(l_out, axis=1)
        acc_sc[h] = jnp.concatenate(acc_out, axis=1)

    def run(score_fn, first):
        nxt = [score_fn(0, r) for r in range(nqb)]
        for h in range(ATTN_HEADS):
            cur = nxt
            if h + 1 < ATTN_HEADS:
                nxt = [score_fn(h + 1, r) for r in range(nqb)]
            head_update(h, cur, first)

    @pl.when(j == 0)
    def _():
        krow = lax.broadcasted_iota(jnp.int32, (km_ref.shape[1], 1), 0)
        is_meta = jnp.logical_and(krow >= meta_lo, krow < meta_hi)
        run(lambda h, r: (jnp.where(is_meta, _dot_nt(km_ref[h], q_block(h, r)), -jnp.inf), vtm_ref[h]), True)

    @pl.when(j < i)
    def _():
        run(lambda h, r: (_dot_nt(k_ref[h], q_block(h, r)), vt_ref[h]), False)

    @pl.when(j == i)
    def _():
        def diag_scores(h, r):
            nk = (r + 1) * QBLK
            s_t = _dot_nt(k_ref[h, :nk, :], q_block(h, r))
            krow = lax.broadcasted_iota(jnp.int32, (nk, QBLK), 0)
            qcol = lax.broadcasted_iota(jnp.int32, (nk, QBLK), 1) + r * QBLK
            return jnp.where(krow <= qcol, s_t, -jnp.inf), vt_ref[h, :, :nk]

        run(diag_scores, False)
        for p in range(HEAD_PAIRS):
            o_t = jnp.concatenate([acc_sc[2 * p] / l_sc[2 * p], acc_sc[2 * p + 1] / l_sc[2 * p + 1]], axis=0)
            o_ref[:, p * LANES:(p + 1) * LANES] = jnp.transpose(o_t).astype(o_ref.dtype)


def _attn_prompt(q_aug, k_aug, v_t, km_aug, vm_t, *, meta_lo):
    nb, _, l, _ = q_aug.shape
    nt = l // ATT_T
    it = np.array([i for i in range(nt) for _ in range(i + 1)], np.int32)
    jt = np.array([j for i in range(nt) for j in range(i + 1)], np.int32)
    mrows = km_aug.shape[2]
    grid_spec = pltpu.PrefetchScalarGridSpec(
        num_scalar_prefetch=2,
        grid=(nb, len(it)),
        in_specs=[
            pl.BlockSpec((None, ATTN_HEADS, ATT_T, LANES), lambda b, t, it, jt: (b, 0, it[t], 0)),
            pl.BlockSpec((None, ATTN_HEADS, ATT_T, LANES), lambda b, t, it, jt: (b, 0, jt[t], 0)),
            pl.BlockSpec((None, ATTN_HEADS, ATTN_HEAD_DIM, ATT_T), lambda b, t, it, jt: (b, 0, 0, jt[t])),
            pl.BlockSpec((None, ATTN_HEADS, mrows, LANES), lambda b, t, it, jt: (0, 0, 0, 0)),
            pl.BlockSpec((None, ATTN_HEADS, ATTN_HEAD_DIM, mrows), lambda b, t, it, jt: (0, 0, 0, 0)),
        ],
        out_specs=pl.BlockSpec((None, ATT_T, D_ATTN), lambda b, t, it, jt: (b, it[t], 0)),
        scratch_shapes=[
            pltpu.VMEM((ATTN_HEADS, 1, ATT_T), F32),
            pltpu.VMEM((ATTN_HEADS, 1, ATT_T), F32),
            pltpu.VMEM((ATTN_HEADS, ATTN_HEAD_DIM, ATT_T), F32),
        ],
    )
    return pl.pallas_call(
        functools.partial(_attn_prompt_kernel, meta_lo=meta_lo, meta_hi=meta_lo + N_META),
        grid_spec=grid_spec,
        out_shape=jax.ShapeDtypeStruct((nb, l, D_ATTN), BF16),
        compiler_params=_params("parallel", "arbitrary"),
        name="attn_prompt",
    )(jnp.asarray(it), jnp.asarray(jt), q_aug, k_aug, v_t, km_aug, vm_t)


def _attn_sample_kernel(q_ref, kn_ref, vn_ref, km_ref, vm_ref, kc_hbm, vc_hbm, cq_ref, ckm_ref, ckc_ref,
                        ckn_ref, o_ref, kbuf, vbuf, sem):
    b = pl.program_id(0)
    h = pl.program_id(1)
    nh = pl.num_programs(1)
    n = b * nh + h
    slot = n % 2
    T = CHUNK
    D = ATTN_HEAD_DIM
    first = h % 2 == 0

    def cache_copies(bb, hh, sl):
        return (pltpu.make_async_copy(kc_hbm.at[bb, :, hh, :], kbuf.at[sl], sem.at[0, sl]),
                pltpu.make_async_copy(vc_hbm.at[bb, :, hh, :], vbuf.at[sl], sem.at[1, sl]))

    @pl.when(n == 0)
    def _():
        for cp in cache_copies(b, h, slot):
            cp.start()

    @pl.when(n + 1 < pl.num_programs(0) * nh)
    def _():
        for cp in cache_copies((n + 1) // nh, (n + 1) % nh, 1 - slot):
            cp.start()

    for cp in cache_copies(b, h, slot):
        cp.wait()
    kc_ref = kbuf.at[slot]
    vc_ref = vbuf.at[slot]

    def half(ref):
        x = ref[...]
        return jnp.where(first, x[:, :D], x[:, D:])

    qh = (half(q_ref) * (D ** -0.5)).astype(BF16)
    cq = cq_ref[...]
    row = lax.broadcasted_iota(jnp.int32, (T, T), 0)
    col = lax.broadcasted_iota(jnp.int32, (T, T), 1)
    s_m = _dot_nt(qh, half(km_ref).astype(BF16)) + cq - ckm_ref[...]
    s_c = _dot_nt(qh, kc_ref[...].astype(BF16)) + cq - ckc_ref[...]
    s_n = jnp.where(col <= row, _dot_nt(qh, half(kn_ref).astype(BF16)) + cq - ckn_ref[...], -jnp.inf)
    m = jnp.maximum(jnp.maximum(jnp.max(s_m, axis=1, keepdims=True), jnp.max(s_c, axis=1, keepdims=True)),
                    jnp.max(s_n, axis=1, keepdims=True))
    p_m = jnp.exp(s_m - m)
    p_c = jnp.exp(s_c - m)
    p_n = jnp.exp(s_n - m)
    l = (jnp.sum(p_m, axis=1, keepdims=True) + jnp.sum(p_c, axis=1, keepdims=True)
         + jnp.sum(p_n, axis=1, keepdims=True))
    o = (_dot(p_m.astype(BF16), half(vm_ref).astype(BF16)) + _dot(p_c.astype(BF16), vc_ref[...].astype(BF16))
         + _dot(p_n.astype(BF16), half(vn_ref).astype(BF16)))
    o_ref[...] = (o / l).astype(o_ref.dtype)


def _attn_sample(proj3, proj2, meta_blk, cache_k, cache_v, cq, ckm_t, ckc_t, ckn_t, *, nb):
    past = cache_k.shape[1]
    lb = lambda off: off // LANES
    pair = lambda off: (lambda b, h: (b, 0, lb(off) + h // 2))
    ckspec = lambda n: pl.BlockSpec((None, 1, n), lambda b, h: (b * ATTN_HEADS + h, 0, 0))
    return pl.pallas_call(
        _attn_sample_kernel,
        grid=(nb, ATTN_HEADS),
        in_specs=[
            pl.BlockSpec((None, CHUNK, LANES), pair(P_Q)),
            pl.BlockSpec((None, CHUNK, LANES), pair(P_K)),
            pl.BlockSpec((None, CHUNK, LANES), pair(P_V)),
            pl.BlockSpec((N_META, LANES), lambda b, h: (meta_blk, lb(P_K) + h // 2)),
            pl.BlockSpec((N_META, LANES), lambda b, h: (meta_blk, lb(P_V) + h // 2)),
            pl.BlockSpec(memory_space=pl.ANY),
            pl.BlockSpec(memory_space=pl.ANY),
            pl.BlockSpec((None, None, CHUNK, 1), lambda b, h: (b, h, 0, 0)),
            ckspec(N_META), ckspec(past), ckspec(CHUNK),
        ],
        out_specs=pl.BlockSpec((None, None, CHUNK, ATTN_HEAD_DIM), lambda b, h: (b, h, 0, 0)),
        out_shape=jax.ShapeDtypeStruct((nb, ATTN_HEADS, CHUNK, ATTN_HEAD_DIM), BF16),
        scratch_shapes=[
            pltpu.VMEM((2, past, ATTN_HEAD_DIM), F32),
            pltpu.VMEM((2, past, ATTN_HEAD_DIM), F32),
            pltpu.SemaphoreType.DMA((2, 2)),
        ],
        compiler_params=_params("arbitrary", "arbitrary"),
        name="attn_sample",
    )(proj3, proj3, proj3, proj2, proj2, cache_k, cache_v, cq, ckm_t, ckc_t, ckn_t)


def _merge_kernel(os_ref, oa_ref, gs_ref, ga_ref, h_ref, wbs_ref, wba_ref, wo_ref, o_ref):
    bs = _dot(os_ref[...], wbs_ref[...])
    ba = _dot(oa_ref[...], wba_ref[...])
    merged = jax.nn.sigmoid(gs_ref[...]) * bs + jax.nn.sigmoid(ga_ref[...]) * ba
    o_ref[...] = h_ref[...] + _dot(merged.astype(BF16), wo_ref[...])


def _merge(o_ssm, o_attn, proj, h, w_bs, w_ba, w_o, *, tm):
    rows = o_ssm.shape[0]
    const = lambda i: (0, 0)
    return pl.pallas_call(
        _merge_kernel,
        grid=(rows // tm,),
        in_specs=[
            pl.BlockSpec((tm, D_SSM), lambda i: (i, 0)),
            pl.BlockSpec((tm, D_ATTN), lambda i: (i, 0)),
            pl.BlockSpec((tm, D_MODEL), lambda i: (i, P_GS // D_MODEL)),
            pl.BlockSpec((tm, D_MODEL), lambda i: (i, P_GA // D_MODEL)),
            pl.BlockSpec((tm, D_MODEL), lambda i: (i, 0)),
            pl.BlockSpec((D_SSM, D_MODEL), const),
            pl.BlockSpec((D_ATTN, D_MODEL), const),
            pl.BlockSpec((D_MODEL, D_MODEL), const),
        ],
        out_specs=pl.BlockSpec((tm, D_MODEL), lambda i: (i, 0)),
        out_shape=jax.ShapeDtypeStruct((rows, D_MODEL), F32),
        compiler_params=_params("parallel"),
        name="merge",
    )(o_ssm, o_attn, proj, proj, h, w_bs, w_ba, w_o)


def kernel(x_prompt, x_sample, cache_k, cache_v, cache_logf, state_ssm, state_conv, meta_tokens, norm_ffn1, ffn1_w_in, ffn1_w_out, norm_mix, w_in, conv_w, conv_b, dt_bias, a_log, d_skip, f_bias, ssm_norm, w_br_ssm, w_br_attn, w_out, norm_ffn2, ffn2_w_in, ffn2_w_out, norm_final):
    assert norm_ffn1.shape[0] == 1, "single-layer trunk"
    bp, seq, _ = x_prompt.shape
    bs, dseq, _ = x_sample.shape
    assert dseq == CHUNK and seq % ATT_T == 0
    n_s = bs * dseq
    rows_small = n_s + 2 * CHUNK
    meta_row0 = n_s + CHUNK - N_META
    meta_blk = meta_row0 // N_META

    row = lambda v: v.reshape(1, -1).astype(F32)
    w = w_in[0]
    w_main = jnp.concatenate([w[:, OFF_XBC:OFF_DT], w[:, :OFF_XBC], w[:, OFF_Q:OFF_F], w[:, OFF_GS:]],
                             axis=1).astype(BF16)
    pad_small = LANES - SSM_HEADS - ATTN_HEADS
    w_small = jnp.concatenate([w[:, OFF_DT:OFF_Q], w[:, OFF_F:OFF_GS], jnp.zeros((D_MODEL, pad_small), F32)],
                              axis=1).astype(BF16)
    b_small = jnp.concatenate([dt_bias[0], f_bias[0], jnp.zeros((pad_small,), F32)]).reshape(1, LANES)
    w1i, w1o = ffn1_w_in[0].astype(BF16), ffn1_w_out[0].astype(BF16)
    w2i, w2o = ffn2_w_in[0].astype(BF16), ffn2_w_out[0].astype(BF16)
    w_bs, w_ba, w_o = w_br_ssm[0].astype(BF16), w_br_attn[0].astype(BF16), w_out[0].astype(BF16)
    expand = jnp.asarray(np.repeat(np.eye(SSM_HEADS, dtype=np.float32), SSM_HEAD_DIM, axis=1), BF16)
    ssd_consts = (conv_w[0], row(conv_b[0]), row(a_log[0]), row(jnp.repeat(d_skip[0], SSM_HEAD_DIM)),
                  row(ssm_norm[0]), expand)

    def front(x, tm_ffn, tm_proj):
        h1, u = _ffn(x, row(norm_ffn1[0]), w1i, w1o, row(norm_mix[0]), tm=tm_ffn, emit_h=True, u_dtype=BF16)
        return h1, _proj(u, w_main, tm=tm_proj), _small(u, w_small, b_small, tm=tm_proj)

    xp = x_prompt.reshape(bp * seq, D_MODEL)
    h1_p, pm_p, ps_p = front(xp, 512, 1024)
    x_small = jnp.concatenate([x_sample.reshape(n_s, D_MODEL), jnp.zeros((CHUNK - N_META, D_MODEL), F32),
                               meta_tokens, jnp.zeros((CHUNK, D_MODEL), F32)], axis=0)
    h1_s, pm_s, ps_s = front(x_small, rows_small // 2, rows_small // 2)

    pm_p3 = pm_p.reshape(bp, seq, P_TOTAL)
    ps_p3 = ps_p.reshape(bp, seq, LANES)
    pm_s3 = pm_s.reshape(rows_small // CHUNK, CHUNK, P_TOTAL)
    ps_s3 = ps_s.reshape(rows_small // CHUNK, CHUNK, LANES)

    zero_state = jnp.zeros((1, D_STATE, D_SSM), F32)
    zero_hist = jnp.zeros((1, CONV_W - 1, D_CONV), F32)
    _, st_m, hist_m = _ssd(pm_s3, ps_s3, zero_state, zero_hist, *ssd_consts,
                           nb=1, b_off=bs, shared_init=True, n_pad=CHUNK - N_META)
    o_ssm_p, st_p, hist_p = _ssd(pm_p3, ps_p3, st_m, hist_m, *ssd_consts,
                                 nb=bp, b_off=0, shared_init=True, n_pad=0)
    st_s0 = jnp.transpose(state_ssm[0].reshape(bs, D_SSM, D_STATE), (0, 2, 1))
    o_ssm_s, st_s, hist_s = _ssd(pm_s3, ps_s3, st_s0, state_conv[0], *ssd_consts,
                                 nb=bs, b_off=0, shared_init=False, n_pad=0)

    lf_m = ps_s[meta_row0:meta_row0 + N_META, S_F:S_F + ATTN_HEADS]
    lf_p = ps_p3[:, :, S_F:S_F + ATTN_HEADS]
    lf_s = ps_s[:n_s, S_F:S_F + ATTN_HEADS].reshape(bs, dseq, ATTN_HEADS)
    logf_prompt = jnp.concatenate([jnp.broadcast_to(lf_m[None], (bp, N_META, ATTN_HEADS)), lf_p], axis=1)
    lf_all_s = jnp.concatenate([jnp.broadcast_to(lf_m[None], (bs, N_META, ATTN_HEADS)),
                                cache_logf[0].astype(F32), lf_s], axis=1)
    past = cache_logf.shape[2]
    pairs = lambda c: c.reshape(c.shape[0], HEAD_PAIRS, 2, c.shape[2])
    c_p = _cum_logf(logf_prompt)
    c_s = _cum_logf(lf_all_s)

    prep_consts = _bias_placement()
    q_aug, k_aug, v_t = _attn_prep(pm_p3, jnp.transpose(c_p[:, :, N_META:], (0, 2, 1)), prep_consts,
                                   nb=bp, b_off=0, tile=ATT_T)
    meta_tile = 2 * CHUNK
    meta_lo = meta_row0 - n_s
    c_meta = jnp.pad(jnp.transpose(c_p[:1, :, :N_META], (0, 2, 1)),
                     ((0, 0), (meta_lo, meta_tile - meta_lo - N_META), (0, 0)))
    _, km_aug, vm_t = _attn_prep(pm_s.reshape(rows_small // meta_tile, meta_tile, P_TOTAL), c_meta, prep_consts,
                                 nb=1, b_off=n_s // meta_tile, tile=meta_tile)
    o_attn_p = _attn_prompt(q_aug, k_aug, v_t, km_aug, vm_t, meta_lo=meta_lo)
    per_head = lambda c: c.reshape(bs * ATTN_HEADS, 1, c.shape[2])
    o_attn_s = _attn_sample(pm_s3, pm_s, meta_blk, cache_k[0], cache_v[0],
                            c_s[:, :, N_META + past:, None],
                            per_head(c_s[:, :, :N_META]), per_head(c_s[:, :, N_META:N_META + past]),
                            per_head(c_s[:, :, N_META + past:]), nb=bs)
    o_attn_s = jnp.transpose(o_attn_s, (0, 2, 1, 3))

    def back(o_ssm, o_attn, pm, h1, tm):
        h2 = _merge(o_ssm, o_attn, pm, h1, w_bs, w_ba, w_o, tm=tm)
        return _ffn(h2, row(norm_ffn2[0]), w2i, w2o, row(norm_final), tm=tm, emit_h=False, u_dtype=F32)[0]

    y_prompt = back(o_ssm_p.reshape(bp * seq, D_SSM), o_attn_p.reshape(bp * seq, D_ATTN), pm_p, h1_p, 512)
    y_sample = back(o_ssm_s.reshape(n_s, D_SSM), o_attn_s.reshape(n_s, D_ATTN), pm_s, h1_s, 512)

    k_m = pm_s[meta_row0:meta_row0 + N_META, P_K:P_K + D_ATTN]
    v_m = pm_s[meta_row0:meta_row0 + N_META, P_V:P_V + D_ATTN]
    heads = lambda a: a.reshape(1, a.shape[0], a.shape[1], ATTN_HEADS, ATTN_HEAD_DIM)
    k_prompt = heads(jnp.concatenate([jnp.broadcast_to(k_m[None], (bp, N_META, D_ATTN)),
                                      pm_p3[:, :, P_K:P_K + D_ATTN]], axis=1))
    v_prompt = heads(jnp.concatenate([jnp.broadcast_to(v_m[None], (bp, N_META, D_ATTN)),
                                      pm_p3[:, :, P_V:P_V + D_ATTN]], axis=1))
    k_sample = heads(pm_s[:n_s, P_K:P_K + D_ATTN].reshape(bs, dseq, D_ATTN))
    v_sample = heads(pm_s[:n_s, P_V:P_V + D_ATTN].reshape(bs, dseq, D_ATTN))
    state_out = lambda st: jnp.transpose(st, (0, 2, 1)).reshape(1, st.shape[0], SSM_HEADS, SSM_HEAD_DIM, D_STATE)
    return (y_prompt.reshape(bp, seq, D_MODEL), y_sample.reshape(bs, dseq, D_MODEL),
            k_prompt, v_prompt, logf_prompt[None], state_out(st_p), hist_p[None],
            k_sample, v_sample, lf_s[None], state_out(st_s), hist_s[None])
```

```python
import functools

import numpy as np
import jax
import jax.numpy as jnp
from jax import lax
from jax.experimental import pallas as pl
from jax.experimental.pallas import tpu as pltpu

F32 = jnp.float32
BF16 = jnp.bfloat16

D_MODEL = 1024
N_META = 16
CHUNK = 64
EPS = 1e-6
D_SSM = 2048
SSM_HEADS = 32
SSM_HEAD_DIM = 64
SSM_GROUPS = 8
D_STATE = 128
CONV_W = 4
D_CONV = D_SSM + 2 * SSM_GROUPS * D_STATE
ATTN_HEADS = 16
ATTN_HEAD_DIM = 64
D_ATTN = ATTN_HEADS * ATTN_HEAD_DIM
D_FF = 2816
HEAD_PAIRS = ATTN_HEADS // 2
LANES = 128
GROUP_W = D_SSM // SSM_GROUPS

OFF_XBC = D_SSM
OFF_DT = OFF_XBC + D_CONV
OFF_Q = OFF_DT + SSM_HEADS
OFF_K = OFF_Q + D_ATTN
OFF_V = OFF_K + D_ATTN
OFF_F = OFF_V + D_ATTN
OFF_GS = OFF_F + ATTN_HEADS
OFF_GA = OFF_GS + D_MODEL

P_XBC = 0
P_Z = D_CONV
P_Q = P_Z + D_SSM
P_K = P_Q + D_ATTN
P_V = P_K + D_ATTN
P_GS = P_V + D_ATTN
P_GA = P_GS + D_MODEL
P_TOTAL = P_GA + D_MODEL
S_DT = 0
S_F = SSM_HEADS

VMEM_LIMIT = 56 * 1024 * 1024

FF_CHUNK = D_FF // 2
PROJ_TN = P_TOTAL // 4
ATT_T = 512
CUMSUM_BLK = 256


def _dot(a, b):
    return jnp.dot(a, b, preferred_element_type=F32)


def _dot_nt(a, b):
    return lax.dot_general(a, b, (((1,), (1,)), ((), ())), preferred_element_type=F32)


def _dot_tn(a, b):
    return lax.dot_general(a, b, (((0,), (0,)), ((), ())), preferred_element_type=F32)


def _split3(x):
    hi = x.astype(BF16)
    r = x - hi.astype(F32)
    mid = r.astype(BF16)
    lo = (r - mid.astype(F32)).astype(BF16)
    return hi, mid, lo


def _rmsnorm(x, g):
    return x * lax.rsqrt(jnp.mean(x * x, axis=-1, keepdims=True) + EPS) * g


def _softplus(x):
    return jnp.maximum(x, 0.0) + jnp.log1p(jnp.exp(-jnp.abs(x)))


def _params(*sem):
    return pltpu.CompilerParams(dimension_semantics=sem, vmem_limit_bytes=VMEM_LIMIT)


def _ffn_kernel(x_ref, g1_ref, wa_ref, wb_ref, wo_ref, g2_ref, *refs, emit_h):
    if emit_h:
        h_ref, u_ref, xn_sc, acc_sc = refs
    else:
        u_ref, xn_sc, acc_sc = refs
    c = pl.program_id(1)

    @pl.when(c == 0)
    def _():
        xn_sc[...] = _rmsnorm(x_ref[...], g1_ref[...]).astype(BF16)
        acc_sc[...] = jnp.zeros_like(acc_sc)

    xn = xn_sc[...]
    a = _dot(xn, wa_ref[...])
    b = _dot(xn, wb_ref[...])
    g = (a * jax.nn.sigmoid(a) * b).astype(BF16)
    acc_sc[...] += _dot(g, wo_ref[...])

    @pl.when(c == pl.num_programs(1) - 1)
    def _():
        h = x_ref[...] + 0.5 * acc_sc[...]
        if emit_h:
            h_ref[...] = h
        u_ref[...] = _rmsnorm(h, g2_ref[...]).astype(u_ref.dtype)


def _ffn(x, g1, w_in, w_out, g2, *, tm, emit_h, u_dtype):
    rows = x.shape[0]
    nfc = D_FF // FF_CHUNK
    row_spec = pl.BlockSpec((tm, D_MODEL), lambda i, c: (i, 0))
    vec_spec = pl.BlockSpec((1, D_MODEL), lambda i, c: (0, 0))
    out_shape = [jax.ShapeDtypeStruct((rows, D_MODEL), u_dtype)]
    out_specs = [row_spec]
    if emit_h:
        out_shape = [jax.ShapeDtypeStruct((rows, D_MODEL), F32)] + out_shape
        out_specs = [row_spec] + out_specs
    return pl.pallas_call(
        functools.partial(_ffn_kernel, emit_h=emit_h),
        grid=(rows // tm, nfc),
        in_specs=[
            row_spec, vec_spec,
            pl.BlockSpec((D_MODEL, FF_CHUNK), lambda i, c: (0, c)),
            pl.BlockSpec((D_MODEL, FF_CHUNK), lambda i, c: (0, nfc + c)),
            pl.BlockSpec((FF_CHUNK, D_MODEL), lambda i, c: (c, 0)),
            vec_spec,
        ],
        out_specs=out_specs,
        out_shape=out_shape,
        scratch_shapes=[pltpu.VMEM((tm, D_MODEL), BF16), pltpu.VMEM((tm, D_MODEL), F32)],
        compiler_params=_params("parallel", "arbitrary"),
        name="ffn",
    )(x, g1, w_in, w_in, w_out, g2)


def _proj_kernel(u_ref, w_ref, o_ref):
    o_ref[...] = _dot(u_ref[...], w_ref[...])


def _proj(u, w, *, tm):
    rows = u.shape[0]
    return pl.pallas_call(
        _proj_kernel,
        grid=(P_TOTAL // PROJ_TN, rows // tm),
        in_specs=[pl.BlockSpec((tm, D_MODEL), lambda j, i: (i, 0)),
                  pl.BlockSpec((D_MODEL, PROJ_TN), lambda j, i: (0, j))],
        out_specs=pl.BlockSpec((tm, PROJ_TN), lambda j, i: (i, j)),
        out_shape=jax.ShapeDtypeStruct((rows, P_TOTAL), F32),
        compiler_params=_params("parallel", "parallel"),
        name="proj",
    )(u, w)


def _small_kernel(u_ref, w_ref, b_ref, o_ref):
    x = _dot(u_ref[...], w_ref[...]) + b_ref[...]
    lane = lax.broadcasted_iota(jnp.int32, (1, LANES), 1)
    sgn = jnp.where(lane < S_F, 1.0, -1.0)
    o_ref[...] = sgn * _softplus(sgn * x)


def _small(u, w, b, *, tm):
    rows = u.shape[0]
    return pl.pallas_call(
        _small_kernel,
        grid=(rows // tm,),
        in_specs=[pl.BlockSpec((tm, D_MODEL), lambda i: (i, 0)),
                  pl.BlockSpec((D_MODEL, LANES), lambda i: (0, 0)),
                  pl.BlockSpec((1, LANES), lambda i: (0, 0))],
        out_specs=pl.BlockSpec((tm, LANES), lambda i: (i, 0)),
        out_shape=jax.ShapeDtypeStruct((rows, LANES), F32),
        compiler_params=_params("parallel"),
        name="small",
    )(u, w, b)


HIST0 = 8 - (CONV_W - 1)


def _ssd_kernel(xbc_ref, z_ref, dt_ref, s0_ref, h0_ref, cw_ref, cb_ref, alog_ref, dskip_ref, norm_ref,
                e_ref, o_ref, sT_ref, hT_ref, st_sc, xp_sc, y_sc, *, n_pad):
    T = CHUNK
    c = pl.program_id(1)

    @pl.when(c == 0)
    def _():
        st_sc[...] = s0_ref[...]
        xp_sc[HIST0:8, :] = h0_ref[...]

    xraw = xbc_ref[...]
    xp_sc[8:8 + T, :] = xraw
    cw = cw_ref[...]
    conv = cb_ref[...] + cw[0:1] * xp_sc[HIST0:HIST0 + T, :]
    conv = conv + cw[1:2] * xp_sc[HIST0 + 1:HIST0 + 1 + T, :]
    conv = conv + cw[2:3] * xp_sc[HIST0 + 2:HIST0 + 2 + T, :]
    conv = conv + cw[3:4] * xraw
    xp_sc[HIST0:8, :] = xraw[T - (CONV_W - 1):T, :]
    xc = conv * jax.nn.sigmoid(conv)
    dtv = dt_ref[:, S_DT:S_DT + SSM_HEADS]
    if n_pad:
        valid = lax.broadcasted_iota(jnp.int32, (T, 1), 0) >= n_pad
        xc = jnp.where(valid, xc, 0.0)
        dtv = jnp.where(valid, dtv, 0.0)

    a = -jnp.exp(alog_ref[...])
    adt = a * dtv
    row = lax.broadcasted_iota(jnp.int32, (T, T), 0)
    col = lax.broadcasted_iota(jnp.int32, (T, T), 1)
    causal = col <= row
    tril = jnp.where(causal, 1.0, 0.0).astype(BF16)
    r32 = lax.broadcasted_iota(jnp.int32, (SSM_HEADS, SSM_HEADS), 0)
    c32 = lax.broadcasted_iota(jnp.int32, (SSM_HEADS, SSM_HEADS), 1)
    eye = jnp.where(r32 == c32, 1.0, 0.0).astype(BF16)
    expand = e_ref[...]

    a_cs = sum(_dot(tril, piece) for piece in _split3(adt))
    a_cs_t = sum(_dot_nt(eye, piece) for piece in _split3(a_cs))
    dt_e = sum(_dot(piece, expand) for piece in _split3(dtv))
    acs_e = sum(_dot(piece, expand) for piece in _split3(a_cs))

    xs = xc[:, :D_SSM]
    bm = xc[:, D_SSM:D_SSM + SSM_GROUPS * D_STATE].astype(BF16)
    cm = xc[:, D_SSM + SSM_GROUPS * D_STATE:].astype(BF16)
    xd = xs * dt_e
    al_e = acs_e[T - 1:T, :]
    xdw = (xd * jnp.exp(al_e - acs_e)).astype(BF16)
    cdec = jnp.exp(al_e)
    eacs = jnp.exp(acs_e)
    lane = lax.broadcasted_iota(jnp.int32, (1, D_SSM), 1)
    lo = (lane & ATTN_HEAD_DIM) == 0
    xd_lo = jnp.where(lo, xd, 0.0).astype(BF16)
    xd_hi = jnp.where(lo, 0.0, xd).astype(BF16)

    for g in range(SSM_GROUPS):
        gs = slice(g * GROUP_W, (g + 1) * GROUP_W)
        cmg = cm[:, g * D_STATE:(g + 1) * D_STATE]
        bmg = bm[:, g * D_STATE:(g + 1) * D_STATE]
        cb = _dot_nt(cmg, bmg)
        stg = st_sc[:, gs]
        yoff = _dot(cmg, stg.astype(BF16))
        for rr in range(2):
            pair = g * 2 + rr
            ps = slice(pair * LANES, (pair + 1) * LANES)
            ydiag = None
            for hh in range(2):
                h = pair * 2 + hh
                seg = a_cs[:, h:h + 1] - a_cs_t[h:h + 1, :]
                m = (cb * jnp.exp(jnp.where(causal, seg, -jnp.inf))).astype(BF16)
                d = _dot(m, (xd_lo, xd_hi)[hh][:, ps])
                ydiag = d if ydiag is None else ydiag + d
            y_sc[:, ps] = ydiag + yoff[:, rr * LANES:(rr + 1) * LANES] * eacs[:, ps]
        st_sc[:, gs] = stg * cdec[:, gs] + _dot_tn(bmg, xdw[:, gs])

    y = y_sc[...] + dskip_ref[...] * xs
    zz = z_ref[...]
    y = y * (zz * jax.nn.sigmoid(zz))
    for g in range(SSM_GROUPS):
        gs = slice(g * GROUP_W, (g + 1) * GROUP_W)
        yg = y[:, gs]
        yg = yg * lax.rsqrt(jnp.mean(yg * yg, axis=-1, keepdims=True) + EPS)
        o_ref[:, gs] = (yg * norm_ref[:, gs]).astype(o_ref.dtype)

    @pl.when(c == pl.num_programs(1) - 1)
    def _():
        sT_ref[...] = st_sc[...]
        hT_ref[...] = xp_sc[HIST0:8, :]


def _ssd(proj3, small3, s0, h0, cw, cb, alog, dskip, norm, expand, *, nb, b_off, shared_init, n_pad):
    nc = proj3.shape[1] // CHUNK
    init = (lambda b, c: (0, 0, 0)) if shared_init else (lambda b, c: (b, 0, 0))
    const = lambda b, c: (0, 0)
    return pl.pallas_call(
        functools.partial(_ssd_kernel, n_pad=n_pad),
        grid=(nb, nc),
        in_specs=[
            pl.BlockSpec((None, CHUNK, D_CONV), lambda b, c: (b + b_off, c, P_XBC // D_CONV)),
            pl.BlockSpec((None, CHUNK, D_SSM), lambda b, c: (b + b_off, c, P_Z // D_SSM)),
            pl.BlockSpec((None, CHUNK, LANES), lambda b, c: (b + b_off, c, 0)),
            pl.BlockSpec((None, D_STATE, D_SSM), init),
            pl.BlockSpec((None, CONV_W - 1, D_CONV), init),
            pl.BlockSpec((CONV_W, D_CONV), const),
            pl.BlockSpec((1, D_CONV), const),
            pl.BlockSpec((1, SSM_HEADS), const),
            pl.BlockSpec((1, D_SSM), const),
            pl.BlockSpec((1, D_SSM), const),
            pl.BlockSpec((SSM_HEADS, D_SSM), const),
        ],
        out_specs=[
            pl.BlockSpec((None, CHUNK, D_SSM), lambda b, c: (b, c, 0)),
            pl.BlockSpec((None, D_STATE, D_SSM), lambda b, c: (b, 0, 0)),
            pl.BlockSpec((None, CONV_W - 1, D_CONV), lambda b, c: (b, 0, 0)),
        ],
        out_shape=[
            jax.ShapeDtypeStruct((nb, nc * CHUNK, D_SSM), BF16),
            jax.ShapeDtypeStruct((nb, D_STATE, D_SSM), F32),
            jax.ShapeDtypeStruct((nb, CONV_W - 1, D_CONV), F32),
        ],
        scratch_shapes=[
            pltpu.VMEM((D_STATE, D_SSM), F32),
            pltpu.VMEM((8 + CHUNK, D_CONV), F32),
            pltpu.VMEM((CHUNK, D_SSM), F32),
        ],
        compiler_params=_params("parallel", "arbitrary"),
        name="ssd",
    )(proj3, proj3, small3, s0, h0, cw, cb, alog, dskip, norm, expand)


def _cumsum_kernel(x_ref, o_ref):
    n = x_ref.shape[-1] // CUMSUM_BLK
    r = lax.broadcasted_iota(jnp.int32, (CUMSUM_BLK, CUMSUM_BLK), 0)
    c = lax.broadcasted_iota(jnp.int32, (CUMSUM_BLK, CUMSUM_BLK), 1)
    upper = jnp.where(r <= c, 1.0, 0.0).astype(BF16)
    carry = jnp.zeros((ATTN_HEADS, 1), F32)
    for i in range(n):
        blk = slice(i * CUMSUM_BLK, (i + 1) * CUMSUM_BLK)
        cs = sum(_dot(piece, upper) for piece in _split3(x_ref[:, blk])) + carry
        o_ref[:, blk] = cs
        carry = cs[:, CUMSUM_BLK - 1:CUMSUM_BLK]


def _cumsum_t(x_t):
    b, h, l = x_t.shape
    spec = pl.BlockSpec((None, h, l), lambda i: (i, 0, 0))
    return pl.pallas_call(
        _cumsum_kernel, grid=(b,), in_specs=[spec], out_specs=spec,
        out_shape=jax.ShapeDtypeStruct(x_t.shape, F32),
        compiler_params=_params("parallel"), name="cumsum",
    )(x_t)


def _cum_logf(lf_t):
    l = lf_t.shape[2]
    lp = -(-l // CUMSUM_BLK) * CUMSUM_BLK
    return _cumsum_t(jnp.pad(lf_t, ((0, 0), (0, 0), (0, lp - l))))[:, :, :l]


BIAS_PIECES = 3


def _bias_lane0(head):
    return ATTN_HEAD_DIM if head % 2 == 0 else 0


def _bias_placement():
    sq = np.zeros((BIAS_PIECES, ATTN_HEADS, ATTN_HEADS * LANES), np.float32)
    sk = np.zeros_like(sq)
    one_q = np.zeros((1, ATTN_HEADS * LANES), np.float32)
    one_k = np.zeros_like(one_q)
    for h in range(ATTN_HEADS):
        base = h * LANES + _bias_lane0(h)
        for piece in range(BIAS_PIECES):
            sk[piece, h, base + piece] = -1.0
            one_q[0, base + piece] = 1.0
            sq[piece, h, base + BIAS_PIECES + piece] = 1.0
            one_k[0, base + BIAS_PIECES + piece] = 1.0
    return jnp.asarray(sq, BF16), jnp.asarray(sk, BF16), jnp.asarray(one_q), jnp.asarray(one_k)


def _attn_prep_kernel(q_ref, k_ref, v_ref, c_ref, sq_ref, sk_ref, oneq_ref, onek_ref, qo_ref, ko_ref, kt_ref,
                      vt_ref):
    pieces = _split3(c_ref[...])
    bias_q = sum(_dot(pc, sq_ref[n]) for n, pc in enumerate(pieces)) + oneq_ref[...]
    bias_k = sum(_dot(pc, sk_ref[n]) for n, pc in enumerate(pieces)) + onek_ref[...]
    lane = lax.broadcasted_iota(jnp.int32, (1, LANES), 1)
    lo = lane < ATTN_HEAD_DIM
    for p in range(HEAD_PAIRS):
        ps = slice(p * LANES, (p + 1) * LANES)
        q2 = q_ref[:, ps] * (ATTN_HEAD_DIM ** -0.5)
        k2 = k_ref[:, ps]
        for hh in range(2):
            h = 2 * p + hh
            hs = slice(h * LANES, (h + 1) * LANES)
            sel = lo if hh == 0 else jnp.logical_not(lo)
            qo_ref[h] = (jnp.where(sel, q2, 0.0) + bias_q[:, hs]).astype(BF16)
            ko_ref[h] = (jnp.where(sel, k2, 0.0) + bias_k[:, hs]).astype(BF16)
        for src, dst in ((k2, kt_ref), (v_ref[:, ps], vt_ref)):
            x_t = jnp.transpose(src)
            dst[2 * p] = x_t[:ATTN_HEAD_DIM]
            dst[2 * p + 1] = x_t[ATTN_HEAD_DIM:]


def _attn_prep(proj3, c_rows, consts, *, nb, b_off, tile):
    l = c_rows.shape[1]
    col = lambda off: off // D_ATTN
    sq, sk, one_q, one_k = consts
    c3 = lambda b, i: (0, 0, 0)
    c2 = lambda b, i: (0, 0)
    return pl.pallas_call(
        _attn_prep_kernel,
        grid=(nb, l // tile),
        in_specs=[
            pl.BlockSpec((None, tile, D_ATTN), lambda b, i: (b + b_off, i, col(P_Q))),
            pl.BlockSpec((None, tile, D_ATTN), lambda b, i: (b + b_off, i, col(P_K))),
            pl.BlockSpec((None, tile, D_ATTN), lambda b, i: (b + b_off, i, col(P_V))),
            pl.BlockSpec((None, tile, ATTN_HEADS), lambda b, i: (b, i, 0)),
            pl.BlockSpec(sq.shape, c3), pl.BlockSpec(sk.shape, c3),
            pl.BlockSpec(one_q.shape, c2), pl.BlockSpec(one_k.shape, c2),
        ],
        out_specs=[
            pl.BlockSpec((None, ATTN_HEADS, tile, LANES), lambda b, i: (b, 0, i, 0)),
            pl.BlockSpec((None, ATTN_HEADS, tile, LANES), lambda b, i: (b, 0, i, 0)),
            pl.BlockSpec((None, ATTN_HEADS, ATTN_HEAD_DIM, tile), lambda b, i: (b, 0, 0, i)),
            pl.BlockSpec((None, ATTN_HEADS, ATTN_HEAD_DIM, tile), lambda b, i: (b, 0, 0, i)),
        ],
        out_shape=[
            jax.ShapeDtypeStruct((nb, ATTN_HEADS, l, LANES), BF16),
            jax.ShapeDtypeStruct((nb, ATTN_HEADS, l, LANES), BF16),
            jax.ShapeDtypeStruct((nb, ATTN_HEADS, ATTN_HEAD_DIM, l), F32),
            jax.ShapeDtypeStruct((nb, ATTN_HEADS, ATTN_HEAD_DIM, l), F32),
        ],
        compiler_params=_params("parallel", "parallel"),
        name="attn_prep",
    )(proj3, proj3, proj3, c_rows, sq, sk, one_q, one_k)


QBLK = 256


def _attn_prompt_kernel(it_ref, jt_ref, q_ref, k_ref, vt_ref, km_ref, vtm_ref, o_ref, m_sc, l_sc, acc_sc,
                        *, meta_lo, meta_hi):
    t = pl.program_id(1)
    i = it_ref[t]
    j = jt_ref[t]
    nqb = ATT_T // QBLK

    def q_block(h, r):
        return q_ref[h, r * QBLK:(r + 1) * QBLK, :]

    def head_update(h, scores, first):
        if not first:
            m_old_all, l_old_all, acc_old_all = m_sc[h], l_sc[h], acc_sc[h]
        m_out, l_out, acc_out = [], [], []
        for r, (s_t, v_t) in enumerate(scores):
            qs = slice(r * QBLK, (r + 1) * QBLK)
            mx = jnp.max(s_t, axis=0, keepdims=True)
            if first:
                m_new = mx
                pr = jnp.exp(s_t - m_new)
                l_new = jnp.sum(pr, axis=0, keepdims=True)
                acc_new = _dot(v_t, pr.astype(BF16))
            else:
                m_old = m_old_all[:, qs]
                m_new = jnp.maximum(m_old, mx)
                alpha = jnp.exp(m_old - m_new)
                pr = jnp.exp(s_t - m_new)
                l_new = alpha * l_old_all[:, qs] + jnp.sum(pr, axis=0, keepdims=True)
                acc_new = acc_old_all[:, qs] * alpha + _dot(v_t, pr.astype(BF16))
            m_out.append(m_new)
            l_out.append(l_new)
            acc_out.append(acc_new)
        m_sc[h] = jnp.concatenate(m_out, axis=1)
        l_sc[h] = jnp.concatenate(l_out, axis=1)
        acc_sc[h] = jnp.concatenate(acc_out, axis=1)

    def run(score_fn, first):
        nxt = [score_fn(0, r) for r in range(nqb)]
        for h in range(ATTN_HEADS):
            cur = nxt
            if h + 1 < ATTN_HEADS:
                nxt = [score_fn(h + 1, r) for r in range(nqb)]
            head_update(h, cur, first)

    @pl.when(j == 0)
    def _():
        krow = lax.broadcasted_iota(jnp.int32, (km_ref.shape[1], 1), 0)
        is_meta = jnp.logical_and(krow >= meta_lo, krow < meta_hi)
        run(lambda h, r: (jnp.where(is_meta, _dot_nt(km_ref[h], q_block(h, r)), -jnp.inf),
                          vtm_ref[h].astype(BF16)), True)

    @pl.when(j < i)
    def _():
        run(lambda h, r: (_dot_nt(k_ref[h], q_block(h, r)), vt_ref[h].astype(BF16)), False)

    @pl.when(j == i)
    def _():
        def diag_scores(h, r):
            nk = (r + 1) * QBLK
            s_t = _dot_nt(k_ref[h, :nk, :], q_block(h, r))
            krow = lax.broadcasted_iota(jnp.int32, (nk, QBLK), 0)
            qcol = lax.broadcasted_iota(jnp.int32, (nk, QBLK), 1) + r * QBLK
            return jnp.where(krow <= qcol, s_t, -jnp.inf), vt_ref[h, :, :nk].astype(BF16)

        run(diag_scores, False)
        for p in range(HEAD_PAIRS):
            o_t = jnp.concatenate([acc_sc[2 * p] / l_sc[2 * p], acc_sc[2 * p + 1] / l_sc[2 * p + 1]], axis=0)
            o_ref[:, p * LANES:(p + 1) * LANES] = jnp.transpose(o_t).astype(o_ref.dtype)


def _attn_prompt(q_aug, k_aug, v_t, km_aug, vm_t, *, meta_lo):
    nb, _, l, _ = q_aug.shape
    nt = l // ATT_T
    it = np.array([i for i in range(nt) for _ in range(i + 1)], np.int32)
    jt = np.array([j for i in range(nt) for j in range(i + 1)], np.int32)
    mrows = km_aug.shape[2]
    grid_spec = pltpu.PrefetchScalarGridSpec(
        num_scalar_prefetch=2,
        grid=(nb, len(it)),
        in_specs=[
            pl.BlockSpec((None, ATTN_HEADS, ATT_T, LANES), lambda b, t, it, jt: (b, 0, it[t], 0)),
            pl.BlockSpec((None, ATTN_HEADS, ATT_T, LANES), lambda b, t, it, jt: (b, 0, jt[t], 0)),
            pl.BlockSpec((None, ATTN_HEADS, ATTN_HEAD_DIM, ATT_T), lambda b, t, it, jt: (b, 0, 0, jt[t])),
            pl.BlockSpec((None, ATTN_HEADS, mrows, LANES), lambda b, t, it, jt: (0, 0, 0, 0)),
            pl.BlockSpec((None, ATTN_HEADS, ATTN_HEAD_DIM, mrows), lambda b, t, it, jt: (0, 0, 0, 0)),
        ],
        out_specs=pl.BlockSpec((None, ATT_T, D_ATTN), lambda b, t, it, jt: (b, it[t], 0)),
        scratch_shapes=[
            pltpu.VMEM((ATTN_HEADS, 1, ATT_T), F32),
            pltpu.VMEM((ATTN_HEADS, 1, ATT_T), F32),
            pltpu.VMEM((ATTN_HEADS, ATTN_HEAD_DIM, ATT_T), F32),
        ],
    )
    return pl.pallas_call(
        functools.partial(_attn_prompt_kernel, meta_lo=meta_lo, meta_hi=meta_lo + N_META),
        grid_spec=grid_spec,
        out_shape=jax.ShapeDtypeStruct((nb, l, D_ATTN), BF16),
        compiler_params=_params("parallel", "arbitrary"),
        name="attn_prompt",
    )(jnp.asarray(it), jnp.asarray(jt), q_aug, k_aug, v_t, km_aug, vm_t)


def _attn_sample_kernel(q_ref, kn_ref, vn_ref, km_ref, vm_ref, kct_ref, vct_ref, cq_ref, ckm_ref, ckc_ref,
                        ckn_ref, o_ref):
    h = pl.program_id(1)
    T = CHUNK
    D = ATTN_HEAD_DIM
    first = h % 2 == 0

    def half(ref):
        x = ref[...]
        return jnp.where(first, x[:, :D], x[:, D:])

    qh = (half(q_ref) * (D ** -0.5)).astype(BF16)
    cq = cq_ref[...]
    row = lax.broadcasted_iota(jnp.int32, (T, T), 0)
    col = lax.broadcasted_iota(jnp.int32, (T, T), 1)
    s_m = _dot_nt(qh, half(km_ref).astype(BF16)) + cq - ckm_ref[...]
    s_c = _dot(qh, kct_ref[...].astype(BF16)) + cq - ckc_ref[...]
    s_n = jnp.where(col <= row, _dot_nt(qh, half(kn_ref).astype(BF16)) + cq - ckn_ref[...], -jnp.inf)
    m = jnp.maximum(jnp.maximum(jnp.max(s_m, axis=1, keepdims=True), jnp.max(s_c, axis=1, keepdims=True)),
                    jnp.max(s_n, axis=1, keepdims=True))
    p_m = jnp.exp(s_m - m)
    p_c = jnp.exp(s_c - m)
    p_n = jnp.exp(s_n - m)
    l = (jnp.sum(p_m, axis=1, keepdims=True) + jnp.sum(p_c, axis=1, keepdims=True)
         + jnp.sum(p_n, axis=1, keepdims=True))
    o = (_dot(p_m.astype(BF16), half(vm_ref).astype(BF16)) + _dot_nt(p_c.astype(BF16), vct_ref[...].astype(BF16))
         + _dot(p_n.astype(BF16), half(vn_ref).astype(BF16)))
    o_ref[...] = (o / l).astype(o_ref.dtype)


def _attn_sample(proj3, proj2, meta_blk, cache_k, cache_v, cq, ckm_t, ckc_t, ckn_t, *, nb):
    past = cache_k.shape[3]
    lb = lambda off: off // LANES
    pair = lambda off: (lambda b, h: (b, 0, lb(off) + h // 2))
    ckspec = lambda n: pl.BlockSpec((None, 1, n), lambda b, h: (b * ATTN_HEADS + h, 0, 0))
    return pl.pallas_call(
        _attn_sample_kernel,
        grid=(nb, ATTN_HEADS),
        in_specs=[
            pl.BlockSpec((None, CHUNK, LANES), pair(P_Q)),
            pl.BlockSpec((None, CHUNK, LANES), pair(P_K)),
            pl.BlockSpec((None, CHUNK, LANES), pair(P_V)),
            pl.BlockSpec((N_META, LANES), lambda b, h: (meta_blk, lb(P_K) + h // 2)),
            pl.BlockSpec((N_META, LANES), lambda b, h: (meta_blk, lb(P_V) + h // 2)),
            pl.BlockSpec((None, None, ATTN_HEAD_DIM, past), lambda b, h: (b, h, 0, 0)),
            pl.BlockSpec((None, None, ATTN_HEAD_DIM, past), lambda b, h: (b, h, 0, 0)),
            pl.BlockSpec((None, None, CHUNK, 1), lambda b, h: (b, h, 0, 0)),
            ckspec(N_META), ckspec(past), ckspec(CHUNK),
        ],
        out_specs=pl.BlockSpec((None, None, CHUNK, ATTN_HEAD_DIM), lambda b, h: (b, h, 0, 0)),
        out_shape=jax.ShapeDtypeStruct((nb, ATTN_HEADS, CHUNK, ATTN_HEAD_DIM), BF16),
        compiler_params=_params("parallel", "parallel"),
        name="attn_sample",
    )(proj3, proj3, proj3, proj2, proj2, cache_k, cache_v, cq, ckm_t, ckc_t, ckn_t)


def _merge_kernel(os_ref, oa_ref, gs_ref, ga_ref, h_ref, wbs_ref, wba_ref, wo_ref, o_ref):
    bs = _dot(os_ref[...], wbs_ref[...])
    ba = _dot(oa_ref[...], wba_ref[...])
    merged = jax.nn.sigmoid(gs_ref[...]) * bs + jax.nn.sigmoid(ga_ref[...]) * ba
    o_ref[...] = h_ref[...] + _dot(merged.astype(BF16), wo_ref[...])


def _merge(o_ssm, o_attn, proj, h, w_bs, w_ba, w_o, *, tm):
    rows = o_ssm.shape[0]
    const = lambda i: (0, 0)
    return pl.pallas_call(
        _merge_kernel,
        grid=(rows // tm,),
        in_specs=[
            pl.BlockSpec((tm, D_SSM), lambda i: (i, 0)),
            pl.BlockSpec((tm, D_ATTN), lambda i: (i, 0)),
            pl.BlockSpec((tm, D_MODEL), lambda i: (i, P_GS // D_MODEL)),
            pl.BlockSpec((tm, D_MODEL), lambda i: (i, P_GA // D_MODEL)),
            pl.BlockSpec((tm, D_MODEL), lambda i: (i, 0)),
            pl.BlockSpec((D_SSM, D_MODEL), const),
            pl.BlockSpec((D_ATTN, D_MODEL), const),
            pl.BlockSpec((D_MODEL, D_MODEL), const),
        ],
        out_specs=pl.BlockSpec((tm, D_MODEL), lambda i: (i, 0)),
        out_shape=jax.ShapeDtypeStruct((rows, D_MODEL), F32),
        compiler_params=_params("parallel"),
        name="merge",
    )(o_ssm, o_attn, proj, proj, h, w_bs, w_ba, w_o)


def kernel(x_prompt, x_sample, cache_k, cache_v, cache_logf, state_ssm, state_conv, meta_tokens, norm_ffn1, ffn1_w_in, ffn1_w_out, norm_mix, w_in, conv_w, conv_b, dt_bias, a_log, d_skip, f_bias, ssm_norm, w_br_ssm, w_br_attn, w_out, norm_ffn2, ffn2_w_in, ffn2_w_out, norm_final):
    assert norm_ffn1.shape[0] == 1, "single-layer trunk"
    bp, seq, _ = x_prompt.shape
    bs, dseq, _ = x_sample.shape
    assert dseq == CHUNK and seq % ATT_T == 0
    n_s = bs * dseq
    rows_small = n_s + 2 * CHUNK
    meta_row0 = n_s + CHUNK - N_META
    meta_blk = meta_row0 // N_META

    row = lambda v: v.reshape(1, -1).astype(F32)
    w = w_in[0]
    w_main = jnp.concatenate([w[:, OFF_XBC:OFF_DT], w[:, :OFF_XBC], w[:, OFF_Q:OFF_F], w[:, OFF_GS:]],
                             axis=1).astype(BF16)
    pad_small = LANES - SSM_HEADS - ATTN_HEADS
    w_small = jnp.concatenate([w[:, OFF_DT:OFF_Q], w[:, OFF_F:OFF_GS], jnp.zeros((D_MODEL, pad_small), F32)],
                              axis=1).astype(BF16)
    b_small = jnp.concatenate([dt_bias[0], f_bias[0], jnp.zeros((pad_small,), F32)]).reshape(1, LANES)
    w1i, w1o = ffn1_w_in[0].astype(BF16), ffn1_w_out[0].astype(BF16)
    w2i, w2o = ffn2_w_in[0].astype(BF16), ffn2_w_out[0].astype(BF16)
    w_bs, w_ba, w_o = w_br_ssm[0].astype(BF16), w_br_attn[0].astype(BF16), w_out[0].astype(BF16)
    expand = jnp.asarray(np.repeat(np.eye(SSM_HEADS, dtype=np.float32), SSM_HEAD_DIM, axis=1), BF16)
    ssd_consts = (conv_w[0], row(conv_b[0]), row(a_log[0]), row(jnp.repeat(d_skip[0], SSM_HEAD_DIM)),
                  row(ssm_norm[0]), expand)

    def front(x, tm_ffn, tm_proj):
        h1, u = _ffn(x, row(norm_ffn1[0]), w1i, w1o, row(norm_mix[0]), tm=tm_ffn, emit_h=True, u_dtype=BF16)
        return h1, _proj(u, w_main, tm=tm_proj), _small(u, w_small, b_small, tm=tm_proj)

    xp = x_prompt.reshape(bp * seq, D_MODEL)
    h1_p, pm_p, ps_p = front(xp, 512, 1024)
    x_small = jnp.concatenate([x_sample.reshape(n_s, D_MODEL), jnp.zeros((CHUNK - N_META, D_MODEL), F32),
                               meta_tokens, jnp.zeros((CHUNK, D_MODEL), F32)], axis=0)
    h1_s, pm_s, ps_s = front(x_small, rows_small // 2, rows_small // 2)

    pm_p3 = pm_p.reshape(bp, seq, P_TOTAL)
    ps_p3 = ps_p.reshape(bp, seq, LANES)
    pm_s3 = pm_s.reshape(rows_small // CHUNK, CHUNK, P_TOTAL)
    ps_s3 = ps_s.reshape(rows_small // CHUNK, CHUNK, LANES)

    zero_state = jnp.zeros((1, D_STATE, D_SSM), F32)
    zero_hist = jnp.zeros((1, CONV_W - 1, D_CONV), F32)
    _, st_m, hist_m = _ssd(pm_s3, ps_s3, zero_state, zero_hist, *ssd_consts,
                           nb=1, b_off=bs, shared_init=True, n_pad=CHUNK - N_META)
    o_ssm_p, st_p, hist_p = _ssd(pm_p3, ps_p3, st_m, hist_m, *ssd_consts,
                                 nb=bp, b_off=0, shared_init=True, n_pad=0)
    st_s0 = jnp.transpose(state_ssm[0].reshape(bs, D_SSM, D_STATE), (0, 2, 1))
    o_ssm_s, st_s, hist_s = _ssd(pm_s3, ps_s3, st_s0, state_conv[0], *ssd_consts,
                                 nb=bs, b_off=0, shared_init=False, n_pad=0)

    t_last = lambda a: jnp.swapaxes(a, -1, -2)
    lf_m = t_last(ps_s[meta_row0:meta_row0 + N_META, S_F:S_F + ATTN_HEADS])
    lf_p = t_last(ps_p3[:, :, S_F:S_F + ATTN_HEADS])
    lf_s = ps_s[:n_s, S_F:S_F + ATTN_HEADS].reshape(bs, dseq, ATTN_HEADS)
    lf_all_p = jnp.concatenate([jnp.broadcast_to(lf_m[None], (bp, ATTN_HEADS, N_META)), lf_p], axis=2)
    lf_all_s = jnp.concatenate([jnp.broadcast_to(lf_m[None], (bs, ATTN_HEADS, N_META)),
                                t_last(cache_logf[0].astype(F32)), t_last(lf_s)], axis=2)
    past = cache_logf.shape[2]
    c_p = _cum_logf(lf_all_p)
    c_s = _cum_logf(lf_all_s)

    prep_consts = _bias_placement()
    q_aug, k_aug, k_t, v_t = _attn_prep(pm_p3, t_last(c_p[:, :, N_META:]), prep_consts,
                                        nb=bp, b_off=0, tile=ATT_T)
    meta_tile = 2 * CHUNK
    meta_lo = meta_row0 - n_s
    c_meta = jnp.pad(t_last(c_p[:1, :, :N_META]), ((0, 0), (meta_lo, meta_tile - meta_lo - N_META), (0, 0)))
    _, km_aug, km_t, vm_t = _attn_prep(pm_s.reshape(rows_small // meta_tile, meta_tile, P_TOTAL), c_meta,
                                       prep_consts, nb=1, b_off=n_s // meta_tile, tile=meta_tile)
    o_attn_p = _attn_prompt(q_aug, k_aug, v_t, km_aug, vm_t, meta_lo=meta_lo)
    per_head = lambda c: c.reshape(bs * ATTN_HEADS, 1, c.shape[2])
    feat_major = lambda a: jnp.transpose(a, (0, 2, 3, 1))
    o_attn_s = _attn_sample(pm_s3, pm_s, meta_blk, feat_major(cache_k[0]), feat_major(cache_v[0]),
                            c_s[:, :, N_META + past:, None],
                            per_head(c_s[:, :, :N_META]), per_head(c_s[:, :, N_META:N_META + past]),
                            per_head(c_s[:, :, N_META + past:]), nb=bs)
    o_attn_s = jnp.transpose(o_attn_s, (0, 2, 1, 3))

    def back(o_ssm, o_attn, pm, h1, tm):
        h2 = _merge(o_ssm, o_attn, pm, h1, w_bs, w_ba, w_o, tm=tm)
        return _ffn(h2, row(norm_ffn2[0]), w2i, w2o, row(norm_final), tm=tm, emit_h=False, u_dtype=F32)[0]

    y_prompt = back(o_ssm_p.reshape(bp * seq, D_SSM), o_attn_p.reshape(bp * seq, D_ATTN), pm_p, h1_p, 512)
    y_sample = back(o_ssm_s.reshape(n_s, D_SSM), o_attn_s.reshape(n_s, D_ATTN), pm_s, h1_s, 512)

    heads = lambda a: a.reshape(1, a.shape[0], a.shape[1], ATTN_HEADS, ATTN_HEAD_DIM)

    def with_meta(x_t, m_t):
        m_t = jnp.broadcast_to(m_t[:, :, :, meta_lo:meta_lo + N_META], (bp, ATTN_HEADS, ATTN_HEAD_DIM, N_META))
        return jnp.transpose(jnp.concatenate([m_t, x_t], axis=3), (0, 3, 1, 2))[None]

    k_prompt = with_meta(k_t, km_t)
    v_prompt = with_meta(v_t, vm_t)
    logf_prompt = t_last(lf_all_p)
    k_sample = heads(pm_s[:n_s, P_K:P_K + D_ATTN].reshape(bs, dseq, D_ATTN))
    v_sample = heads(pm_s[:n_s, P_V:P_V + D_ATTN].reshape(bs, dseq, D_ATTN))
    state_out = lambda st: jnp.transpose(st, (0, 2, 1)).reshape(1, st.shape[0], SSM_HEADS, SSM_HEAD_DIM, D_STATE)
    return (y_prompt.reshape(bp, seq, D_MODEL), y_sample.reshape(bs, dseq, D_MODEL),
            k_prompt, v_prompt, logf_prompt[None], state_out(st_p), hist_p[None],
            k_sample, v_sample, lf_s[None], state_out(st_s), hist_s[None])
```

```python
import functools

import numpy as np
import jax
import jax.numpy as jnp
from jax import lax
from jax.experimental import pallas as pl
from jax.experimental.pallas import tpu as pltpu

F32 = jnp.float32
BF16 = jnp.bfloat16

D_MODEL = 1024
N_META = 16
CHUNK = 64
EPS = 1e-6
D_SSM = 2048
SSM_HEADS = 32
SSM_HEAD_DIM = 64
SSM_GROUPS = 8
D_STATE = 128
CONV_W = 4
D_CONV = D_SSM + 2 * SSM_GROUPS * D_STATE
ATTN_HEADS = 16
ATTN_HEAD_DIM = 64
D_ATTN = ATTN_HEADS * ATTN_HEAD_DIM
D_FF = 2816
HEAD_PAIRS = ATTN_HEADS // 2
LANES = 128
GROUP_W = D_SSM // SSM_GROUPS

OFF_XBC = D_SSM
OFF_DT = OFF_XBC + D_CONV
OFF_Q = OFF_DT + SSM_HEADS
OFF_K = OFF_Q + D_ATTN
OFF_V = OFF_K + D_ATTN
OFF_F = OFF_V + D_ATTN
OFF_GS = OFF_F + ATTN_HEADS
OFF_GA = OFF_GS + D_MODEL

P_XBC = 0
P_Z = D_CONV
P_Q = P_Z + D_SSM
P_K = P_Q + D_ATTN
P_V = P_K + D_ATTN
P_GS = P_V + D_ATTN
P_GA = P_GS + D_MODEL
P_TOTAL = P_GA + D_MODEL
S_DT = 0
S_F = SSM_HEADS

VMEM_LIMIT = 56 * 1024 * 1024

FF_CHUNK = D_FF // 2
PROJ_TN = P_TOTAL // 4
ATT_T = 512
CUMSUM_BLK = 256


def _dot(a, b):
    return jnp.dot(a, b, preferred_element_type=F32)


def _dot_nt(a, b):
    return lax.dot_general(a, b, (((1,), (1,)), ((), ())), preferred_element_type=F32)


def _dot_tn(a, b):
    return lax.dot_general(a, b, (((0,), (0,)), ((), ())), preferred_element_type=F32)


def _split3(x):
    hi = x.astype(BF16)
    r = x - hi.astype(F32)
    mid = r.astype(BF16)
    lo = (r - mid.astype(F32)).astype(BF16)
    return hi, mid, lo


def _rmsnorm(x, g):
    return x * lax.rsqrt(jnp.mean(x * x, axis=-1, keepdims=True) + EPS) * g


def _softplus(x):
    return jnp.maximum(x, 0.0) + jnp.log1p(jnp.exp(-jnp.abs(x)))


def _params(*sem):
    return pltpu.CompilerParams(dimension_semantics=sem, vmem_limit_bytes=VMEM_LIMIT)


def _ffn_kernel(x_ref, g1_ref, wa_ref, wb_ref, wo_ref, g2_ref, *refs, emit_h):
    if emit_h:
        h_ref, u_ref, xn_sc, acc_sc = refs
    else:
        u_ref, xn_sc, acc_sc = refs
    c = pl.program_id(1)

    @pl.when(c == 0)
    def _():
        xn_sc[...] = _rmsnorm(x_ref[...], g1_ref[...]).astype(BF16)
        acc_sc[...] = jnp.zeros_like(acc_sc)

    xn = xn_sc[...]
    a = _dot(xn, wa_ref[...])
    b = _dot(xn, wb_ref[...])
    g = (a * jax.nn.sigmoid(a) * b).astype(BF16)
    acc_sc[...] += _dot(g, wo_ref[...])

    @pl.when(c == pl.num_programs(1) - 1)
    def _():
        h = x_ref[...] + 0.5 * acc_sc[...]
        if emit_h:
            h_ref[...] = h
        u_ref[...] = _rmsnorm(h, g2_ref[...]).astype(u_ref.dtype)


def _ffn(x, g1, w_in, w_out, g2, *, tm, emit_h, u_dtype):
    rows = x.shape[0]
    nfc = D_FF // FF_CHUNK
    row_spec = pl.BlockSpec((tm, D_MODEL), lambda i, c: (i, 0))
    vec_spec = pl.BlockSpec((1, D_MODEL), lambda i, c: (0, 0))
    out_shape = [jax.ShapeDtypeStruct((rows, D_MODEL), u_dtype)]
    out_specs = [row_spec]
    if emit_h:
        out_shape = [jax.ShapeDtypeStruct((rows, D_MODEL), F32)] + out_shape
        out_specs = [row_spec] + out_specs
    return pl.pallas_call(
        functools.partial(_ffn_kernel, emit_h=emit_h),
        grid=(rows // tm, nfc),
        in_specs=[
            row_spec, vec_spec,
            pl.BlockSpec((D_MODEL, FF_CHUNK), lambda i, c: (0, c)),
            pl.BlockSpec((D_MODEL, FF_CHUNK), lambda i, c: (0, nfc + c)),
            pl.BlockSpec((FF_CHUNK, D_MODEL), lambda i, c: (c, 0)),
            vec_spec,
        ],
        out_specs=out_specs,
        out_shape=out_shape,
        scratch_shapes=[pltpu.VMEM((tm, D_MODEL), BF16), pltpu.VMEM((tm, D_MODEL), F32)],
        compiler_params=_params("parallel", "arbitrary"),
        name="ffn",
    )(x, g1, w_in, w_in, w_out, g2)


def _proj_kernel(u_ref, w_ref, o_ref):
    o_ref[...] = _dot(u_ref[...], w_ref[...])


def _proj(u, w, *, tm):
    rows = u.shape[0]
    return pl.pallas_call(
        _proj_kernel,
        grid=(P_TOTAL // PROJ_TN, rows // tm),
        in_specs=[pl.BlockSpec((tm, D_MODEL), lambda j, i: (i, 0)),
                  pl.BlockSpec((D_MODEL, PROJ_TN), lambda j, i: (0, j))],
        out_specs=pl.BlockSpec((tm, PROJ_TN), lambda j, i: (i, j)),
        out_shape=jax.ShapeDtypeStruct((rows, P_TOTAL), F32),
        compiler_params=_params("parallel", "parallel"),
        name="proj",
    )(u, w)


def _small_kernel(u_ref, w_ref, b_ref, o_ref):
    x = _dot(u_ref[...], w_ref[...]) + b_ref[...]
    lane = lax.broadcasted_iota(jnp.int32, (1, LANES), 1)
    sgn = jnp.where(lane < S_F, 1.0, -1.0)
    o_ref[...] = sgn * _softplus(sgn * x)


def _small(u, w, b, *, tm):
    rows = u.shape[0]
    return pl.pallas_call(
        _small_kernel,
        grid=(rows // tm,),
        in_specs=[pl.BlockSpec((tm, D_MODEL), lambda i: (i, 0)),
                  pl.BlockSpec((D_MODEL, LANES), lambda i: (0, 0)),
                  pl.BlockSpec((1, LANES), lambda i: (0, 0))],
        out_specs=pl.BlockSpec((tm, LANES), lambda i: (i, 0)),
        out_shape=jax.ShapeDtypeStruct((rows, LANES), F32),
        compiler_params=_params("parallel"),
        name="small",
    )(u, w, b)


HIST0 = 8 - (CONV_W - 1)


def _ssd_kernel(xbc_ref, z_ref, dt_ref, s0_ref, h0_ref, cw_ref, cb_ref, alog_ref, dskip_ref, norm_ref,
                e_ref, o_ref, sT_ref, hT_ref, st_sc, xp_sc, y_sc, *, n_pad):
    T = CHUNK
    c = pl.program_id(1)

    @pl.when(c == 0)
    def _():
        st_sc[...] = s0_ref[...]
        xp_sc[HIST0:8, :] = h0_ref[...]

    xraw = xbc_ref[...]
    xp_sc[8:8 + T, :] = xraw
    cw = cw_ref[...]
    xp = xp_sc[...]
    conv = cb_ref[...]
    for j in range(CONV_W - 1):
        conv = conv + cw[j:j + 1] * pltpu.roll(xp, CONV_W - 1 - j, axis=0)[8:8 + T, :]
    conv = conv + cw[3:4] * xraw
    xp_sc[HIST0:8, :] = xraw[T - (CONV_W - 1):T, :]
    xc = conv * jax.nn.sigmoid(conv)
    dtv = dt_ref[:, S_DT:S_DT + SSM_HEADS]
    if n_pad:
        valid = lax.broadcasted_iota(jnp.int32, (T, 1), 0) >= n_pad
        xc = jnp.where(valid, xc, 0.0)
        dtv = jnp.where(valid, dtv, 0.0)

    a = -jnp.exp(alog_ref[...])
    adt = a * dtv
    row = lax.broadcasted_iota(jnp.int32, (T, T), 0)
    col = lax.broadcasted_iota(jnp.int32, (T, T), 1)
    causal = col <= row
    tril = jnp.where(causal, 1.0, 0.0).astype(BF16)
    r32 = lax.broadcasted_iota(jnp.int32, (SSM_HEADS, SSM_HEADS), 0)
    c32 = lax.broadcasted_iota(jnp.int32, (SSM_HEADS, SSM_HEADS), 1)
    eye = jnp.where(r32 == c32, 1.0, 0.0).astype(BF16)
    expand3 = e_ref[...]

    def expand(v):
        return _dot(jnp.concatenate(_split3(v), axis=1), expand3)

    a_cs = sum(_dot(tril, piece) for piece in _split3(adt))
    a_cs_t = sum(_dot_nt(eye, piece) for piece in _split3(a_cs))
    a_last = a_cs[T - 1:T, :]
    dt_e = expand(dtv)
    wdec_e = expand(jnp.exp(a_last - a_cs) * dtv)
    eacs = expand(jnp.exp(a_cs))
    cdec = eacs[T - 1:T, :]

    xs = xc[:, :D_SSM]
    bm = xc[:, D_SSM:D_SSM + SSM_GROUPS * D_STATE].astype(BF16)
    cm = xc[:, D_SSM + SSM_GROUPS * D_STATE:].astype(BF16)
    xd = xs * dt_e
    xdw = (xs * wdec_e).astype(BF16)
    lane = lax.broadcasted_iota(jnp.int32, (1, D_SSM), 1)
    lo = (lane & SSM_HEAD_DIM) == 0
    xd_lo = jnp.where(lo, xd, 0.0).astype(BF16)
    xd_hi = jnp.where(lo, 0.0, xd).astype(BF16)
    lo2 = lax.broadcasted_iota(jnp.int32, (1, LANES), 1) < SSM_HEAD_DIM
    step2 = lax.broadcasted_iota(jnp.int32, (T, LANES), 1) & (SSM_HEAD_DIM - 1)
    causal2 = step2 <= lax.broadcasted_iota(jnp.int32, (T, LANES), 0)

    for g in range(SSM_GROUPS):
        gs = slice(g * GROUP_W, (g + 1) * GROUP_W)
        cmg = cm[:, g * D_STATE:(g + 1) * D_STATE]
        bmg = bm[:, g * D_STATE:(g + 1) * D_STATE]
        cb2 = _dot_nt(cmg, jnp.concatenate([bmg, bmg], axis=0))
        stg = st_sc[:, gs]
        yoff = _dot(cmg, stg.astype(BF16))
        for rr in range(2):
            pair = g * 2 + rr
            ps = slice(pair * LANES, (pair + 1) * LANES)
            ha, hb = 2 * pair, 2 * pair + 1
            seg = (jnp.where(lo2, a_cs[:, ha:ha + 1], a_cs[:, hb:hb + 1])
                   - jnp.concatenate([a_cs_t[ha:ha + 1, :], a_cs_t[hb:hb + 1, :]], axis=1))
            m2 = (cb2 * jnp.exp(jnp.where(causal2, seg, -jnp.inf))).astype(BF16)
            ydiag = _dot(m2, jnp.concatenate([xd_lo[:, ps], xd_hi[:, ps]], axis=0))
            y_sc[:, ps] = ydiag + yoff[:, rr * LANES:(rr + 1) * LANES] * eacs[:, ps]
        st_sc[:, gs] = stg * cdec[:, gs] + _dot_tn(bmg, xdw[:, gs])

    y = y_sc[...] + dskip_ref[...] * xs
    zz = z_ref[...]
    y = y * (zz * jax.nn.sigmoid(zz))
    for g in range(SSM_GROUPS):
        gs = slice(g * GROUP_W, (g + 1) * GROUP_W)
        yg = y[:, gs]
        yg = yg * lax.rsqrt(jnp.mean(yg * yg, axis=-1, keepdims=True) + EPS)
        o_ref[:, gs] = (yg * norm_ref[:, gs]).astype(o_ref.dtype)

    @pl.when(c == pl.num_programs(1) - 1)
    def _():
        sT_ref[...] = st_sc[...]
        hT_ref[...] = xp_sc[HIST0:8, :]


def _ssd(proj3, small3, s0, h0, cw, cb, alog, dskip, norm, expand, *, nb, b_off, shared_init, n_pad):
    nc = proj3.shape[1] // CHUNK
    init = (lambda b, c: (0, 0, 0)) if shared_init else (lambda b, c: (b, 0, 0))
    const = lambda b, c: (0, 0)
    return pl.pallas_call(
        functools.partial(_ssd_kernel, n_pad=n_pad),
        grid=(nb, nc),
        in_specs=[
            pl.BlockSpec((None, CHUNK, D_CONV), lambda b, c: (b + b_off, c, P_XBC // D_CONV)),
            pl.BlockSpec((None, CHUNK, D_SSM), lambda b, c: (b + b_off, c, P_Z // D_SSM)),
            pl.BlockSpec((None, CHUNK, LANES), lambda b, c: (b + b_off, c, 0)),
            pl.BlockSpec((None, D_STATE, D_SSM), init),
            pl.BlockSpec((None, CONV_W - 1, D_CONV), init),
            pl.BlockSpec((CONV_W, D_CONV), const),
            pl.BlockSpec((1, D_CONV), const),
            pl.BlockSpec((1, SSM_HEADS), const),
            pl.BlockSpec((1, D_SSM), const),
            pl.BlockSpec((1, D_SSM), const),
            pl.BlockSpec((3 * SSM_HEADS, D_SSM), const),
        ],
        out_specs=[
            pl.BlockSpec((None, CHUNK, D_SSM), lambda b, c: (b, c, 0)),
            pl.BlockSpec((None, D_STATE, D_SSM), lambda b, c: (b, 0, 0)),
            pl.BlockSpec((None, CONV_W - 1, D_CONV), lambda b, c: (b, 0, 0)),
        ],
        out_shape=[
            jax.ShapeDtypeStruct((nb, nc * CHUNK, D_SSM), BF16),
            jax.ShapeDtypeStruct((nb, D_STATE, D_SSM), F32),
            jax.ShapeDtypeStruct((nb, CONV_W - 1, D_CONV), F32),
        ],
        scratch_shapes=[
            pltpu.VMEM((D_STATE, D_SSM), F32),
            pltpu.VMEM((8 + CHUNK, D_CONV), F32),
            pltpu.VMEM((CHUNK, D_SSM), F32),
        ],
        compiler_params=_params("parallel", "arbitrary"),
        name="ssd",
    )(proj3, proj3, small3, s0, h0, cw, cb, alog, dskip, norm, expand)


def _cumsum_kernel(x_ref, o_ref):
    n = x_ref.shape[-1] // CUMSUM_BLK
    r = lax.broadcasted_iota(jnp.int32, (CUMSUM_BLK, CUMSUM_BLK), 0)
    c = lax.broadcasted_iota(jnp.int32, (CUMSUM_BLK, CUMSUM_BLK), 1)
    upper = jnp.where(r <= c, 1.0, 0.0).astype(BF16)
    carry = jnp.zeros((ATTN_HEADS, 1), F32)
    for i in range(n):
        blk = slice(i * CUMSUM_BLK, (i + 1) * CUMSUM_BLK)
        cs = sum(_dot(piece, upper) for piece in _split3(x_ref[:, blk])) + carry
        o_ref[:, blk] = cs
        carry = cs[:, CUMSUM_BLK - 1:CUMSUM_BLK]


def _cumsum_t(x_t):
    b, h, l = x_t.shape
    spec = pl.BlockSpec((None, h, l), lambda i: (i, 0, 0))
    return pl.pallas_call(
        _cumsum_kernel, grid=(b,), in_specs=[spec], out_specs=spec,
        out_shape=jax.ShapeDtypeStruct(x_t.shape, F32),
        compiler_params=_params("parallel"), name="cumsum",
    )(x_t)


def _cum_logf(lf_t):
    l = lf_t.shape[2]
    lp = -(-l // CUMSUM_BLK) * CUMSUM_BLK
    return _cumsum_t(jnp.pad(lf_t, ((0, 0), (0, 0), (0, lp - l))))[:, :, :l]


BIAS_PIECES = 3
LOG2E = 1.4426950408889634


def _bias_lane0(head):
    return ATTN_HEAD_DIM if head % 2 == 0 else 0


def _bias_placement():
    sq = np.zeros((BIAS_PIECES, ATTN_HEADS, ATTN_HEADS * LANES), np.float32)
    sk = np.zeros_like(sq)
    one_q = np.zeros((1, ATTN_HEADS * LANES), np.float32)
    one_k = np.zeros_like(one_q)
    for h in range(ATTN_HEADS):
        base = h * LANES + _bias_lane0(h)
        for piece in range(BIAS_PIECES):
            sk[piece, h, base + piece] = -1.0
            one_q[0, base + piece] = 1.0
            sq[piece, h, base + BIAS_PIECES + piece] = 1.0
            one_k[0, base + BIAS_PIECES + piece] = 1.0
    return jnp.asarray(sq, BF16), jnp.asarray(sk, BF16), jnp.asarray(one_q), jnp.asarray(one_k)


def _attn_prep_kernel(q_ref, k_ref, v_ref, c_ref, sq_ref, sk_ref, oneq_ref, onek_ref, qo_ref, ko_ref, kt_ref,
                      vt_ref):
    pieces = _split3(c_ref[...] * LOG2E)
    bias_q = sum(_dot(pc, sq_ref[n]) for n, pc in enumerate(pieces)) + oneq_ref[...]
    bias_k = sum(_dot(pc, sk_ref[n]) for n, pc in enumerate(pieces)) + onek_ref[...]
    lane = lax.broadcasted_iota(jnp.int32, (1, LANES), 1)
    lo = lane < ATTN_HEAD_DIM
    for p in range(HEAD_PAIRS):
        ps = slice(p * LANES, (p + 1) * LANES)
        q2 = q_ref[:, ps] * (ATTN_HEAD_DIM ** -0.5 * LOG2E)
        k2 = k_ref[:, ps]
        for hh in range(2):
            h = 2 * p + hh
            hs = slice(h * LANES, (h + 1) * LANES)
            sel = lo if hh == 0 else jnp.logical_not(lo)
            qo_ref[h] = (jnp.where(sel, q2, 0.0) + bias_q[:, hs]).astype(BF16)
            ko_ref[h] = (jnp.where(sel, k2, 0.0) + bias_k[:, hs]).astype(BF16)
        for src, dst in ((k2, kt_ref), (v_ref[:, ps], vt_ref)):
            x_t = jnp.transpose(src)
            dst[2 * p] = x_t[:ATTN_HEAD_DIM]
            dst[2 * p + 1] = x_t[ATTN_HEAD_DIM:]


def _attn_prep(proj3, c_rows, consts, *, nb, b_off, tile):
    l = c_rows.shape[1]
    col = lambda off: off // D_ATTN
    sq, sk, one_q, one_k = consts
    c3 = lambda b, i: (0, 0, 0)
    c2 = lambda b, i: (0, 0)
    return pl.pallas_call(
        _attn_prep_kernel,
        grid=(nb, l // tile),
        in_specs=[
            pl.BlockSpec((None, tile, D_ATTN), lambda b, i: (b + b_off, i, col(P_Q))),
            pl.BlockSpec((None, tile, D_ATTN), lambda b, i: (b + b_off, i, col(P_K))),
            pl.BlockSpec((None, tile, D_ATTN), lambda b, i: (b + b_off, i, col(P_V))),
            pl.BlockSpec((None, tile, ATTN_HEADS), lambda b, i: (b, i, 0)),
            pl.BlockSpec(sq.shape, c3), pl.BlockSpec(sk.shape, c3),
            pl.BlockSpec(one_q.shape, c2), pl.BlockSpec(one_k.shape, c2),
        ],
        out_specs=[
            pl.BlockSpec((None, ATTN_HEADS, tile, LANES), lambda b, i: (b, 0, i, 0)),
            pl.BlockSpec((None, ATTN_HEADS, tile, LANES), lambda b, i: (b, 0, i, 0)),
            pl.BlockSpec((None, ATTN_HEADS, ATTN_HEAD_DIM, tile), lambda b, i: (b, 0, 0, i)),
            pl.BlockSpec((None, ATTN_HEADS, ATTN_HEAD_DIM, tile), lambda b, i: (b, 0, 0, i)),
        ],
        out_shape=[
            jax.ShapeDtypeStruct((nb, ATTN_HEADS, l, LANES), BF16),
            jax.ShapeDtypeStruct((nb, ATTN_HEADS, l, LANES), BF16),
            jax.ShapeDtypeStruct((nb, ATTN_HEADS, ATTN_HEAD_DIM, l), F32),
            jax.ShapeDtypeStruct((nb, ATTN_HEADS, ATTN_HEAD_DIM, l), F32),
        ],
        compiler_params=_params("parallel", "parallel"),
        name="attn_prep",
    )(proj3, proj3, proj3, c_rows, sq, sk, one_q, one_k)


QBLK = 256
ACC_ROWS = ATTN_HEAD_DIM + 16


def _attn_prompt_kernel(it_ref, jt_ref, q_ref, k_ref, vt_ref, km_ref, vtm_ref, o_ref, m_sc, acc_sc,
                        *, meta_lo, meta_hi):
    t = pl.program_id(1)
    i = it_ref[t]
    j = jt_ref[t]
    nqb = ATT_T // QBLK

    def q_block(h, r):
        return q_ref[h, r * QBLK:(r + 1) * QBLK, :]

    def v_rows(v_t):
        return jnp.concatenate([v_t.astype(BF16), jnp.ones((ACC_ROWS - ATTN_HEAD_DIM, v_t.shape[1]), BF16)], axis=0)

    def head_update(h, scores, first):
        if not first:
            m_old_all, acc_old_all = m_sc[h], acc_sc[h]
        m_out, acc_out = [], []
        for r, (s_t, v_aug) in enumerate(scores):
            qs = slice(r * QBLK, (r + 1) * QBLK)
            mx = jnp.max(s_t, axis=0, keepdims=True)
            if first:
                m_new = mx
                acc_new = _dot(v_aug, jnp.exp2(s_t - m_new).astype(BF16))
            else:
                m_old = m_old_all[:, qs]
                m_new = jnp.maximum(m_old, mx)
                acc_new = (acc_old_all[:, qs] * jnp.exp2(m_old - m_new)
                           + _dot(v_aug, jnp.exp2(s_t - m_new).astype(BF16)))
            m_out.append(m_new)
            acc_out.append(acc_new)
        m_sc[h] = jnp.concatenate(m_out, axis=1)
        acc_sc[h] = jnp.concatenate(acc_out, axis=1)

    def run(score_fn, first):
        nxt = [score_fn(0, r) for r in range(nqb)]
        for h in range(ATTN_HEADS):
            cur = nxt
            if h + 1 < ATTN_HEADS:
                nxt = [score_fn(h + 1, r) for r in range(nqb)]
            head_update(h, cur, first)

    @pl.when(j == 0)
    def _():
        krow = lax.broadcasted_iota(jnp.int32, (km_ref.shape[1], 1), 0)
        is_meta = jnp.logical_and(krow >= meta_lo, krow < meta_hi)
        run(lambda h, r: (jnp.where(is_meta, _dot_nt(km_ref[h], q_block(h, r)), -jnp.inf),
                          v_rows(vtm_ref[h])), True)

    @pl.when(j < i)
    def _():
        run(lambda h, r: (_dot_nt(k_ref[h], q_block(h, r)), v_rows(vt_ref[h])), False)

    @pl.when(j == i)
    def _():
        def diag_scores(h, r):
            nk = (r + 1) * QBLK
            s_t = _dot_nt(k_ref[h, :nk, :], q_block(h, r))
            krow = lax.broadcasted_iota(jnp.int32, (nk, QBLK), 0)
            qcol = lax.broadcasted_iota(jnp.int32, (nk, QBLK), 1) + r * QBLK
            return jnp.where(krow <= qcol, s_t, -jnp.inf), v_rows(vt_ref[h, :, :nk])

        run(diag_scores, False)
        d = ATTN_HEAD_DIM
        for p in range(HEAD_PAIRS):
            o_t = jnp.concatenate([acc_sc[h, :d, :] / acc_sc[h, d:d + 1, :] for h in (2 * p, 2 * p + 1)], axis=0)
            o_ref[:, p * LANES:(p + 1) * LANES] = jnp.transpose(o_t).astype(o_ref.dtype)


def _attn_prompt(q_aug, k_aug, v_t, km_aug, vm_t, *, meta_lo):
    nb, _, l, _ = q_aug.shape
    nt = l // ATT_T
    it = np.array([i for i in range(nt) for _ in range(i + 1)], np.int32)
    jt = np.array([j for i in range(nt) for j in range(i + 1)], np.int32)
    mrows = km_aug.shape[2]
    grid_spec = pltpu.PrefetchScalarGridSpec(
        num_scalar_prefetch=2,
        grid=(nb, len(it)),
        in_specs=[
            pl.BlockSpec((None, ATTN_HEADS, ATT_T, LANES), lambda b, t, it, jt: (b, 0, it[t], 0)),
            pl.BlockSpec((None, ATTN_HEADS, ATT_T, LANES), lambda b, t, it, jt: (b, 0, jt[t], 0)),
            pl.BlockSpec((None, ATTN_HEADS, ATTN_HEAD_DIM, ATT_T), lambda b, t, it, jt: (b, 0, 0, jt[t])),
            pl.BlockSpec((None, ATTN_HEADS, mrows, LANES), lambda b, t, it, jt: (0, 0, 0, 0)),
            pl.BlockSpec((None, ATTN_HEADS, ATTN_HEAD_DIM, mrows), lambda b, t, it, jt: (0, 0, 0, 0)),
        ],
        out_specs=pl.BlockSpec((None, ATT_T, D_ATTN), lambda b, t, it, jt: (b, it[t], 0)),
        scratch_shapes=[
            pltpu.VMEM((ATTN_HEADS, 1, ATT_T), F32),
            pltpu.VMEM((ATTN_HEADS, ACC_ROWS, ATT_T), F32),
        ],
    )
    return pl.pallas_call(
        functools.partial(_attn_prompt_kernel, meta_lo=meta_lo, meta_hi=meta_lo + N_META),
        grid_spec=grid_spec,
        out_shape=jax.ShapeDtypeStruct((nb, l, D_ATTN), BF16),
        compiler_params=_params("parallel", "arbitrary"),
        name="attn_prompt",
    )(jnp.asarray(it), jnp.asarray(jt), q_aug, k_aug, v_t, km_aug, vm_t)


def _attn_sample_kernel(q_ref, kn_ref, vn_ref, km_ref, vm_ref, kct_ref, vct_ref, cq_ref, ckm_ref, ckc_ref,
                        ckn_ref, o_ref):
    h = pl.program_id(1)
    T = CHUNK
    D = ATTN_HEAD_DIM
    first = h % 2 == 0

    def half(ref):
        x = ref[...]
        return jnp.where(first, x[:, :D], x[:, D:])

    qh = (half(q_ref) * (D ** -0.5)).astype(BF16)
    cq = cq_ref[...]
    row = lax.broadcasted_iota(jnp.int32, (T, T), 0)
    col = lax.broadcasted_iota(jnp.int32, (T, T), 1)
    s_m = _dot_nt(qh, half(km_ref).astype(BF16)) + cq - ckm_ref[...]
    s_c = _dot(qh, kct_ref[...].astype(BF16)) + cq - ckc_ref[...]
    s_n = jnp.where(col <= row, _dot_nt(qh, half(kn_ref).astype(BF16)) + cq - ckn_ref[...], -jnp.inf)
    m = jnp.maximum(jnp.maximum(jnp.max(s_m, axis=1, keepdims=True), jnp.max(s_c, axis=1, keepdims=True)),
                    jnp.max(s_n, axis=1, keepdims=True))
    p_m = jnp.exp(s_m - m)
    p_c = jnp.exp(s_c - m)
    p_n = jnp.exp(s_n - m)
    l = (jnp.sum(p_m, axis=1, keepdims=True) + jnp.sum(p_c, axis=1, keepdims=True)
         + jnp.sum(p_n, axis=1, keepdims=True))
    o = (_dot(p_m.astype(BF16), half(vm_ref).astype(BF16)) + _dot_nt(p_c.astype(BF16), vct_ref[...].astype(BF16))
         + _dot(p_n.astype(BF16), half(vn_ref).astype(BF16)))
    o_ref[...] = (o / l).astype(o_ref.dtype)


def _attn_sample(proj3, proj2, meta_blk, cache_k, cache_v, cq, ckm_t, ckc_t, ckn_t, *, nb):
    past = cache_k.shape[3]
    lb = lambda off: off // LANES
    pair = lambda off: (lambda b, h: (b, 0, lb(off) + h // 2))
    ckspec = lambda n: pl.BlockSpec((None, 1, n), lambda b, h: (b * ATTN_HEADS + h, 0, 0))
    return pl.pallas_call(
        _attn_sample_kernel,
        grid=(nb, ATTN_HEADS),
        in_specs=[
            pl.BlockSpec((None, CHUNK, LANES), pair(P_Q)),
            pl.BlockSpec((None, CHUNK, LANES), pair(P_K)),
            pl.BlockSpec((None, CHUNK, LANES), pair(P_V)),
            pl.BlockSpec((N_META, LANES), lambda b, h: (meta_blk, lb(P_K) + h // 2)),
            pl.BlockSpec((N_META, LANES), lambda b, h: (meta_blk, lb(P_V) + h // 2)),
            pl.BlockSpec((None, None, ATTN_HEAD_DIM, past), lambda b, h: (b, h, 0, 0)),
            pl.BlockSpec((None, None, ATTN_HEAD_DIM, past), lambda b, h: (b, h, 0, 0)),
            pl.BlockSpec((None, None, CHUNK, 1), lambda b, h: (b, h, 0, 0)),
            ckspec(N_META), ckspec(past), ckspec(CHUNK),
        ],
        out_specs=pl.BlockSpec((None, None, CHUNK, ATTN_HEAD_DIM), lambda b, h: (b, h, 0, 0)),
        out_shape=jax.ShapeDtypeStruct((nb, ATTN_HEADS, CHUNK, ATTN_HEAD_DIM), BF16),
        compiler_params=_params("parallel", "parallel"),
        name="attn_sample",
    )(proj3, proj3, proj3, proj2, proj2, cache_k, cache_v, cq, ckm_t, ckc_t, ckn_t)


def _merge_kernel(os_ref, oa_ref, gs_ref, ga_ref, h_ref, wbs_ref, wba_ref, wo_ref, o_ref):
    bs = _dot(os_ref[...], wbs_ref[...])
    ba = _dot(oa_ref[...], wba_ref[...])
    merged = jax.nn.sigmoid(gs_ref[...]) * bs + jax.nn.sigmoid(ga_ref[...]) * ba
    o_ref[...] = h_ref[...] + _dot(merged.astype(BF16), wo_ref[...])


def _merge(o_ssm, o_attn, proj, h, w_bs, w_ba, w_o, *, tm):
    rows = o_ssm.shape[0]
    const = lambda i: (0, 0)
    return pl.pallas_call(
        _merge_kernel,
        grid=(rows // tm,),
        in_specs=[
            pl.BlockSpec((tm, D_SSM), lambda i: (i, 0)),
            pl.BlockSpec((tm, D_ATTN), lambda i: (i, 0)),
            pl.BlockSpec((tm, D_MODEL), lambda i: (i, P_GS // D_MODEL)),
            pl.BlockSpec((tm, D_MODEL), lambda i: (i, P_GA // D_MODEL)),
            pl.BlockSpec((tm, D_MODEL), lambda i: (i, 0)),
            pl.BlockSpec((D_SSM, D_MODEL), const),
            pl.BlockSpec((D_ATTN, D_MODEL), const),
            pl.BlockSpec((D_MODEL, D_MODEL), const),
        ],
        out_specs=pl.BlockSpec((tm, D_MODEL), lambda i: (i, 0)),
        out_shape=jax.ShapeDtypeStruct((rows, D_MODEL), F32),
        compiler_params=_params("parallel"),
        name="merge",
    )(o_ssm, o_attn, proj, proj, h, w_bs, w_ba, w_o)


def kernel(x_prompt, x_sample, cache_k, cache_v, cache_logf, state_ssm, state_conv, meta_tokens, norm_ffn1, ffn1_w_in, ffn1_w_out, norm_mix, w_in, conv_w, conv_b, dt_bias, a_log, d_skip, f_bias, ssm_norm, w_br_ssm, w_br_attn, w_out, norm_ffn2, ffn2_w_in, ffn2_w_out, norm_final):
    assert norm_ffn1.shape[0] == 1, "single-layer trunk"
    bp, seq, _ = x_prompt.shape
    bs, dseq, _ = x_sample.shape
    assert dseq == CHUNK and seq % ATT_T == 0
    n_s = bs * dseq
    rows_small = n_s + 2 * CHUNK
    meta_row0 = n_s + CHUNK - N_META
    meta_blk = meta_row0 // N_META

    row = lambda v: v.reshape(1, -1).astype(F32)
    w = w_in[0]
    w_main = jnp.concatenate([w[:, OFF_XBC:OFF_DT], w[:, :OFF_XBC], w[:, OFF_Q:OFF_F], w[:, OFF_GS:]],
                             axis=1).astype(BF16)
    pad_small = LANES - SSM_HEADS - ATTN_HEADS
    w_small = jnp.concatenate([w[:, OFF_DT:OFF_Q], w[:, OFF_F:OFF_GS], jnp.zeros((D_MODEL, pad_small), F32)],
                              axis=1).astype(BF16)
    b_small = jnp.concatenate([dt_bias[0], f_bias[0], jnp.zeros((pad_small,), F32)]).reshape(1, LANES)
    w1i, w1o = ffn1_w_in[0].astype(BF16), ffn1_w_out[0].astype(BF16)
    w2i, w2o = ffn2_w_in[0].astype(BF16), ffn2_w_out[0].astype(BF16)
    w_bs, w_ba, w_o = w_br_ssm[0].astype(BF16), w_br_attn[0].astype(BF16), w_out[0].astype(BF16)
    expand = jnp.asarray(np.tile(np.repeat(np.eye(SSM_HEADS, dtype=np.float32), SSM_HEAD_DIM, axis=1), (3, 1)),
                         BF16)
    ssd_consts = (conv_w[0], row(conv_b[0]), row(a_log[0]), row(jnp.repeat(d_skip[0], SSM_HEAD_DIM)),
                  row(ssm_norm[0]), expand)

    def front(x, tm_ffn, tm_proj):
        h1, u = _ffn(x, row(norm_ffn1[0]), w1i, w1o, row(norm_mix[0]), tm=tm_ffn, emit_h=True, u_dtype=BF16)
        return h1, _proj(u, w_main, tm=tm_proj), _small(u, w_small, b_small, tm=tm_proj)

    xp = x_prompt.reshape(bp * seq, D_MODEL)
    h1_p, pm_p, ps_p = front(xp, 512, 1024)
    x_small = jnp.concatenate([x_sample.reshape(n_s, D_MODEL), jnp.zeros((CHUNK - N_META, D_MODEL), F32),
                               meta_tokens, jnp.zeros((CHUNK, D_MODEL), F32)], axis=0)
    h1_s, pm_s, ps_s = front(x_small, rows_small // 2, rows_small // 2)

    pm_p3 = pm_p.reshape(bp, seq, P_TOTAL)
    ps_p3 = ps_p.reshape(bp, seq, LANES)
    pm_s3 = pm_s.reshape(rows_small // CHUNK, CHUNK, P_TOTAL)
    ps_s3 = ps_s.reshape(rows_small // CHUNK, CHUNK, LANES)

    zero_state = jnp.zeros((1, D_STATE, D_SSM), F32)
    zero_hist = jnp.zeros((1, CONV_W - 1, D_CONV), F32)
    _, st_m, hist_m = _ssd(pm_s3, ps_s3, zero_state, zero_hist, *ssd_consts,
                           nb=1, b_off=bs, shared_init=True, n_pad=CHUNK - N_META)
    o_ssm_p, st_p, hist_p = _ssd(pm_p3, ps_p3, st_m, hist_m, *ssd_consts,
                                 nb=bp, b_off=0, shared_init=True, n_pad=0)
    st_s0 = jnp.transpose(state_ssm[0].reshape(bs, D_SSM, D_STATE), (0, 2, 1))
    o_ssm_s, st_s, hist_s = _ssd(pm_s3, ps_s3, st_s0, state_conv[0], *ssd_consts,
                                 nb=bs, b_off=0, shared_init=False, n_pad=0)

    t_last = lambda a: jnp.swapaxes(a, -1, -2)
    lf_m = t_last(ps_s[meta_row0:meta_row0 + N_META, S_F:S_F + ATTN_HEADS])
    lf_p = t_last(ps_p3[:, :, S_F:S_F + ATTN_HEADS])
    lf_s = ps_s[:n_s, S_F:S_F + ATTN_HEADS].reshape(bs, dseq, ATTN_HEADS)
    lf_all_p = jnp.concatenate([jnp.broadcast_to(lf_m[None], (bp, ATTN_HEADS, N_META)), lf_p], axis=2)
    lf_all_s = jnp.concatenate([jnp.broadcast_to(lf_m[None], (bs, ATTN_HEADS, N_META)),
                                t_last(cache_logf[0].astype(F32)), t_last(lf_s)], axis=2)
    past = cache_logf.shape[2]
    c_p = _cum_logf(lf_all_p)
    c_s = _cum_logf(lf_all_s)

    prep_consts = _bias_placement()
    q_aug, k_aug, k_t, v_t = _attn_prep(pm_p3, t_last(c_p[:, :, N_META:]), prep_consts,
                                        nb=bp, b_off=0, tile=ATT_T)
    meta_tile = 2 * CHUNK
    meta_lo = meta_row0 - n_s
    c_meta = jnp.pad(t_last(c_p[:1, :, :N_META]), ((0, 0), (meta_lo, meta_tile - meta_lo - N_META), (0, 0)))
    _, km_aug, km_t, vm_t = _attn_prep(pm_s.reshape(rows_small // meta_tile, meta_tile, P_TOTAL), c_meta,
                                       prep_consts, nb=1, b_off=n_s // meta_tile, tile=meta_tile)
    o_attn_p = _attn_prompt(q_aug, k_aug, v_t, km_aug, vm_t, meta_lo=meta_lo)
    per_head = lambda c: c.reshape(bs * ATTN_HEADS, 1, c.shape[2])
    feat_major = lambda a: jnp.transpose(a, (0, 2, 3, 1))
    o_attn_s = _attn_sample(pm_s3, pm_s, meta_blk, feat_major(cache_k[0]), feat_major(cache_v[0]),
                            c_s[:, :, N_META + past:, None],
                            per_head(c_s[:, :, :N_META]), per_head(c_s[:, :, N_META:N_META + past]),
                            per_head(c_s[:, :, N_META + past:]), nb=bs)
    o_attn_s = jnp.transpose(o_attn_s, (0, 2, 1, 3))

    def back(o_ssm, o_attn, pm, h1, tm):
        h2 = _merge(o_ssm, o_attn, pm, h1, w_bs, w_ba, w_o, tm=tm)
        return _ffn(h2, row(norm_ffn2[0]), w2i, w2o, row(norm_final), tm=tm, emit_h=False, u_dtype=F32)[0]

    y_prompt = back(o_ssm_p.reshape(bp * seq, D_SSM), o_attn_p.reshape(bp * seq, D_ATTN), pm_p, h1_p, 512)
    y_sample = back(o_ssm_s.reshape(n_s, D_SSM), o_attn_s.reshape(n_s, D_ATTN), pm_s, h1_s, 512)

    heads = lambda a: a.reshape(1, a.shape[0], a.shape[1], ATTN_HEADS, ATTN_HEAD_DIM)

    def with_meta(x_t, m_t):
        m_t = jnp.broadcast_to(m_t[:, :, :, meta_lo:meta_lo + N_META], (bp, ATTN_HEADS, ATTN_HEAD_DIM, N_META))
        return jnp.transpose(jnp.concatenate([m_t, x_t], axis=3), (0, 3, 1, 2))[None]

    k_prompt = with_meta(k_t, km_t)
    v_prompt = with_meta(v_t, vm_t)
    logf_prompt = t_last(lf_all_p)
    k_sample = heads(pm_s[:n_s, P_K:P_K + D_ATTN].reshape(bs, dseq, D_ATTN))
    v_sample = heads(pm_s[:n_s, P_V:P_V + D_ATTN].reshape(bs, dseq, D_ATTN))
    state_out = lambda st: jnp.transpose(st, (0, 2, 1)).reshape(1, st.shape[0], SSM_HEADS, SSM_HEAD_DIM, D_STATE)
    return (y_prompt.reshape(bp, seq, D_MODEL), y_sample.reshape(bs, dseq, D_MODEL),
            k_prompt, v_prompt, logf_prompt[None], state_out(st_p), hist_p[None],
            k_sample, v_sample, lf_s[None], state_out(st_s), hist_s[None])
```

```python
import functools

import numpy as np
import jax
import jax.numpy as jnp
from jax import lax
from jax.experimental import pallas as pl
from jax.experimental.pallas import tpu as pltpu

F32 = jnp.float32
BF16 = jnp.bfloat16

D_MODEL = 1024
N_META = 16
CHUNK = 64
EPS = 1e-6
D_SSM = 2048
SSM_HEADS = 32
SSM_HEAD_DIM = 64
SSM_GROUPS = 8
D_STATE = 128
CONV_W = 4
D_CONV = D_SSM + 2 * SSM_GROUPS * D_STATE
ATTN_HEADS = 16
ATTN_HEAD_DIM = 64
D_ATTN = ATTN_HEADS * ATTN_HEAD_DIM
D_FF = 2816
HEAD_PAIRS = ATTN_HEADS // 2
LANES = 128
GROUP_W = D_SSM // SSM_GROUPS

OFF_XBC = D_SSM
OFF_DT = OFF_XBC + D_CONV
OFF_Q = OFF_DT + SSM_HEADS
OFF_K = OFF_Q + D_ATTN
OFF_V = OFF_K + D_ATTN
OFF_F = OFF_V + D_ATTN
OFF_GS = OFF_F + ATTN_HEADS
OFF_GA = OFF_GS + D_MODEL

P_XBC = 0
P_Z = D_CONV
P_Q = P_Z + D_SSM
P_K = P_Q + D_ATTN
P_V = P_K + D_ATTN
P_GS = P_V + D_ATTN
P_GA = P_GS + D_MODEL
P_TOTAL = P_GA + D_MODEL
S_DT = 0
S_F = SSM_HEADS

VMEM_LIMIT = 56 * 1024 * 1024

PROJ_TN = P_TOTAL // 4
ATT_T = 512
CUMSUM_BLK = 256


def _dot(a, b):
    return jnp.dot(a, b, preferred_element_type=F32)


def _dot_nt(a, b):
    return lax.dot_general(a, b, (((1,), (1,)), ((), ())), preferred_element_type=F32)


def _dot_tn(a, b):
    return lax.dot_general(a, b, (((0,), (0,)), ((), ())), preferred_element_type=F32)


def _split3(x):
    hi = x.astype(BF16)
    r = x - hi.astype(F32)
    mid = r.astype(BF16)
    lo = (r - mid.astype(F32)).astype(BF16)
    return hi, mid, lo


def _rmsnorm(x, g):
    return x * lax.rsqrt(jnp.mean(x * x, axis=-1, keepdims=True) + EPS) * g


def _softplus(x):
    return jnp.maximum(x, 0.0) + jnp.log1p(jnp.exp(-jnp.abs(x)))


def _params(*sem):
    return pltpu.CompilerParams(dimension_semantics=sem, vmem_limit_bytes=VMEM_LIMIT)


FFN_HALVES = 2


def _ffn_kernel(x_ref, g1_ref, wi_ref, wo_ref, g2_ref, *out_refs, emit_h):
    h_ref, u_ref = out_refs if emit_h else (None,) + out_refs
    rows = x_ref.shape[0] // FFN_HALVES
    for s in range(FFN_HALVES):
        rs = slice(s * rows, (s + 1) * rows)
        x = x_ref[rs, :]
        xn = _rmsnorm(x, g1_ref[...]).astype(BF16)
        a = _dot(xn, wi_ref[:, :D_FF])
        b = _dot(xn, wi_ref[:, D_FF:])
        g = (a * jax.nn.sigmoid(a) * b).astype(BF16)
        h = x + 0.5 * _dot(g, wo_ref[...])
        if emit_h:
            h_ref[rs, :] = h
        u_ref[rs, :] = _rmsnorm(h, g2_ref[...]).astype(u_ref.dtype)


def _ffn(x, g1, w_in, w_out, g2, *, tm, emit_h, u_dtype):
    rows = x.shape[0]
    row_spec = pl.BlockSpec((tm, D_MODEL), lambda i: (i, 0))
    vec_spec = pl.BlockSpec((1, D_MODEL), lambda i: (0, 0))
    resident = lambda shape: pl.BlockSpec(shape, lambda i: (0, 0), pipeline_mode=pl.Buffered(1))
    out_shape = [jax.ShapeDtypeStruct((rows, D_MODEL), u_dtype)]
    out_specs = [row_spec]
    if emit_h:
        out_shape = [jax.ShapeDtypeStruct((rows, D_MODEL), F32)] + out_shape
        out_specs = [row_spec] + out_specs
    return pl.pallas_call(
        functools.partial(_ffn_kernel, emit_h=emit_h),
        grid=(rows // tm,),
        in_specs=[row_spec, vec_spec, resident((D_MODEL, 2 * D_FF)), resident((D_FF, D_MODEL)), vec_spec],
        out_specs=out_specs,
        out_shape=out_shape,
        compiler_params=_params("parallel"),
        name="ffn",
    )(x, g1, w_in, w_out, g2)


def _proj_kernel(u_ref, w_ref, o_ref):
    o_ref[...] = _dot(u_ref[...], w_ref[...])


def _proj(u, w, *, tm):
    rows = u.shape[0]
    return pl.pallas_call(
        _proj_kernel,
        grid=(P_TOTAL // PROJ_TN, rows // tm),
        in_specs=[pl.BlockSpec((tm, D_MODEL), lambda j, i: (i, 0)),
                  pl.BlockSpec((D_MODEL, PROJ_TN), lambda j, i: (0, j))],
        out_specs=pl.BlockSpec((tm, PROJ_TN), lambda j, i: (i, j)),
        out_shape=jax.ShapeDtypeStruct((rows, P_TOTAL), F32),
        compiler_params=_params("parallel", "parallel"),
        name="proj",
    )(u, w)


def _small_kernel(u_ref, w_ref, b_ref, o_ref):
    x = _dot(u_ref[...], w_ref[...]) + b_ref[...]
    lane = lax.broadcasted_iota(jnp.int32, (1, LANES), 1)
    sgn = jnp.where(lane < S_F, 1.0, -1.0)
    o_ref[...] = sgn * _softplus(sgn * x)


def _small(u, w, b, *, tm):
    rows = u.shape[0]
    return pl.pallas_call(
        _small_kernel,
        grid=(rows // tm,),
        in_specs=[pl.BlockSpec((tm, D_MODEL), lambda i: (i, 0)),
                  pl.BlockSpec((D_MODEL, LANES), lambda i: (0, 0)),
                  pl.BlockSpec((1, LANES), lambda i: (0, 0))],
        out_specs=pl.BlockSpec((tm, LANES), lambda i: (i, 0)),
        out_shape=jax.ShapeDtypeStruct((rows, LANES), F32),
        compiler_params=_params("parallel"),
        name="small",
    )(u, w, b)


HIST0 = 8 - (CONV_W - 1)


def _ssd_kernel(xbc_ref, z_ref, dt_ref, s0_ref, h0_ref, cw_ref, cb_ref, alog_ref, dskip_ref, norm_ref,
                e_ref, o_ref, sT_ref, hT_ref, st_sc, xp_sc, y_sc, *, n_pad):
    T = CHUNK
    c = pl.program_id(1)

    @pl.when(c == 0)
    def _():
        st_sc[...] = s0_ref[...]
        xp_sc[HIST0:8, :] = h0_ref[...]

    xraw = xbc_ref[...]
    xp_sc[8:8 + T, :] = xraw
    cw = cw_ref[...]
    xp = xp_sc[...]
    conv = cb_ref[...]
    for j in range(CONV_W - 1):
        conv = conv + cw[j:j + 1] * pltpu.roll(xp, CONV_W - 1 - j, axis=0)[8:8 + T, :]
    conv = conv + cw[3:4] * xraw
    xp_sc[HIST0:8, :] = xraw[T - (CONV_W - 1):T, :]
    xc = conv * jax.nn.sigmoid(conv)
    dtv = dt_ref[:, S_DT:S_DT + SSM_HEADS]
    if n_pad:
        valid = lax.broadcasted_iota(jnp.int32, (T, 1), 0) >= n_pad
        xc = jnp.where(valid, xc, 0.0)
        dtv = jnp.where(valid, dtv, 0.0)

    a = -jnp.exp(alog_ref[...])
    adt = a * dtv
    row = lax.broadcasted_iota(jnp.int32, (T, T), 0)
    col = lax.broadcasted_iota(jnp.int32, (T, T), 1)
    causal = col <= row
    tril = jnp.where(causal, 1.0, 0.0).astype(BF16)
    r32 = lax.broadcasted_iota(jnp.int32, (SSM_HEADS, SSM_HEADS), 0)
    c32 = lax.broadcasted_iota(jnp.int32, (SSM_HEADS, SSM_HEADS), 1)
    eye = jnp.where(r32 == c32, 1.0, 0.0).astype(BF16)
    expand3 = e_ref[...]

    def expand(v):
        return _dot(jnp.concatenate(_split3(v), axis=1), expand3)

    a_cs = sum(_dot(tril, piece) for piece in _split3(adt))
    a_cs_t = sum(_dot_nt(eye, piece) for piece in _split3(a_cs))
    a_last = a_cs[T - 1:T, :]
    dt_e = expand(dtv)
    wdec_e = expand(jnp.exp(a_last - a_cs) * dtv)
    eacs = expand(jnp.exp(a_cs))
    cdec = eacs[T - 1:T, :]

    xs = xc[:, :D_SSM]
    bm = xc[:, D_SSM:D_SSM + SSM_GROUPS * D_STATE].astype(BF16)
    cm = xc[:, D_SSM + SSM_GROUPS * D_STATE:].astype(BF16)
    xd = xs * dt_e
    xdw = (xs * wdec_e).astype(BF16)
    lane = lax.broadcasted_iota(jnp.int32, (1, D_SSM), 1)
    lo = (lane & SSM_HEAD_DIM) == 0
    xd_lo = jnp.where(lo, xd, 0.0).astype(BF16)
    xd_hi = jnp.where(lo, 0.0, xd).astype(BF16)
    lo2 = lax.broadcasted_iota(jnp.int32, (1, LANES), 1) < SSM_HEAD_DIM
    step2 = lax.broadcasted_iota(jnp.int32, (T, LANES), 1) & (SSM_HEAD_DIM - 1)
    causal2 = step2 <= lax.broadcasted_iota(jnp.int32, (T, LANES), 0)

    for g in range(SSM_GROUPS):
        gs = slice(g * GROUP_W, (g + 1) * GROUP_W)
        cmg = cm[:, g * D_STATE:(g + 1) * D_STATE]
        bmg = bm[:, g * D_STATE:(g + 1) * D_STATE]
        cb2 = _dot_nt(cmg, jnp.concatenate([bmg, bmg], axis=0))
        stg = st_sc[:, gs]
        yoff = _dot(cmg, stg.astype(BF16))
        for rr in range(2):
            pair = g * 2 + rr
            ps = slice(pair * LANES, (pair + 1) * LANES)
            ha, hb = 2 * pair, 2 * pair + 1
            seg = (jnp.where(lo2, a_cs[:, ha:ha + 1], a_cs[:, hb:hb + 1])
                   - jnp.concatenate([a_cs_t[ha:ha + 1, :], a_cs_t[hb:hb + 1, :]], axis=1))
            m2 = (cb2 * jnp.exp(jnp.where(causal2, seg, -jnp.inf))).astype(BF16)
            ydiag = _dot(m2, jnp.concatenate([xd_lo[:, ps], xd_hi[:, ps]], axis=0))
            y_sc[:, ps] = ydiag + yoff[:, rr * LANES:(rr + 1) * LANES] * eacs[:, ps]
        st_sc[:, gs] = stg * cdec[:, gs] + _dot_tn(bmg, xdw[:, gs])

    y = y_sc[...] + dskip_ref[...] * xs
    zz = z_ref[...]
    y = y * (zz * jax.nn.sigmoid(zz))
    for g in range(SSM_GROUPS):
        gs = slice(g * GROUP_W, (g + 1) * GROUP_W)
        yg = y[:, gs]
        yg = yg * lax.rsqrt(jnp.mean(yg * yg, axis=-1, keepdims=True) + EPS)
        o_ref[:, gs] = (yg * norm_ref[:, gs]).astype(o_ref.dtype)

    @pl.when(c == pl.num_programs(1) - 1)
    def _():
        sT_ref[...] = st_sc[...]
        hT_ref[...] = xp_sc[HIST0:8, :]


def _ssd(proj3, small3, s0, h0, cw, cb, alog, dskip, norm, expand, *, nb, b_off, shared_init, n_pad):
    nc = proj3.shape[1] // CHUNK
    init = (lambda b, c: (0, 0, 0)) if shared_init else (lambda b, c: (b, 0, 0))
    const = lambda b, c: (0, 0)
    return pl.pallas_call(
        functools.partial(_ssd_kernel, n_pad=n_pad),
        grid=(nb, nc),
        in_specs=[
            pl.BlockSpec((None, CHUNK, D_CONV), lambda b, c: (b + b_off, c, P_XBC // D_CONV)),
            pl.BlockSpec((None, CHUNK, D_SSM), lambda b, c: (b + b_off, c, P_Z // D_SSM)),
            pl.BlockSpec((None, CHUNK, LANES), lambda b, c: (b + b_off, c, 0)),
            pl.BlockSpec((None, D_STATE, D_SSM), init),
            pl.BlockSpec((None, CONV_W - 1, D_CONV), init),
            pl.BlockSpec((CONV_W, D_CONV), const),
            pl.BlockSpec((1, D_CONV), const),
            pl.BlockSpec((1, SSM_HEADS), const),
            pl.BlockSpec((1, D_SSM), const),
            pl.BlockSpec((1, D_SSM), const),
            pl.BlockSpec((3 * SSM_HEADS, D_SSM), const),
        ],
        out_specs=[
            pl.BlockSpec((None, CHUNK, D_SSM), lambda b, c: (b, c, 0)),
            pl.BlockSpec((None, D_STATE, D_SSM), lambda b, c: (b, 0, 0)),
            pl.BlockSpec((None, CONV_W - 1, D_CONV), lambda b, c: (b, 0, 0)),
        ],
        out_shape=[
            jax.ShapeDtypeStruct((nb, nc * CHUNK, D_SSM), BF16),
            jax.ShapeDtypeStruct((nb, D_STATE, D_SSM), F32),
            jax.ShapeDtypeStruct((nb, CONV_W - 1, D_CONV), F32),
        ],
        scratch_shapes=[
            pltpu.VMEM((D_STATE, D_SSM), F32),
            pltpu.VMEM((8 + CHUNK, D_CONV), F32),
            pltpu.VMEM((CHUNK, D_SSM), F32),
        ],
        compiler_params=_params("parallel", "arbitrary"),
        name="ssd",
    )(proj3, proj3, small3, s0, h0, cw, cb, alog, dskip, norm, expand)


def _cumsum_kernel(x_ref, o_ref):
    n = x_ref.shape[-1] // CUMSUM_BLK
    r = lax.broadcasted_iota(jnp.int32, (CUMSUM_BLK, CUMSUM_BLK), 0)
    c = lax.broadcasted_iota(jnp.int32, (CUMSUM_BLK, CUMSUM_BLK), 1)
    upper = jnp.where(r <= c, 1.0, 0.0).astype(BF16)
    carry = jnp.zeros((ATTN_HEADS, 1), F32)
    for i in range(n):
        blk = slice(i * CUMSUM_BLK, (i + 1) * CUMSUM_BLK)
        cs = sum(_dot(piece, upper) for piece in _split3(x_ref[:, blk])) + carry
        o_ref[:, blk] = cs
        carry = cs[:, CUMSUM_BLK - 1:CUMSUM_BLK]


def _cumsum_t(x_t):
    b, h, l = x_t.shape
    spec = pl.BlockSpec((None, h, l), lambda i: (i, 0, 0))
    return pl.pallas_call(
        _cumsum_kernel, grid=(b,), in_specs=[spec], out_specs=spec,
        out_shape=jax.ShapeDtypeStruct(x_t.shape, F32),
        compiler_params=_params("parallel"), name="cumsum",
    )(x_t)


def _cum_logf(lf_t):
    l = lf_t.shape[2]
    lp = -(-l // CUMSUM_BLK) * CUMSUM_BLK
    return _cumsum_t(jnp.pad(lf_t, ((0, 0), (0, 0), (0, lp - l))))[:, :, :l]


BIAS_PIECES = 3
LOG2E = 1.4426950408889634


def _bias_lane0(head):
    return ATTN_HEAD_DIM if head % 2 == 0 else 0


def _bias_placement():
    sq = np.zeros((BIAS_PIECES, ATTN_HEADS, ATTN_HEADS * LANES), np.float32)
    sk = np.zeros_like(sq)
    one_q = np.zeros((1, ATTN_HEADS * LANES), np.float32)
    one_k = np.zeros_like(one_q)
    for h in range(ATTN_HEADS):
        base = h * LANES + _bias_lane0(h)
        for piece in range(BIAS_PIECES):
            sk[piece, h, base + piece] = -1.0
            one_q[0, base + piece] = 1.0
            sq[piece, h, base + BIAS_PIECES + piece] = 1.0
            one_k[0, base + BIAS_PIECES + piece] = 1.0
    return jnp.asarray(sq, BF16), jnp.asarray(sk, BF16), jnp.asarray(one_q), jnp.asarray(one_k)


def _attn_prep_kernel(q_ref, k_ref, v_ref, c_ref, sq_ref, sk_ref, oneq_ref, onek_ref, qo_ref, ko_ref, kt_ref,
                      vt_ref):
    pieces = _split3(c_ref[...] * LOG2E)
    bias_q = sum(_dot(pc, sq_ref[n]) for n, pc in enumerate(pieces)) + oneq_ref[...]
    bias_k = sum(_dot(pc, sk_ref[n]) for n, pc in enumerate(pieces)) + onek_ref[...]
    lane = lax.broadcasted_iota(jnp.int32, (1, LANES), 1)
    lo = lane < ATTN_HEAD_DIM
    for p in range(HEAD_PAIRS):
        ps = slice(p * LANES, (p + 1) * LANES)
        q2 = q_ref[:, ps] * (ATTN_HEAD_DIM ** -0.5 * LOG2E)
        k2 = k_ref[:, ps]
        for hh in range(2):
            h = 2 * p + hh
            hs = slice(h * LANES, (h + 1) * LANES)
            sel = lo if hh == 0 else jnp.logical_not(lo)
            qo_ref[h] = (jnp.where(sel, q2, 0.0) + bias_q[:, hs]).astype(BF16)
            ko_ref[h] = (jnp.where(sel, k2, 0.0) + bias_k[:, hs]).astype(BF16)
        for src, dst in ((k2, kt_ref), (v_ref[:, ps], vt_ref)):
            x_t = jnp.transpose(src)
            dst[2 * p] = x_t[:ATTN_HEAD_DIM]
            dst[2 * p + 1] = x_t[ATTN_HEAD_DIM:]


def _attn_prep(proj3, c_rows, consts, *, nb, b_off, tile):
    l = c_rows.shape[1]
    col = lambda off: off // D_ATTN
    sq, sk, one_q, one_k = consts
    c3 = lambda b, i: (0, 0, 0)
    c2 = lambda b, i: (0, 0)
    return pl.pallas_call(
        _attn_prep_kernel,
        grid=(nb, l // tile),
        in_specs=[
            pl.BlockSpec((None, tile, D_ATTN), lambda b, i: (b + b_off, i, col(P_Q))),
            pl.BlockSpec((None, tile, D_ATTN), lambda b, i: (b + b_off, i, col(P_K))),
            pl.BlockSpec((None, tile, D_ATTN), lambda b, i: (b + b_off, i, col(P_V))),
            pl.BlockSpec((None, tile, ATTN_HEADS), lambda b, i: (b, i, 0)),
            pl.BlockSpec(sq.shape, c3), pl.BlockSpec(sk.shape, c3),
            pl.BlockSpec(one_q.shape, c2), pl.BlockSpec(one_k.shape, c2),
        ],
        out_specs=[
            pl.BlockSpec((None, ATTN_HEADS, tile, LANES), lambda b, i: (b, 0, i, 0)),
            pl.BlockSpec((None, ATTN_HEADS, tile, LANES), lambda b, i: (b, 0, i, 0)),
            pl.BlockSpec((None, ATTN_HEADS, ATTN_HEAD_DIM, tile), lambda b, i: (b, 0, 0, i)),
            pl.BlockSpec((None, ATTN_HEADS, ATTN_HEAD_DIM, tile), lambda b, i: (b, 0, 0, i)),
        ],
        out_shape=[
            jax.ShapeDtypeStruct((nb, ATTN_HEADS, l, LANES), BF16),
            jax.ShapeDtypeStruct((nb, ATTN_HEADS, l, LANES), BF16),
            jax.ShapeDtypeStruct((nb, ATTN_HEADS, ATTN_HEAD_DIM, l), F32),
            jax.ShapeDtypeStruct((nb, ATTN_HEADS, ATTN_HEAD_DIM, l), F32),
        ],
        compiler_params=_params("parallel", "parallel"),
        name="attn_prep",
    )(proj3, proj3, proj3, c_rows, sq, sk, one_q, one_k)


QBLK = 256
ACC_ROWS = ATTN_HEAD_DIM + 16


def _attn_prompt_kernel(it_ref, jt_ref, q_ref, k_ref, vt_ref, km_ref, vtm_ref, o_ref, m_sc, acc_sc,
                        *, meta_lo, meta_hi):
    t = pl.program_id(1)
    i = it_ref[t]
    j = jt_ref[t]
    nqb = ATT_T // QBLK

    def q_block(h, r):
        return q_ref[h, r * QBLK:(r + 1) * QBLK, :]

    def v_rows(v_t):
        return jnp.concatenate([v_t.astype(BF16), jnp.ones((ACC_ROWS - ATTN_HEAD_DIM, v_t.shape[1]), BF16)], axis=0)

    def head_update(h, scores, first):
        if not first:
            m_old_all, acc_old_all = m_sc[h], acc_sc[h]
        m_out, acc_out = [], []
        for r, (s_t, v_aug) in enumerate(scores):
            qs = slice(r * QBLK, (r + 1) * QBLK)
            mx = jnp.max(s_t, axis=0, keepdims=True)
            if first:
                m_new = mx
                acc_new = _dot(v_aug, jnp.exp2(s_t - m_new).astype(BF16))
            else:
                m_old = m_old_all[:, qs]
                m_new = jnp.maximum(m_old, mx)
                acc_new = (acc_old_all[:, qs] * jnp.exp2(m_old - m_new)
                           + _dot(v_aug, jnp.exp2(s_t - m_new).astype(BF16)))
            m_out.append(m_new)
            acc_out.append(acc_new)
        m_sc[h] = jnp.concatenate(m_out, axis=1)
        acc_sc[h] = jnp.concatenate(acc_out, axis=1)

    def run(score_fn, first):
        nxt = [score_fn(0, r) for r in range(nqb)]
        for h in range(ATTN_HEADS):
            cur = nxt
            if h + 1 < ATTN_HEADS:
                nxt = [score_fn(h + 1, r) for r in range(nqb)]
            head_update(h, cur, first)

    @pl.when(j == 0)
    def _():
        krow = lax.broadcasted_iota(jnp.int32, (km_ref.shape[1], 1), 0)
        is_meta = jnp.logical_and(krow >= meta_lo, krow < meta_hi)
        run(lambda h, r: (jnp.where(is_meta, _dot_nt(km_ref[h], q_block(h, r)), -jnp.inf),
                          v_rows(vtm_ref[h])), True)

    @pl.when(j < i)
    def _():
        run(lambda h, r: (_dot_nt(k_ref[h], q_block(h, r)), v_rows(vt_ref[h])), False)

    @pl.when(j == i)
    def _():
        def diag_scores(h, r):
            nk = (r + 1) * QBLK
            s_t = _dot_nt(k_ref[h, :nk, :], q_block(h, r))
            krow = lax.broadcasted_iota(jnp.int32, (nk, QBLK), 0)
            qcol = lax.broadcasted_iota(jnp.int32, (nk, QBLK), 1) + r * QBLK
            return jnp.where(krow <= qcol, s_t, -jnp.inf), v_rows(vt_ref[h, :, :nk])

        run(diag_scores, False)
        d = ATTN_HEAD_DIM
        for p in range(HEAD_PAIRS):
            o_t = jnp.concatenate([acc_sc[h, :d, :] / acc_sc[h, d:d + 1, :] for h in (2 * p, 2 * p + 1)], axis=0)
            o_ref[:, p * LANES:(p + 1) * LANES] = jnp.transpose(o_t).astype(o_ref.dtype)


def _attn_prompt(q_aug, k_aug, v_t, km_aug, vm_t, *, meta_lo):
    nb, _, l, _ = q_aug.shape
    nt = l // ATT_T
    it = np.array([i for i in range(nt) for _ in range(i + 1)], np.int32)
    jt = np.array([j for i in range(nt) for j in range(i + 1)], np.int32)
    mrows = km_aug.shape[2]
    grid_spec = pltpu.PrefetchScalarGridSpec(
        num_scalar_prefetch=2,
        grid=(nb, len(it)),
        in_specs=[
            pl.BlockSpec((None, ATTN_HEADS, ATT_T, LANES), lambda b, t, it, jt: (b, 0, it[t], 0)),
            pl.BlockSpec((None, ATTN_HEADS, ATT_T, LANES), lambda b, t, it, jt: (b, 0, jt[t], 0)),
            pl.BlockSpec((None, ATTN_HEADS, ATTN_HEAD_DIM, ATT_T), lambda b, t, it, jt: (b, 0, 0, jt[t])),
            pl.BlockSpec((None, ATTN_HEADS, mrows, LANES), lambda b, t, it, jt: (0, 0, 0, 0)),
            pl.BlockSpec((None, ATTN_HEADS, ATTN_HEAD_DIM, mrows), lambda b, t, it, jt: (0, 0, 0, 0)),
        ],
        out_specs=pl.BlockSpec((None, ATT_T, D_ATTN), lambda b, t, it, jt: (b, it[t], 0)),
        scratch_shapes=[
            pltpu.VMEM((ATTN_HEADS, 1, ATT_T), F32),
            pltpu.VMEM((ATTN_HEADS, ACC_ROWS, ATT_T), F32),
        ],
    )
    return pl.pallas_call(
        functools.partial(_attn_prompt_kernel, meta_lo=meta_lo, meta_hi=meta_lo + N_META),
        grid_spec=grid_spec,
        out_shape=jax.ShapeDtypeStruct((nb, l, D_ATTN), BF16),
        compiler_params=_params("parallel", "arbitrary"),
        name="attn_prompt",
    )(jnp.asarray(it), jnp.asarray(jt), q_aug, k_aug, v_t, km_aug, vm_t)


def _attn_sample_kernel(q_ref, kn_ref, vn_ref, km_ref, vm_ref, kct_ref, vct_ref, cq_ref, ckm_ref, ckc_ref,
                        ckn_ref, o_ref):
    T = CHUNK
    D = ATTN_HEAD_DIM
    row = lax.broadcasted_iota(jnp.int32, (T, T), 0)
    col = lax.broadcasted_iota(jnp.int32, (T, T), 1)
    outs = []
    for hh in range(2):
        hs = slice(hh * D, (hh + 1) * D)
        qh = (q_ref[:, hs] * (D ** -0.5)).astype(BF16)
        cq = cq_ref[hh]
        s_m = _dot_nt(qh, km_ref[:, hs].astype(BF16)) + cq - ckm_ref[hh:hh + 1, :]
        s_c = _dot(qh, kct_ref[hh].astype(BF16)) + cq - ckc_ref[hh:hh + 1, :]
        s_n = _dot_nt(qh, kn_ref[:, hs].astype(BF16)) + cq - ckn_ref[hh:hh + 1, :]
        s_n = jnp.where(col <= row, s_n, -jnp.inf)
        m = jnp.maximum(jnp.maximum(jnp.max(s_m, axis=1, keepdims=True), jnp.max(s_c, axis=1, keepdims=True)),
                        jnp.max(s_n, axis=1, keepdims=True))
        p_m = jnp.exp(s_m - m)
        p_c = jnp.exp(s_c - m)
        p_n = jnp.exp(s_n - m)
        l = (jnp.sum(p_m, axis=1, keepdims=True) + jnp.sum(p_c, axis=1, keepdims=True)
             + jnp.sum(p_n, axis=1, keepdims=True))
        o = (_dot(p_m.astype(BF16), vm_ref[:, hs].astype(BF16)) + _dot_nt(p_c.astype(BF16), vct_ref[hh].astype(BF16))
             + _dot(p_n.astype(BF16), vn_ref[:, hs].astype(BF16)))
        outs.append(o / l)
    o_ref[...] = jnp.concatenate(outs, axis=1).astype(o_ref.dtype)


def _attn_sample(proj3, proj2, meta_blk, cache_k, cache_v, cq, ckm_t, ckc_t, ckn_t, *, nb):
    past = cache_k.shape[3]
    lb = lambda off: off // LANES
    pair = lambda off: (lambda b, p: (b, 0, lb(off) + p))
    ckspec = lambda n: pl.BlockSpec((None, 2, n), lambda b, p: (b * HEAD_PAIRS + p, 0, 0))
    return pl.pallas_call(
        _attn_sample_kernel,
        grid=(nb, HEAD_PAIRS),
        in_specs=[
            pl.BlockSpec((None, CHUNK, LANES), pair(P_Q)),
            pl.BlockSpec((None, CHUNK, LANES), pair(P_K)),
            pl.BlockSpec((None, CHUNK, LANES), pair(P_V)),
            pl.BlockSpec((N_META, LANES), lambda b, p: (meta_blk, lb(P_K) + p)),
            pl.BlockSpec((N_META, LANES), lambda b, p: (meta_blk, lb(P_V) + p)),
            pl.BlockSpec((None, 2, ATTN_HEAD_DIM, past), lambda b, p: (b, p, 0, 0)),
            pl.BlockSpec((None, 2, ATTN_HEAD_DIM, past), lambda b, p: (b, p, 0, 0)),
            pl.BlockSpec((None, 2, CHUNK, 1), lambda b, p: (b, p, 0, 0)),
            ckspec(N_META), ckspec(past), ckspec(CHUNK),
        ],
        out_specs=pl.BlockSpec((None, CHUNK, LANES), lambda b, p: (b, 0, p)),
        out_shape=jax.ShapeDtypeStruct((nb, CHUNK, D_ATTN), BF16),
        compiler_params=_params("parallel", "parallel"),
        name="attn_sample",
    )(proj3, proj3, proj3, proj2, proj2, cache_k, cache_v, cq, ckm_t, ckc_t, ckn_t)


def _merge_kernel(os_ref, oa_ref, gs_ref, ga_ref, h_ref, wbs_ref, wba_ref, wo_ref, o_ref):
    bs = _dot(os_ref[...], wbs_ref[...])
    ba = _dot(oa_ref[...], wba_ref[...])
    merged = jax.nn.sigmoid(gs_ref[...]) * bs + jax.nn.sigmoid(ga_ref[...]) * ba
    o_ref[...] = h_ref[...] + _dot(merged.astype(BF16), wo_ref[...])


def _merge(o_ssm, o_attn, proj, h, w_bs, w_ba, w_o, *, tm):
    rows = o_ssm.shape[0]
    const = lambda i: (0, 0)
    return pl.pallas_call(
        _merge_kernel,
        grid=(rows // tm,),
        in_specs=[
            pl.BlockSpec((tm, D_SSM), lambda i: (i, 0)),
            pl.BlockSpec((tm, D_ATTN), lambda i: (i, 0)),
            pl.BlockSpec((tm, D_MODEL), lambda i: (i, P_GS // D_MODEL)),
            pl.BlockSpec((tm, D_MODEL), lambda i: (i, P_GA // D_MODEL)),
            pl.BlockSpec((tm, D_MODEL), lambda i: (i, 0)),
            pl.BlockSpec((D_SSM, D_MODEL), const),
            pl.BlockSpec((D_ATTN, D_MODEL), const),
            pl.BlockSpec((D_MODEL, D_MODEL), const),
        ],
        out_specs=pl.BlockSpec((tm, D_MODEL), lambda i: (i, 0)),
        out_shape=jax.ShapeDtypeStruct((rows, D_MODEL), F32),
        compiler_params=_params("parallel"),
        name="merge",
    )(o_ssm, o_attn, proj, proj, h, w_bs, w_ba, w_o)


def kernel(x_prompt, x_sample, cache_k, cache_v, cache_logf, state_ssm, state_conv, meta_tokens, norm_ffn1, ffn1_w_in, ffn1_w_out, norm_mix, w_in, conv_w, conv_b, dt_bias, a_log, d_skip, f_bias, ssm_norm, w_br_ssm, w_br_attn, w_out, norm_ffn2, ffn2_w_in, ffn2_w_out, norm_final):
    assert norm_ffn1.shape[0] == 1, "single-layer trunk"
    bp, seq, _ = x_prompt.shape
    bs, dseq, _ = x_sample.shape
    assert dseq == CHUNK and seq % ATT_T == 0
    n_s = bs * dseq
    rows_small = n_s + 2 * CHUNK
    meta_row0 = n_s + CHUNK - N_META
    meta_blk = meta_row0 // N_META

    row = lambda v: v.reshape(1, -1).astype(F32)
    w = w_in[0]
    w_main = jnp.concatenate([w[:, OFF_XBC:OFF_DT], w[:, :OFF_XBC], w[:, OFF_Q:OFF_F], w[:, OFF_GS:]],
                             axis=1).astype(BF16)
    pad_small = LANES - SSM_HEADS - ATTN_HEADS
    w_small = jnp.concatenate([w[:, OFF_DT:OFF_Q], w[:, OFF_F:OFF_GS], jnp.zeros((D_MODEL, pad_small), F32)],
                              axis=1).astype(BF16)
    b_small = jnp.concatenate([dt_bias[0], f_bias[0], jnp.zeros((pad_small,), F32)]).reshape(1, LANES)
    w1i, w1o = ffn1_w_in[0].astype(BF16), ffn1_w_out[0].astype(BF16)
    w2i, w2o = ffn2_w_in[0].astype(BF16), ffn2_w_out[0].astype(BF16)
    w_bs, w_ba, w_o = w_br_ssm[0].astype(BF16), w_br_attn[0].astype(BF16), w_out[0].astype(BF16)
    expand = jnp.asarray(np.tile(np.repeat(np.eye(SSM_HEADS, dtype=np.float32), SSM_HEAD_DIM, axis=1), (3, 1)),
                         BF16)
    ssd_consts = (conv_w[0], row(conv_b[0]), row(a_log[0]), row(jnp.repeat(d_skip[0], SSM_HEAD_DIM)),
                  row(ssm_norm[0]), expand)

    def front(x, tm_ffn, tm_proj):
        h1, u = _ffn(x, row(norm_ffn1[0]), w1i, w1o, row(norm_mix[0]), tm=tm_ffn, emit_h=True, u_dtype=BF16)
        return h1, _proj(u, w_main, tm=tm_proj), _small(u, w_small, b_small, tm=tm_proj)

    xp = x_prompt.reshape(bp * seq, D_MODEL)
    h1_p, pm_p, ps_p = front(xp, 512, 1024)
    x_small = jnp.concatenate([x_sample.reshape(n_s, D_MODEL), jnp.zeros((CHUNK - N_META, D_MODEL), F32),
                               meta_tokens, jnp.zeros((CHUNK, D_MODEL), F32)], axis=0)
    h1_s, pm_s, ps_s = front(x_small, rows_small // 2, rows_small // 2)

    pm_p3 = pm_p.reshape(bp, seq, P_TOTAL)
    ps_p3 = ps_p.reshape(bp, seq, LANES)
    pm_s3 = pm_s.reshape(rows_small // CHUNK, CHUNK, P_TOTAL)
    ps_s3 = ps_s.reshape(rows_small // CHUNK, CHUNK, LANES)

    zero_state = jnp.zeros((1, D_STATE, D_SSM), F32)
    zero_hist = jnp.zeros((1, CONV_W - 1, D_CONV), F32)
    _, st_m, hist_m = _ssd(pm_s3, ps_s3, zero_state, zero_hist, *ssd_consts,
                           nb=1, b_off=bs, shared_init=True, n_pad=CHUNK - N_META)
    o_ssm_p, st_p, hist_p = _ssd(pm_p3, ps_p3, st_m, hist_m, *ssd_consts,
                                 nb=bp, b_off=0, shared_init=True, n_pad=0)
    st_s0 = jnp.transpose(state_ssm[0].reshape(bs, D_SSM, D_STATE), (0, 2, 1))
    o_ssm_s, st_s, hist_s = _ssd(pm_s3, ps_s3, st_s0, state_conv[0], *ssd_consts,
                                 nb=bs, b_off=0, shared_init=False, n_pad=0)

    t_last = lambda a: jnp.swapaxes(a, -1, -2)
    lf_m = t_last(ps_s[meta_row0:meta_row0 + N_META, S_F:S_F + ATTN_HEADS])
    lf_p = t_last(ps_p3[:, :, S_F:S_F + ATTN_HEADS])
    lf_s = ps_s[:n_s, S_F:S_F + ATTN_HEADS].reshape(bs, dseq, ATTN_HEADS)
    lf_all_p = jnp.concatenate([jnp.broadcast_to(lf_m[None], (bp, ATTN_HEADS, N_META)), lf_p], axis=2)
    lf_all_s = jnp.concatenate([jnp.broadcast_to(lf_m[None], (bs, ATTN_HEADS, N_META)),
                                t_last(cache_logf[0].astype(F32)), t_last(lf_s)], axis=2)
    past = cache_logf.shape[2]
    c_p = _cum_logf(lf_all_p)
    c_s = _cum_logf(lf_all_s)

    prep_consts = _bias_placement()
    q_aug, k_aug, k_t, v_t = _attn_prep(pm_p3, t_last(c_p[:, :, N_META:]), prep_consts,
                                        nb=bp, b_off=0, tile=ATT_T)
    meta_tile = 2 * CHUNK
    meta_lo = meta_row0 - n_s
    c_meta = jnp.pad(t_last(c_p[:1, :, :N_META]), ((0, 0), (meta_lo, meta_tile - meta_lo - N_META), (0, 0)))
    _, km_aug, km_t, vm_t = _attn_prep(pm_s.reshape(rows_small // meta_tile, meta_tile, P_TOTAL), c_meta,
                                       prep_consts, nb=1, b_off=n_s // meta_tile, tile=meta_tile)
    o_attn_p = _attn_prompt(q_aug, k_aug, v_t, km_aug, vm_t, meta_lo=meta_lo)
    per_pair = lambda c: c.reshape(bs * HEAD_PAIRS, 2, c.shape[2])
    feat_major = lambda a: jnp.transpose(a, (0, 2, 3, 1))
    o_attn_s = _attn_sample(pm_s3, pm_s, meta_blk, feat_major(cache_k[0]), feat_major(cache_v[0]),
                            c_s[:, :, N_META + past:, None],
                            per_pair(c_s[:, :, :N_META]), per_pair(c_s[:, :, N_META:N_META + past]),
                            per_pair(c_s[:, :, N_META + past:]), nb=bs)

    def back(o_ssm, o_attn, pm, h1, tm):
        h2 = _merge(o_ssm, o_attn, pm, h1, w_bs, w_ba, w_o, tm=tm)
        return _ffn(h2, row(norm_ffn2[0]), w2i, w2o, row(norm_final), tm=tm, emit_h=False, u_dtype=F32)[0]

    y_prompt = back(o_ssm_p.reshape(bp * seq, D_SSM), o_attn_p.reshape(bp * seq, D_ATTN), pm_p, h1_p, 512)
    y_sample = back(o_ssm_s.reshape(n_s, D_SSM), o_attn_s.reshape(n_s, D_ATTN), pm_s, h1_s, 512)

    heads = lambda a: a.reshape(1, a.shape[0], a.shape[1], ATTN_HEADS, ATTN_HEAD_DIM)

    def with_meta(x_t, m_t):
        m_t = jnp.broadcast_to(m_t[:, :, :, meta_lo:meta_lo + N_META], (bp, ATTN_HEADS, ATTN_HEAD_DIM, N_META))
        return jnp.transpose(jnp.concatenate([m_t, x_t], axis=3), (0, 3, 1, 2))[None]

    k_prompt = with_meta(k_t, km_t)
    v_prompt = with_meta(v_t, vm_t)
    logf_prompt = t_last(lf_all_p)
    k_sample = heads(pm_s[:n_s, P_K:P_K + D_ATTN].reshape(bs, dseq, D_ATTN))
    v_sample = heads(pm_s[:n_s, P_V:P_V + D_ATTN].reshape(bs, dseq, D_ATTN))
    state_out = lambda st: jnp.transpose(st, (0, 2, 1)).reshape(1, st.shape[0], SSM_HEADS, SSM_HEAD_DIM, D_STATE)
    return (y_prompt.reshape(bp, seq, D_MODEL), y_sample.reshape(bs, dseq, D_MODEL),
            k_prompt, v_prompt, logf_prompt[None], state_out(st_p), hist_p[None],
            k_sample, v_sample, lf_s[None], state_out(st_s), hist_s[None])
```

```python
import functools

import numpy as np
import jax
import jax.numpy as jnp
from jax import lax
from jax.experimental import pallas as pl
from jax.experimental.pallas import tpu as pltpu

F32 = jnp.float32
BF16 = jnp.bfloat16

D_MODEL = 1024
N_META = 16
CHUNK = 64
EPS = 1e-6
D_SSM = 2048
SSM_HEADS = 32
SSM_HEAD_DIM = 64
SSM_GROUPS = 8
D_STATE = 128
CONV_W = 4
D_CONV = D_SSM + 2 * SSM_GROUPS * D_STATE
ATTN_HEADS = 16
ATTN_HEAD_DIM = 64
D_ATTN = ATTN_HEADS * ATTN_HEAD_DIM
D_FF = 2816
HEAD_PAIRS = ATTN_HEADS // 2
LANES = 128
GROUP_W = D_SSM // SSM_GROUPS

OFF_XBC = D_SSM
OFF_DT = OFF_XBC + D_CONV
OFF_Q = OFF_DT + SSM_HEADS
OFF_K = OFF_Q + D_ATTN
OFF_V = OFF_K + D_ATTN
OFF_F = OFF_V + D_ATTN
OFF_GS = OFF_F + ATTN_HEADS
OFF_GA = OFF_GS + D_MODEL

P_XBC = 0
P_Z = D_CONV
P_Q = P_Z + D_SSM
P_K = P_Q + D_ATTN
P_V = P_K + D_ATTN
P_GS = P_V + D_ATTN
P_GA = P_GS + D_MODEL
P_TOTAL = P_GA + D_MODEL
S_DT = 0
S_F = SSM_HEADS

VMEM_LIMIT = 56 * 1024 * 1024

PROJ_COL_TILES = 4
PP_GS = P_Q
PP_GA = PP_GS + D_MODEL
ATT_T = 512
CUMSUM_BLK = 256


def _dot(a, b):
    return jnp.dot(a, b, preferred_element_type=F32)


def _dot_nt(a, b):
    return lax.dot_general(a, b, (((1,), (1,)), ((), ())), preferred_element_type=F32)


def _dot_tn(a, b):
    return lax.dot_general(a, b, (((0,), (0,)), ((), ())), preferred_element_type=F32)


def _split3(x):
    hi = x.astype(BF16)
    r = x - hi.astype(F32)
    mid = r.astype(BF16)
    lo = (r - mid.astype(F32)).astype(BF16)
    return hi, mid, lo


def _rmsnorm(x, g):
    return x * lax.rsqrt(jnp.mean(x * x, axis=-1, keepdims=True) + EPS) * g


def _softplus(x):
    return jnp.maximum(x, 0.0) + jnp.log1p(jnp.exp(-jnp.abs(x)))


def _params(*sem):
    return pltpu.CompilerParams(dimension_semantics=sem, vmem_limit_bytes=VMEM_LIMIT)


FFN_HALVES = 2


def _ffn_kernel(x_ref, g1_ref, wi_ref, wo_ref, g2_ref, *out_refs, emit_h):
    h_ref, u_ref = out_refs if emit_h else (None,) + out_refs
    rows = x_ref.shape[0] // FFN_HALVES
    for s in range(FFN_HALVES):
        rs = slice(s * rows, (s + 1) * rows)
        x = x_ref[rs, :]
        xn = _rmsnorm(x, g1_ref[...]).astype(BF16)
        a = _dot(xn, wi_ref[:, :D_FF])
        b = _dot(xn, wi_ref[:, D_FF:])
        g = (a * jax.nn.sigmoid(a) * b).astype(BF16)
        h = x + 0.5 * _dot(g, wo_ref[...])
        if emit_h:
            h_ref[rs, :] = h
        u_ref[rs, :] = _rmsnorm(h, g2_ref[...]).astype(u_ref.dtype)


def _ffn(x, g1, w_in, w_out, g2, *, tm, emit_h, u_dtype):
    rows = x.shape[0]
    row_spec = pl.BlockSpec((tm, D_MODEL), lambda i: (i, 0))
    vec_spec = pl.BlockSpec((1, D_MODEL), lambda i: (0, 0))
    resident = lambda shape: pl.BlockSpec(shape, lambda i: (0, 0), pipeline_mode=pl.Buffered(1))
    out_shape = [jax.ShapeDtypeStruct((rows, D_MODEL), u_dtype)]
    out_specs = [row_spec]
    if emit_h:
        out_shape = [jax.ShapeDtypeStruct((rows, D_MODEL), F32)] + out_shape
        out_specs = [row_spec] + out_specs
    return pl.pallas_call(
        functools.partial(_ffn_kernel, emit_h=emit_h),
        grid=(rows // tm,),
        in_specs=[row_spec, vec_spec, resident((D_MODEL, 2 * D_FF)), resident((D_FF, D_MODEL)), vec_spec],
        out_specs=out_specs,
        out_shape=out_shape,
        compiler_params=_params("parallel"),
        name="ffn",
    )(x, g1, w_in, w_out, g2)


def _proj_kernel(u_ref, w_ref, o_ref):
    o_ref[...] = _dot(u_ref[...], w_ref[...])


def _proj(u, w, *, tm):
    rows = u.shape[0]
    width = w.shape[1]
    tn = width // PROJ_COL_TILES
    return pl.pallas_call(
        _proj_kernel,
        grid=(PROJ_COL_TILES, rows // tm),
        in_specs=[pl.BlockSpec((tm, D_MODEL), lambda j, i: (i, 0)),
                  pl.BlockSpec((D_MODEL, tn), lambda j, i: (0, j))],
        out_specs=pl.BlockSpec((tm, tn), lambda j, i: (i, j)),
        out_shape=jax.ShapeDtypeStruct((rows, width), F32),
        compiler_params=_params("parallel", "parallel"),
        name="proj",
    )(u, w)


def _small_kernel(u_ref, w_ref, b_ref, o_ref):
    x = _dot(u_ref[...], w_ref[...]) + b_ref[...]
    lane = lax.broadcasted_iota(jnp.int32, (1, LANES), 1)
    sgn = jnp.where(lane < S_F, 1.0, -1.0)
    o_ref[...] = sgn * _softplus(sgn * x)


def _small(u, w, b, *, tm):
    rows = u.shape[0]
    return pl.pallas_call(
        _small_kernel,
        grid=(rows // tm,),
        in_specs=[pl.BlockSpec((tm, D_MODEL), lambda i: (i, 0)),
                  pl.BlockSpec((D_MODEL, LANES), lambda i: (0, 0)),
                  pl.BlockSpec((1, LANES), lambda i: (0, 0))],
        out_specs=pl.BlockSpec((tm, LANES), lambda i: (i, 0)),
        out_shape=jax.ShapeDtypeStruct((rows, LANES), F32),
        compiler_params=_params("parallel"),
        name="small",
    )(u, w, b)


HIST0 = 8 - (CONV_W - 1)


def _ssd_kernel(xbc_ref, z_ref, dt_ref, s0_ref, h0_ref, cw_ref, cb_ref, alog_ref, dskip_ref, norm_ref,
                e_ref, o_ref, sT_ref, hT_ref, st_sc, xp_sc, y_sc, *, n_pad):
    T = CHUNK
    c = pl.program_id(1)

    @pl.when(c == 0)
    def _():
        st_sc[...] = s0_ref[...]
        xp_sc[HIST0:8, :] = h0_ref[...]

    xraw = xbc_ref[...]
    xp_sc[8:8 + T, :] = xraw
    cw = cw_ref[...]
    xp = xp_sc[...]
    conv = cb_ref[...]
    for j in range(CONV_W - 1):
        conv = conv + cw[j:j + 1] * pltpu.roll(xp, CONV_W - 1 - j, axis=0)[8:8 + T, :]
    conv = conv + cw[3:4] * xraw
    xp_sc[HIST0:8, :] = xraw[T - (CONV_W - 1):T, :]
    xc = conv * jax.nn.sigmoid(conv)
    dtv = dt_ref[:, S_DT:S_DT + SSM_HEADS]
    if n_pad:
        valid = lax.broadcasted_iota(jnp.int32, (T, 1), 0) >= n_pad
        xc = jnp.where(valid, xc, 0.0)
        dtv = jnp.where(valid, dtv, 0.0)

    a = -jnp.exp(alog_ref[...])
    adt = a * dtv
    row = lax.broadcasted_iota(jnp.int32, (T, T), 0)
    col = lax.broadcasted_iota(jnp.int32, (T, T), 1)
    causal = col <= row
    tril = jnp.where(causal, 1.0, 0.0).astype(BF16)
    r32 = lax.broadcasted_iota(jnp.int32, (SSM_HEADS, SSM_HEADS), 0)
    c32 = lax.broadcasted_iota(jnp.int32, (SSM_HEADS, SSM_HEADS), 1)
    eye = jnp.where(r32 == c32, 1.0, 0.0).astype(BF16)
    expand3 = e_ref[...]

    def expand(v):
        return _dot(jnp.concatenate(_split3(v), axis=1), expand3)

    a_cs = sum(_dot(tril, piece) for piece in _split3(adt))
    a_cs_t = sum(_dot_nt(eye, piece) for piece in _split3(a_cs))
    a_last = a_cs[T - 1:T, :]
    dt_e = expand(dtv)
    wdec_e = expand(jnp.exp(a_last - a_cs) * dtv)
    eacs = expand(jnp.exp(a_cs))
    cdec = eacs[T - 1:T, :]

    xs = xc[:, :D_SSM]
    bm = xc[:, D_SSM:D_SSM + SSM_GROUPS * D_STATE].astype(BF16)
    cm = xc[:, D_SSM + SSM_GROUPS * D_STATE:].astype(BF16)
    xd = xs * dt_e
    xdw = (xs * wdec_e).astype(BF16)
    lane = lax.broadcasted_iota(jnp.int32, (1, D_SSM), 1)
    lo = (lane & SSM_HEAD_DIM) == 0
    xd_lo = jnp.where(lo, xd, 0.0).astype(BF16)
    xd_hi = jnp.where(lo, 0.0, xd).astype(BF16)
    lo2 = lax.broadcasted_iota(jnp.int32, (1, LANES), 1) < SSM_HEAD_DIM
    step2 = lax.broadcasted_iota(jnp.int32, (T, LANES), 1) & (SSM_HEAD_DIM - 1)
    causal2 = step2 <= lax.broadcasted_iota(jnp.int32, (T, LANES), 0)

    for g in range(SSM_GROUPS):
        gs = slice(g * GROUP_W, (g + 1) * GROUP_W)
        cmg = cm[:, g * D_STATE:(g + 1) * D_STATE]
        bmg = bm[:, g * D_STATE:(g + 1) * D_STATE]
        cb2 = _dot_nt(cmg, jnp.concatenate([bmg, bmg], axis=0))
        stg = st_sc[:, gs]
        yoff = _dot(cmg, stg.astype(BF16))
        for rr in range(2):
            pair = g * 2 + rr
            ps = slice(pair * LANES, (pair + 1) * LANES)
            ha, hb = 2 * pair, 2 * pair + 1
            seg = (jnp.where(lo2, a_cs[:, ha:ha + 1], a_cs[:, hb:hb + 1])
                   - jnp.concatenate([a_cs_t[ha:ha + 1, :], a_cs_t[hb:hb + 1, :]], axis=1))
            m2 = (cb2 * jnp.exp(jnp.where(causal2, seg, -jnp.inf))).astype(BF16)
            ydiag = _dot(m2, jnp.concatenate([xd_lo[:, ps], xd_hi[:, ps]], axis=0))
            y_sc[:, ps] = ydiag + yoff[:, rr * LANES:(rr + 1) * LANES] * eacs[:, ps]
        st_sc[:, gs] = stg * cdec[:, gs] + _dot_tn(bmg, xdw[:, gs])

    y = y_sc[...] + dskip_ref[...] * xs
    zz = z_ref[...]
    y = y * (zz * jax.nn.sigmoid(zz))
    for g in range(SSM_GROUPS):
        gs = slice(g * GROUP_W, (g + 1) * GROUP_W)
        yg = y[:, gs]
        yg = yg * lax.rsqrt(jnp.mean(yg * yg, axis=-1, keepdims=True) + EPS)
        o_ref[:, gs] = (yg * norm_ref[:, gs]).astype(o_ref.dtype)

    @pl.when(c == pl.num_programs(1) - 1)
    def _():
        sT_ref[...] = st_sc[...]
        hT_ref[...] = xp_sc[HIST0:8, :]


def _ssd(proj3, small3, s0, h0, cw, cb, alog, dskip, norm, expand, *, nb, b_off, shared_init, n_pad):
    nc = proj3.shape[1] // CHUNK
    init = (lambda b, c: (0, 0, 0)) if shared_init else (lambda b, c: (b, 0, 0))
    const = lambda b, c: (0, 0)
    return pl.pallas_call(
        functools.partial(_ssd_kernel, n_pad=n_pad),
        grid=(nb, nc),
        in_specs=[
            pl.BlockSpec((None, CHUNK, D_CONV), lambda b, c: (b + b_off, c, P_XBC // D_CONV)),
            pl.BlockSpec((None, CHUNK, D_SSM), lambda b, c: (b + b_off, c, P_Z // D_SSM)),
            pl.BlockSpec((None, CHUNK, LANES), lambda b, c: (b + b_off, c, 0)),
            pl.BlockSpec((None, D_STATE, D_SSM), init),
            pl.BlockSpec((None, CONV_W - 1, D_CONV), init),
            pl.BlockSpec((CONV_W, D_CONV), const),
            pl.BlockSpec((1, D_CONV), const),
            pl.BlockSpec((1, SSM_HEADS), const),
            pl.BlockSpec((1, D_SSM), const),
            pl.BlockSpec((1, D_SSM), const),
            pl.BlockSpec((3 * SSM_HEADS, D_SSM), const),
        ],
        out_specs=[
            pl.BlockSpec((None, CHUNK, D_SSM), lambda b, c: (b, c, 0)),
            pl.BlockSpec((None, D_STATE, D_SSM), lambda b, c: (b, 0, 0)),
            pl.BlockSpec((None, CONV_W - 1, D_CONV), lambda b, c: (b, 0, 0)),
        ],
        out_shape=[
            jax.ShapeDtypeStruct((nb, nc * CHUNK, D_SSM), BF16),
            jax.ShapeDtypeStruct((nb, D_STATE, D_SSM), F32),
            jax.ShapeDtypeStruct((nb, CONV_W - 1, D_CONV), F32),
        ],
        scratch_shapes=[
            pltpu.VMEM((D_STATE, D_SSM), F32),
            pltpu.VMEM((8 + CHUNK, D_CONV), F32),
            pltpu.VMEM((CHUNK, D_SSM), F32),
        ],
        compiler_params=_params("parallel", "arbitrary"),
        name="ssd",
    )(proj3, proj3, small3, s0, h0, cw, cb, alog, dskip, norm, expand)


def _cumsum_kernel(x_ref, o_ref):
    n = x_ref.shape[-1] // CUMSUM_BLK
    r = lax.broadcasted_iota(jnp.int32, (CUMSUM_BLK, CUMSUM_BLK), 0)
    c = lax.broadcasted_iota(jnp.int32, (CUMSUM_BLK, CUMSUM_BLK), 1)
    upper = jnp.where(r <= c, 1.0, 0.0).astype(BF16)
    carry = jnp.zeros((ATTN_HEADS, 1), F32)
    for i in range(n):
        blk = slice(i * CUMSUM_BLK, (i + 1) * CUMSUM_BLK)
        cs = sum(_dot(piece, upper) for piece in _split3(x_ref[:, blk])) + carry
        o_ref[:, blk] = cs
        carry = cs[:, CUMSUM_BLK - 1:CUMSUM_BLK]


def _cumsum_t(x_t):
    b, h, l = x_t.shape
    spec = pl.BlockSpec((None, h, l), lambda i: (i, 0, 0))
    return pl.pallas_call(
        _cumsum_kernel, grid=(b,), in_specs=[spec], out_specs=spec,
        out_shape=jax.ShapeDtypeStruct(x_t.shape, F32),
        compiler_params=_params("parallel"), name="cumsum",
    )(x_t)


def _cum_logf(lf_t):
    l = lf_t.shape[2]
    lp = -(-l // CUMSUM_BLK) * CUMSUM_BLK
    return _cumsum_t(jnp.pad(lf_t, ((0, 0), (0, 0), (0, lp - l))))[:, :, :l]


BIAS_PIECES = 3
LOG2E = 1.4426950408889634


def _bias_lane0(head):
    return ATTN_HEAD_DIM if head % 2 == 0 else 0


def _bias_placement():
    sq = np.zeros((BIAS_PIECES, ATTN_HEADS, ATTN_HEADS * LANES), np.float32)
    sk = np.zeros_like(sq)
    one_q = np.zeros((1, ATTN_HEADS * LANES), np.float32)
    one_k = np.zeros_like(one_q)
    for h in range(ATTN_HEADS):
        base = h * LANES + _bias_lane0(h)
        for piece in range(BIAS_PIECES):
            sk[piece, h, base + piece] = -1.0
            one_q[0, base + piece] = 1.0
            sq[piece, h, base + BIAS_PIECES + piece] = 1.0
            one_k[0, base + BIAS_PIECES + piece] = 1.0
    flat = lambda a: jnp.asarray(a.reshape(BIAS_PIECES * ATTN_HEADS, ATTN_HEADS * LANES), BF16)
    return flat(sq), flat(sk), jnp.asarray(one_q), jnp.asarray(one_k)


def _attn_prep_kernel(q_ref, k_ref, v_ref, c_ref, sq_ref, sk_ref, oneq_ref, onek_ref, qo_ref, ko_ref, kt_ref,
                      vt_ref):
    _prep_rows(q_ref[...], k_ref[...], v_ref[...], c_ref[...], sq_ref, sk_ref, oneq_ref, onek_ref,
               qo_ref, ko_ref, kt_ref, vt_ref, slice(None))


def _prep_rows(q, k, v, c, sq_ref, sk_ref, oneq_ref, onek_ref, qo_ref, ko_ref, kt_ref, vt_ref, rs):
    pieces = jnp.concatenate(_split3(c * LOG2E), axis=1)
    bias_q = _dot(pieces, sq_ref[...]) + oneq_ref[...]
    bias_k = _dot(pieces, sk_ref[...]) + onek_ref[...]
    lane = lax.broadcasted_iota(jnp.int32, (1, LANES), 1)
    lo = lane < ATTN_HEAD_DIM
    for p in range(HEAD_PAIRS):
        ps = slice(p * LANES, (p + 1) * LANES)
        q2 = q[:, ps] * (ATTN_HEAD_DIM ** -0.5 * LOG2E)
        k2 = k[:, ps]
        for hh in range(2):
            h = 2 * p + hh
            hs = slice(h * LANES, (h + 1) * LANES)
            sel = lo if hh == 0 else jnp.logical_not(lo)
            qo_ref[h, rs, :] = (jnp.where(sel, q2, 0.0) + bias_q[:, hs]).astype(BF16)
            ko_ref[h, rs, :] = (jnp.where(sel, k2, 0.0) + bias_k[:, hs]).astype(BF16)
        for src, dst in ((k2, kt_ref), (v[:, ps], vt_ref)):
            x_t = jnp.transpose(src)
            dst[2 * p, :, rs] = x_t[:ATTN_HEAD_DIM]
            dst[2 * p + 1, :, rs] = x_t[ATTN_HEAD_DIM:]


QKV_HALVES = 2


def _qkv_prep_kernel(u_ref, w_ref, c_ref, sq_ref, sk_ref, oneq_ref, onek_ref, qo_ref, ko_ref, kt_ref, vt_ref):
    rows = u_ref.shape[0] // QKV_HALVES
    for s in range(QKV_HALVES):
        rs = slice(s * rows, (s + 1) * rows)
        qkv = _dot(u_ref[rs, :], w_ref[...])
        q, k, v = (qkv[:, n * D_ATTN:(n + 1) * D_ATTN] for n in range(3))
        _prep_rows(q, k, v, c_ref[rs, :], sq_ref, sk_ref, oneq_ref, onek_ref, qo_ref, ko_ref, kt_ref, vt_ref, rs)


def _qkv_prep(u, w_qkv, c_rows, consts, *, tile):
    nb, l, _ = c_rows.shape
    nt = l // tile
    sq, sk, one_q, one_k = consts
    c2 = lambda b, i: (0, 0)
    return pl.pallas_call(
        _qkv_prep_kernel,
        grid=(nb, nt),
        in_specs=[
            pl.BlockSpec((tile, D_MODEL), lambda b, i: (b * nt + i, 0)),
            pl.BlockSpec(w_qkv.shape, c2, pipeline_mode=pl.Buffered(1)),
            pl.BlockSpec((None, tile, ATTN_HEADS), lambda b, i: (b, i, 0)),
            pl.BlockSpec(sq.shape, c2), pl.BlockSpec(sk.shape, c2),
            pl.BlockSpec(one_q.shape, c2), pl.BlockSpec(one_k.shape, c2),
        ],
        out_specs=[
            pl.BlockSpec((None, ATTN_HEADS, tile, LANES), lambda b, i: (b, 0, i, 0)),
            pl.BlockSpec((None, ATTN_HEADS, tile, LANES), lambda b, i: (b, 0, i, 0)),
            pl.BlockSpec((None, ATTN_HEADS, ATTN_HEAD_DIM, tile), lambda b, i: (b, 0, 0, i)),
            pl.BlockSpec((None, ATTN_HEADS, ATTN_HEAD_DIM, tile), lambda b, i: (b, 0, 0, i)),
        ],
        out_shape=[
            jax.ShapeDtypeStruct((nb, ATTN_HEADS, l, LANES), BF16),
            jax.ShapeDtypeStruct((nb, ATTN_HEADS, l, LANES), BF16),
            jax.ShapeDtypeStruct((nb, ATTN_HEADS, ATTN_HEAD_DIM, l), F32),
            jax.ShapeDtypeStruct((nb, ATTN_HEADS, ATTN_HEAD_DIM, l), F32),
        ],
        compiler_params=_params("parallel", "parallel"),
        name="qkv_prep",
    )(u, w_qkv, c_rows, sq, sk, one_q, one_k)


def _attn_prep(proj3, c_rows, consts, *, nb, b_off, tile):
    l = c_rows.shape[1]
    col = lambda off: off // D_ATTN
    sq, sk, one_q, one_k = consts
    c2 = lambda b, i: (0, 0)
    return pl.pallas_call(
        _attn_prep_kernel,
        grid=(nb, l // tile),
        in_specs=[
            pl.BlockSpec((None, tile, D_ATTN), lambda b, i: (b + b_off, i, col(P_Q))),
            pl.BlockSpec((None, tile, D_ATTN), lambda b, i: (b + b_off, i, col(P_K))),
            pl.BlockSpec((None, tile, D_ATTN), lambda b, i: (b + b_off, i, col(P_V))),
            pl.BlockSpec((None, tile, ATTN_HEADS), lambda b, i: (b, i, 0)),
            pl.BlockSpec(sq.shape, c2), pl.BlockSpec(sk.shape, c2),
            pl.BlockSpec(one_q.shape, c2), pl.BlockSpec(one_k.shape, c2),
        ],
        out_specs=[
            pl.BlockSpec((None, ATTN_HEADS, tile, LANES), lambda b, i: (b, 0, i, 0)),
            pl.BlockSpec((None, ATTN_HEADS, tile, LANES), lambda b, i: (b, 0, i, 0)),
            pl.BlockSpec((None, ATTN_HEADS, ATTN_HEAD_DIM, tile), lambda b, i: (b, 0, 0, i)),
            pl.BlockSpec((None, ATTN_HEADS, ATTN_HEAD_DIM, tile), lambda b, i: (b, 0, 0, i)),
        ],
        out_shape=[
            jax.ShapeDtypeStruct((nb, ATTN_HEADS, l, LANES), BF16),
            jax.ShapeDtypeStruct((nb, ATTN_HEADS, l, LANES), BF16),
            jax.ShapeDtypeStruct((nb, ATTN_HEADS, ATTN_HEAD_DIM, l), F32),
            jax.ShapeDtypeStruct((nb, ATTN_HEADS, ATTN_HEAD_DIM, l), F32),
        ],
        compiler_params=_params("parallel", "parallel"),
        name="attn_prep",
    )(proj3, proj3, proj3, c_rows, sq, sk, one_q, one_k)


QBLK = 256
ACC_ROWS = ATTN_HEAD_DIM + 16


def _attn_prompt_kernel(it_ref, jt_ref, q_ref, k_ref, vt_ref, km_ref, vtm_ref, o_ref, m_sc, acc_sc,
                        *, meta_lo, meta_hi):
    t = pl.program_id(1)
    i = it_ref[t]
    j = jt_ref[t]
    nqb = ATT_T // QBLK

    def q_block(h, r):
        return q_ref[h, r * QBLK:(r + 1) * QBLK, :]

    def v_rows(v_t):
        return jnp.concatenate([v_t.astype(BF16), jnp.ones((ACC_ROWS - ATTN_HEAD_DIM, v_t.shape[1]), BF16)], axis=0)

    def head_update(h, scores, first):
        if not first:
            m_old_all, acc_old_all = m_sc[h], acc_sc[h]
        m_out, acc_out = [], []
        for r, (s_t, v_aug) in enumerate(scores):
            qs = slice(r * QBLK, (r + 1) * QBLK)
            mx = jnp.max(s_t, axis=0, keepdims=True)
            if first:
                m_new = mx
                acc_new = _dot(v_aug, jnp.exp2(s_t - m_new).astype(BF16))
            else:
                m_old = m_old_all[:, qs]
                m_new = jnp.maximum(m_old, mx)
                acc_new = (acc_old_all[:, qs] * jnp.exp2(m_old - m_new)
                           + _dot(v_aug, jnp.exp2(s_t - m_new).astype(BF16)))
            m_out.append(m_new)
            acc_out.append(acc_new)
        m_sc[h] = jnp.concatenate(m_out, axis=1)
        acc_sc[h] = jnp.concatenate(acc_out, axis=1)

    def run(score_fn, first):
        nxt = [score_fn(0, r) for r in range(nqb)]
        for h in range(ATTN_HEADS):
            cur = nxt
            if h + 1 < ATTN_HEADS:
                nxt = [score_fn(h + 1, r) for r in range(nqb)]
            head_update(h, cur, first)

    @pl.when(j == 0)
    def _():
        krow = lax.broadcasted_iota(jnp.int32, (km_ref.shape[1], 1), 0)
        is_meta = jnp.logical_and(krow >= meta_lo, krow < meta_hi)
        run(lambda h, r: (jnp.where(is_meta, _dot_nt(km_ref[h], q_block(h, r)), -jnp.inf),
                          v_rows(vtm_ref[h])), True)

    @pl.when(j < i)
    def _():
        run(lambda h, r: (_dot_nt(k_ref[h], q_block(h, r)), v_rows(vt_ref[h])), False)

    @pl.when(j == i)
    def _():
        def diag_scores(h, r):
            nk = (r + 1) * QBLK
            s_t = _dot_nt(k_ref[h, :nk, :], q_block(h, r))
            krow = lax.broadcasted_iota(jnp.int32, (nk, QBLK), 0)
            qcol = lax.broadcasted_iota(jnp.int32, (nk, QBLK), 1) + r * QBLK
            return jnp.where(krow <= qcol, s_t, -jnp.inf), v_rows(vt_ref[h, :, :nk])

        run(diag_scores, False)
        d = ATTN_HEAD_DIM
        for p in range(HEAD_PAIRS):
            o_t = jnp.concatenate([acc_sc[h, :d, :] / acc_sc[h, d:d + 1, :] for h in (2 * p, 2 * p + 1)], axis=0)
            o_ref[:, p * LANES:(p + 1) * LANES] = jnp.transpose(o_t).astype(o_ref.dtype)


def _attn_prompt(q_aug, k_aug, v_t, km_aug, vm_t, *, meta_lo):
    nb, _, l, _ = q_aug.shape
    nt = l // ATT_T
    it = np.array([i for i in range(nt) for _ in range(i + 1)], np.int32)
    jt = np.array([j for i in range(nt) for j in range(i + 1)], np.int32)
    mrows = km_aug.shape[2]
    grid_spec = pltpu.PrefetchScalarGridSpec(
        num_scalar_prefetch=2,
        grid=(nb, len(it)),
        in_specs=[
            pl.BlockSpec((None, ATTN_HEADS, ATT_T, LANES), lambda b, t, it, jt: (b, 0, it[t], 0)),
            pl.BlockSpec((None, ATTN_HEADS, ATT_T, LANES), lambda b, t, it, jt: (b, 0, jt[t], 0)),
            pl.BlockSpec((None, ATTN_HEADS, ATTN_HEAD_DIM, ATT_T), lambda b, t, it, jt: (b, 0, 0, jt[t])),
            pl.BlockSpec((None, ATTN_HEADS, mrows, LANES), lambda b, t, it, jt: (0, 0, 0, 0)),
            pl.BlockSpec((None, ATTN_HEADS, ATTN_HEAD_DIM, mrows), lambda b, t, it, jt: (0, 0, 0, 0)),
        ],
        out_specs=pl.BlockSpec((None, ATT_T, D_ATTN), lambda b, t, it, jt: (b, it[t], 0)),
        scratch_shapes=[
            pltpu.VMEM((ATTN_HEADS, 1, ATT_T), F32),
            pltpu.VMEM((ATTN_HEADS, ACC_ROWS, ATT_T), F32),
        ],
    )
    return pl.pallas_call(
        functools.partial(_attn_prompt_kernel, meta_lo=meta_lo, meta_hi=meta_lo + N_META),
        grid_spec=grid_spec,
        out_shape=jax.ShapeDtypeStruct((nb, l, D_ATTN), BF16),
        compiler_params=_params("parallel", "arbitrary"),
        name="attn_prompt",
    )(jnp.asarray(it), jnp.asarray(jt), q_aug, k_aug, v_t, km_aug, vm_t)


def _attn_sample_kernel(q_ref, kn_ref, vn_ref, km_ref, vm_ref, kct_ref, vct_ref, cq_ref, ckm_ref, ckc_ref,
                        ckn_ref, o_ref):
    T = CHUNK
    D = ATTN_HEAD_DIM
    row = lax.broadcasted_iota(jnp.int32, (T, T), 0)
    col = lax.broadcasted_iota(jnp.int32, (T, T), 1)
    outs = []
    for hh in range(2):
        hs = slice(hh * D, (hh + 1) * D)
        qh = (q_ref[:, hs] * (D ** -0.5)).astype(BF16)
        cq = cq_ref[hh]
        s_m = _dot_nt(qh, km_ref[:, hs].astype(BF16)) + cq - ckm_ref[hh:hh + 1, :]
        s_c = _dot(qh, kct_ref[hh].astype(BF16)) + cq - ckc_ref[hh:hh + 1, :]
        s_n = _dot_nt(qh, kn_ref[:, hs].astype(BF16)) + cq - ckn_ref[hh:hh + 1, :]
        s_n = jnp.where(col <= row, s_n, -jnp.inf)
        m = jnp.maximum(jnp.maximum(jnp.max(s_m, axis=1, keepdims=True), jnp.max(s_c, axis=1, keepdims=True)),
                        jnp.max(s_n, axis=1, keepdims=True))
        p_m = jnp.exp(s_m - m)
        p_c = jnp.exp(s_c - m)
        p_n = jnp.exp(s_n - m)
        l = (jnp.sum(p_m, axis=1, keepdims=True) + jnp.sum(p_c, axis=1, keepdims=True)
             + jnp.sum(p_n, axis=1, keepdims=True))
        o = (_dot(p_m.astype(BF16), vm_ref[:, hs].astype(BF16)) + _dot_nt(p_c.astype(BF16), vct_ref[hh].astype(BF16))
             + _dot(p_n.astype(BF16), vn_ref[:, hs].astype(BF16)))
        outs.append(o / l)
    o_ref[...] = jnp.concatenate(outs, axis=1).astype(o_ref.dtype)


def _attn_sample(proj3, proj2, meta_blk, cache_k, cache_v, cq, ckm_t, ckc_t, ckn_t, *, nb):
    past = cache_k.shape[3]
    lb = lambda off: off // LANES
    pair = lambda off: (lambda b, p: (b, 0, lb(off) + p))
    ckspec = lambda n: pl.BlockSpec((None, 2, n), lambda b, p: (b * HEAD_PAIRS + p, 0, 0))
    return pl.pallas_call(
        _attn_sample_kernel,
        grid=(nb, HEAD_PAIRS),
        in_specs=[
            pl.BlockSpec((None, CHUNK, LANES), pair(P_Q)),
            pl.BlockSpec((None, CHUNK, LANES), pair(P_K)),
            pl.BlockSpec((None, CHUNK, LANES), pair(P_V)),
            pl.BlockSpec((N_META, LANES), lambda b, p: (meta_blk, lb(P_K) + p)),
            pl.BlockSpec((N_META, LANES), lambda b, p: (meta_blk, lb(P_V) + p)),
            pl.BlockSpec((None, 2, ATTN_HEAD_DIM, past), lambda b, p: (b, p, 0, 0)),
            pl.BlockSpec((None, 2, ATTN_HEAD_DIM, past), lambda b, p: (b, p, 0, 0)),
            pl.BlockSpec((None, 2, CHUNK, 1), lambda b, p: (b, p, 0, 0)),
            ckspec(N_META), ckspec(past), ckspec(CHUNK),
        ],
        out_specs=pl.BlockSpec((None, CHUNK, LANES), lambda b, p: (b, 0, p)),
        out_shape=jax.ShapeDtypeStruct((nb, CHUNK, D_ATTN), BF16),
        compiler_params=_params("parallel", "parallel"),
        name="attn_sample",
    )(proj3, proj3, proj3, proj2, proj2, cache_k, cache_v, cq, ckm_t, ckc_t, ckn_t)


def _merge_kernel(os_ref, oa_ref, gs_ref, ga_ref, h_ref, wbs_ref, wba_ref, wo_ref, o_ref):
    bs = _dot(os_ref[...], wbs_ref[...])
    ba = _dot(oa_ref[...], wba_ref[...])
    merged = jax.nn.sigmoid(gs_ref[...]) * bs + jax.nn.sigmoid(ga_ref[...]) * ba
    o_ref[...] = h_ref[...] + _dot(merged.astype(BF16), wo_ref[...])


def _merge(o_ssm, o_attn, proj, h, w_bs, w_ba, w_o, *, tm, gs_off, ga_off):
    rows = o_ssm.shape[0]
    const = lambda i: (0, 0)
    return pl.pallas_call(
        _merge_kernel,
        grid=(rows // tm,),
        in_specs=[
            pl.BlockSpec((tm, D_SSM), lambda i: (i, 0)),
            pl.BlockSpec((tm, D_ATTN), lambda i: (i, 0)),
            pl.BlockSpec((tm, D_MODEL), lambda i: (i, gs_off // D_MODEL)),
            pl.BlockSpec((tm, D_MODEL), lambda i: (i, ga_off // D_MODEL)),
            pl.BlockSpec((tm, D_MODEL), lambda i: (i, 0)),
            pl.BlockSpec((D_SSM, D_MODEL), const),
            pl.BlockSpec((D_ATTN, D_MODEL), const),
            pl.BlockSpec((D_MODEL, D_MODEL), const),
        ],
        out_specs=pl.BlockSpec((tm, D_MODEL), lambda i: (i, 0)),
        out_shape=jax.ShapeDtypeStruct((rows, D_MODEL), F32),
        compiler_params=_params("parallel"),
        name="merge",
    )(o_ssm, o_attn, proj, proj, h, w_bs, w_ba, w_o)


def kernel(x_prompt, x_sample, cache_k, cache_v, cache_logf, state_ssm, state_conv, meta_tokens, norm_ffn1, ffn1_w_in, ffn1_w_out, norm_mix, w_in, conv_w, conv_b, dt_bias, a_log, d_skip, f_bias, ssm_norm, w_br_ssm, w_br_attn, w_out, norm_ffn2, ffn2_w_in, ffn2_w_out, norm_final):
    assert norm_ffn1.shape[0] == 1, "single-layer trunk"
    bp, seq, _ = x_prompt.shape
    bs, dseq, _ = x_sample.shape
    assert dseq == CHUNK and seq % ATT_T == 0
    n_s = bs * dseq
    rows_small = n_s + 2 * CHUNK
    meta_row0 = n_s + CHUNK - N_META
    meta_blk = meta_row0 // N_META

    row = lambda v: v.reshape(1, -1).astype(F32)
    w = w_in[0]
    w_main = jnp.concatenate([w[:, OFF_XBC:OFF_DT], w[:, :OFF_XBC], w[:, OFF_Q:OFF_F], w[:, OFF_GS:]],
                             axis=1).astype(BF16)
    pad_small = LANES - SSM_HEADS - ATTN_HEADS
    w_small = jnp.concatenate([w[:, OFF_DT:OFF_Q], w[:, OFF_F:OFF_GS], jnp.zeros((D_MODEL, pad_small), F32)],
                              axis=1).astype(BF16)
    b_small = jnp.concatenate([dt_bias[0], f_bias[0], jnp.zeros((pad_small,), F32)]).reshape(1, LANES)
    w1i, w1o = ffn1_w_in[0].astype(BF16), ffn1_w_out[0].astype(BF16)
    w2i, w2o = ffn2_w_in[0].astype(BF16), ffn2_w_out[0].astype(BF16)
    w_bs, w_ba, w_o = w_br_ssm[0].astype(BF16), w_br_attn[0].astype(BF16), w_out[0].astype(BF16)
    expand = jnp.asarray(np.tile(np.repeat(np.eye(SSM_HEADS, dtype=np.float32), SSM_HEAD_DIM, axis=1), (3, 1)),
                         BF16)
    ssd_consts = (conv_w[0], row(conv_b[0]), row(a_log[0]), row(jnp.repeat(d_skip[0], SSM_HEAD_DIM)),
                  row(ssm_norm[0]), expand)

    def front(x, w_proj, tm_ffn, tm_proj):
        h1, u = _ffn(x, row(norm_ffn1[0]), w1i, w1o, row(norm_mix[0]), tm=tm_ffn, emit_h=True, u_dtype=BF16)
        return h1, u, _proj(u, w_proj, tm=tm_proj), _small(u, w_small, b_small, tm=tm_proj)

    w_qkv = w_main[:, P_Q:P_GS]
    w_prompt = jnp.concatenate([w_main[:, :P_Q], w_main[:, P_GS:]], axis=1)
    xp = x_prompt.reshape(bp * seq, D_MODEL)
    h1_p, u_p, pm_p, ps_p = front(xp, w_prompt, 512, 1024)
    x_small = jnp.concatenate([x_sample.reshape(n_s, D_MODEL), jnp.zeros((CHUNK - N_META, D_MODEL), F32),
                               meta_tokens, jnp.zeros((CHUNK, D_MODEL), F32)], axis=0)
    h1_s, _, pm_s, ps_s = front(x_small, w_main, rows_small // 2, rows_small // 2)

    pm_p3 = pm_p.reshape(bp, seq, w_prompt.shape[1])
    ps_p3 = ps_p.reshape(bp, seq, LANES)
    pm_s3 = pm_s.reshape(rows_small // CHUNK, CHUNK, P_TOTAL)
    ps_s3 = ps_s.reshape(rows_small // CHUNK, CHUNK, LANES)

    zero_state = jnp.zeros((1, D_STATE, D_SSM), F32)
    zero_hist = jnp.zeros((1, CONV_W - 1, D_CONV), F32)
    _, st_m, hist_m = _ssd(pm_s3, ps_s3, zero_state, zero_hist, *ssd_consts,
                           nb=1, b_off=bs, shared_init=True, n_pad=CHUNK - N_META)
    o_ssm_p, st_p, hist_p = _ssd(pm_p3, ps_p3, st_m, hist_m, *ssd_consts,
                                 nb=bp, b_off=0, shared_init=True, n_pad=0)
    st_s0 = jnp.transpose(state_ssm[0].reshape(bs, D_SSM, D_STATE), (0, 2, 1))
    o_ssm_s, st_s, hist_s = _ssd(pm_s3, ps_s3, st_s0, state_conv[0], *ssd_consts,
                                 nb=bs, b_off=0, shared_init=False, n_pad=0)

    t_last = lambda a: jnp.swapaxes(a, -1, -2)
    lf_m = t_last(ps_s[meta_row0:meta_row0 + N_META, S_F:S_F + ATTN_HEADS])
    lf_p = t_last(ps_p3[:, :, S_F:S_F + ATTN_HEADS])
    lf_s = ps_s[:n_s, S_F:S_F + ATTN_HEADS].reshape(bs, dseq, ATTN_HEADS)
    lf_all_p = jnp.concatenate([jnp.broadcast_to(lf_m[None], (bp, ATTN_HEADS, N_META)), lf_p], axis=2)
    lf_all_s = jnp.concatenate([jnp.broadcast_to(lf_m[None], (bs, ATTN_HEADS, N_META)),
                                t_last(cache_logf[0].astype(F32)), t_last(lf_s)], axis=2)
    past = cache_logf.shape[2]
    c_p = _cum_logf(lf_all_p)
    c_s = _cum_logf(lf_all_s)

    prep_consts = _bias_placement()
    q_aug, k_aug, k_t, v_t = _qkv_prep(u_p, w_qkv, t_last(c_p[:, :, N_META:]), prep_consts, tile=ATT_T)
    meta_tile = 2 * CHUNK
    meta_lo = meta_row0 - n_s
    c_meta = jnp.pad(t_last(c_p[:1, :, :N_META]), ((0, 0), (meta_lo, meta_tile - meta_lo - N_META), (0, 0)))
    _, km_aug, km_t, vm_t = _attn_prep(pm_s.reshape(rows_small // meta_tile, meta_tile, P_TOTAL), c_meta,
                                       prep_consts, nb=1, b_off=n_s // meta_tile, tile=meta_tile)
    o_attn_p = _attn_prompt(q_aug, k_aug, v_t, km_aug, vm_t, meta_lo=meta_lo)
    per_pair = lambda c: c.reshape(bs * HEAD_PAIRS, 2, c.shape[2])
    feat_major = lambda a: jnp.transpose(a, (0, 2, 3, 1))
    o_attn_s = _attn_sample(pm_s3, pm_s, meta_blk, feat_major(cache_k[0]), feat_major(cache_v[0]),
                            c_s[:, :, N_META + past:, None],
                            per_pair(c_s[:, :, :N_META]), per_pair(c_s[:, :, N_META:N_META + past]),
                            per_pair(c_s[:, :, N_META + past:]), nb=bs)

    def back(o_ssm, o_attn, pm, h1, tm, gs_off, ga_off):
        h2 = _merge(o_ssm, o_attn, pm, h1, w_bs, w_ba, w_o, tm=tm, gs_off=gs_off, ga_off=ga_off)
        return _ffn(h2, row(norm_ffn2[0]), w2i, w2o, row(norm_final), tm=tm, emit_h=False, u_dtype=F32)[0]

    y_prompt = back(o_ssm_p.reshape(bp * seq, D_SSM), o_attn_p.reshape(bp * seq, D_ATTN), pm_p, h1_p, 512,
                    PP_GS, PP_GA)
    y_sample = back(o_ssm_s.reshape(n_s, D_SSM), o_attn_s.reshape(n_s, D_ATTN), pm_s, h1_s, 512, P_GS, P_GA)

    heads = lambda a: a.reshape(1, a.shape[0], a.shape[1], ATTN_HEADS, ATTN_HEAD_DIM)

    def with_meta(x_t, m_t):
        m_t = jnp.broadcast_to(m_t[:, :, :, meta_lo:meta_lo + N_META], (bp, ATTN_HEADS, ATTN_HEAD_DIM, N_META))
        return jnp.transpose(jnp.concatenate([m_t, x_t], axis=3), (0, 3, 1, 2))[None]

    k_prompt = with_meta(k_t, km_t)
    v_prompt = with_meta(v_t, vm_t)
    logf_prompt = t_last(lf_all_p)
    k_sample = heads(pm_s[:n_s, P_K:P_K + D_ATTN].reshape(bs, dseq, D_ATTN))
    v_sample = heads(pm_s[:n_s, P_V:P_V + D_ATTN].reshape(bs, dseq, D_ATTN))
    state_out = lambda st: jnp.transpose(st, (0, 2, 1)).reshape(1, st.shape[0], SSM_HEADS, SSM_HEAD_DIM, D_STATE)
    return (y_prompt.reshape(bp, seq, D_MODEL), y_sample.reshape(bs, dseq, D_MODEL),
            k_prompt, v_prompt, logf_prompt[None], state_out(st_p), hist_p[None],
            k_sample, v_sample, lf_s[None], state_out(st_s), hist_s[None])
```

```python
import functools

import numpy as np
import jax
import jax.numpy as jnp
from jax import lax
from jax.experimental import pallas as pl
from jax.experimental.pallas import tpu as pltpu

F32 = jnp.float32
BF16 = jnp.bfloat16

D_MODEL = 1024
N_META = 16
CHUNK = 64
EPS = 1e-6
D_SSM = 2048
SSM_HEADS = 32
SSM_HEAD_DIM = 64
SSM_GROUPS = 8
D_STATE = 128
CONV_W = 4
D_CONV = D_SSM + 2 * SSM_GROUPS * D_STATE
ATTN_HEADS = 16
ATTN_HEAD_DIM = 64
D_ATTN = ATTN_HEADS * ATTN_HEAD_DIM
D_FF = 2816
HEAD_PAIRS = ATTN_HEADS // 2
LANES = 128
GROUP_W = D_SSM // SSM_GROUPS

OFF_XBC = D_SSM
OFF_DT = OFF_XBC + D_CONV
OFF_Q = OFF_DT + SSM_HEADS
OFF_K = OFF_Q + D_ATTN
OFF_V = OFF_K + D_ATTN
OFF_F = OFF_V + D_ATTN
OFF_GS = OFF_F + ATTN_HEADS
OFF_GA = OFF_GS + D_MODEL

P_XBC = 0
P_Z = D_CONV
P_Q = P_Z + D_SSM
P_K = P_Q + D_ATTN
P_V = P_K + D_ATTN
P_GS = P_V + D_ATTN
P_GA = P_GS + D_MODEL
P_TOTAL = P_GA + D_MODEL
S_DT = 0
S_F = SSM_HEADS

VMEM_LIMIT = 56 * 1024 * 1024

PROJ_COL_TILES = 4
PP_GS = P_Q
PP_GA = PP_GS + D_MODEL
ATT_T = 512
CUMSUM_BLK = 256


def _dot(a, b):
    return jnp.dot(a, b, preferred_element_type=F32)


def _dot_nt(a, b):
    return lax.dot_general(a, b, (((1,), (1,)), ((), ())), preferred_element_type=F32)


def _dot_tn(a, b):
    return lax.dot_general(a, b, (((0,), (0,)), ((), ())), preferred_element_type=F32)


def _split3(x):
    hi = x.astype(BF16)
    r = x - hi.astype(F32)
    mid = r.astype(BF16)
    lo = (r - mid.astype(F32)).astype(BF16)
    return hi, mid, lo


def _rmsnorm(x, g):
    return x * lax.rsqrt(jnp.mean(x * x, axis=-1, keepdims=True) + EPS) * g


def _softplus(x):
    return jnp.maximum(x, 0.0) + jnp.log1p(jnp.exp(-jnp.abs(x)))


def _params(*sem):
    return pltpu.CompilerParams(dimension_semantics=sem, vmem_limit_bytes=VMEM_LIMIT)


FFN_HALVES = 2


def _ffn_kernel(x_ref, g1_ref, wi_ref, wo_ref, g2_ref, *out_refs, emit_h):
    h_ref, u_ref = out_refs if emit_h else (None,) + out_refs
    rows = x_ref.shape[0] // FFN_HALVES
    for s in range(FFN_HALVES):
        rs = slice(s * rows, (s + 1) * rows)
        x = x_ref[rs, :]
        xn = _rmsnorm(x, g1_ref[...]).astype(BF16)
        a = _dot(xn, wi_ref[:, :D_FF])
        b = _dot(xn, wi_ref[:, D_FF:])
        g = (a * jax.nn.sigmoid(a) * b).astype(BF16)
        h = x + 0.5 * _dot(g, wo_ref[...])
        if emit_h:
            h_ref[rs, :] = h
        u_ref[rs, :] = _rmsnorm(h, g2_ref[...]).astype(u_ref.dtype)


def _ffn(x, g1, w_in, w_out, g2, *, tm, emit_h, u_dtype):
    rows = x.shape[0]
    row_spec = pl.BlockSpec((tm, D_MODEL), lambda i: (i, 0))
    vec_spec = pl.BlockSpec((1, D_MODEL), lambda i: (0, 0))
    resident = lambda shape: pl.BlockSpec(shape, lambda i: (0, 0), pipeline_mode=pl.Buffered(1))
    out_shape = [jax.ShapeDtypeStruct((rows, D_MODEL), u_dtype)]
    out_specs = [row_spec]
    if emit_h:
        out_shape = [jax.ShapeDtypeStruct((rows, D_MODEL), F32)] + out_shape
        out_specs = [row_spec] + out_specs
    return pl.pallas_call(
        functools.partial(_ffn_kernel, emit_h=emit_h),
        grid=(rows // tm,),
        in_specs=[row_spec, vec_spec, resident((D_MODEL, 2 * D_FF)), resident((D_FF, D_MODEL)), vec_spec],
        out_specs=out_specs,
        out_shape=out_shape,
        compiler_params=_params("parallel"),
        name="ffn",
    )(x, g1, w_in, w_out, g2)


def _proj_kernel(u_ref, w_ref, o_ref):
    o_ref[...] = _dot(u_ref[...], w_ref[...])


def _proj(u, w, *, tm):
    rows = u.shape[0]
    width = w.shape[1]
    tn = width // PROJ_COL_TILES
    return pl.pallas_call(
        _proj_kernel,
        grid=(PROJ_COL_TILES, rows // tm),
        in_specs=[pl.BlockSpec((tm, D_MODEL), lambda j, i: (i, 0)),
                  pl.BlockSpec((D_MODEL, tn), lambda j, i: (0, j))],
        out_specs=pl.BlockSpec((tm, tn), lambda j, i: (i, j)),
        out_shape=jax.ShapeDtypeStruct((rows, width), F32),
        compiler_params=_params("parallel", "parallel"),
        name="proj",
    )(u, w)


def _small_kernel(u_ref, w_ref, b_ref, o_ref):
    x = _dot(u_ref[...], w_ref[...]) + b_ref[...]
    lane = lax.broadcasted_iota(jnp.int32, (1, LANES), 1)
    sgn = jnp.where(lane < S_F, 1.0, -1.0)
    o_ref[...] = sgn * _softplus(sgn * x)


def _small(u, w, b, *, tm):
    rows = u.shape[0]
    return pl.pallas_call(
        _small_kernel,
        grid=(rows // tm,),
        in_specs=[pl.BlockSpec((tm, D_MODEL), lambda i: (i, 0)),
                  pl.BlockSpec((D_MODEL, LANES), lambda i: (0, 0)),
                  pl.BlockSpec((1, LANES), lambda i: (0, 0))],
        out_specs=pl.BlockSpec((tm, LANES), lambda i: (i, 0)),
        out_shape=jax.ShapeDtypeStruct((rows, LANES), F32),
        compiler_params=_params("parallel"),
        name="small",
    )(u, w, b)


HIST0 = 8 - (CONV_W - 1)


def _ssd_kernel(xbc_ref, z_ref, dt_ref, s0_ref, h0_ref, cw_ref, cb_ref, alog_ref, dskip_ref, norm_ref,
                e_ref, o_ref, sT_ref, hT_ref, st_sc, xp_sc, y_sc, *, n_pad):
    T = CHUNK
    c = pl.program_id(1)

    @pl.when(c == 0)
    def _():
        st_sc[...] = s0_ref[...]
        xp_sc[HIST0:8, :] = h0_ref[...]

    xraw = xbc_ref[...]
    xp_sc[8:8 + T, :] = xraw
    cw = cw_ref[...]
    xp = xp_sc[...]
    conv = cb_ref[...]
    for j in range(CONV_W - 1):
        conv = conv + cw[j:j + 1] * pltpu.roll(xp, CONV_W - 1 - j, axis=0)[8:8 + T, :]
    conv = conv + cw[3:4] * xraw
    xp_sc[HIST0:8, :] = xraw[T - (CONV_W - 1):T, :]
    xc = conv * jax.nn.sigmoid(conv)
    dtv = dt_ref[:, S_DT:S_DT + SSM_HEADS]
    if n_pad:
        valid = lax.broadcasted_iota(jnp.int32, (T, 1), 0) >= n_pad
        xc = jnp.where(valid, xc, 0.0)
        dtv = jnp.where(valid, dtv, 0.0)

    a = -jnp.exp(alog_ref[...])
    adt = a * dtv
    row = lax.broadcasted_iota(jnp.int32, (T, T), 0)
    col = lax.broadcasted_iota(jnp.int32, (T, T), 1)
    causal = col <= row
    tril = jnp.where(causal, 1.0, 0.0).astype(BF16)
    r32 = lax.broadcasted_iota(jnp.int32, (SSM_HEADS, SSM_HEADS), 0)
    c32 = lax.broadcasted_iota(jnp.int32, (SSM_HEADS, SSM_HEADS), 1)
    eye = jnp.where(r32 == c32, 1.0, 0.0).astype(BF16)
    expand3 = e_ref[...]

    def expand(v):
        return _dot(jnp.concatenate(_split3(v), axis=1), expand3)

    a_cs = sum(_dot(tril, piece) for piece in _split3(adt))
    a_cs_t = sum(_dot_nt(eye, piece) for piece in _split3(a_cs))
    a_last = a_cs[T - 1:T, :]
    dt_e = expand(dtv)
    wdec_e = expand(jnp.exp(a_last - a_cs) * dtv)
    eacs = expand(jnp.exp(a_cs))
    cdec = eacs[T - 1:T, :]

    xs = xc[:, :D_SSM]
    bm = xc[:, D_SSM:D_SSM + SSM_GROUPS * D_STATE].astype(BF16)
    cm = xc[:, D_SSM + SSM_GROUPS * D_STATE:].astype(BF16)
    xd = xs * dt_e
    xdw = (xs * wdec_e).astype(BF16)
    lane = lax.broadcasted_iota(jnp.int32, (1, D_SSM), 1)
    lo = (lane & SSM_HEAD_DIM) == 0
    xd_lo = jnp.where(lo, xd, 0.0).astype(BF16)
    xd_hi = jnp.where(lo, 0.0, xd).astype(BF16)
    lo2 = lax.broadcasted_iota(jnp.int32, (1, LANES), 1) < SSM_HEAD_DIM
    step2 = lax.broadcasted_iota(jnp.int32, (T, LANES), 1) & (SSM_HEAD_DIM - 1)
    causal2 = step2 <= lax.broadcasted_iota(jnp.int32, (T, LANES), 0)

    for g in range(SSM_GROUPS):
        gs = slice(g * GROUP_W, (g + 1) * GROUP_W)
        cmg = cm[:, g * D_STATE:(g + 1) * D_STATE]
        bmg = bm[:, g * D_STATE:(g + 1) * D_STATE]
        cb2 = _dot_nt(cmg, jnp.concatenate([bmg, bmg], axis=0))
        stg = st_sc[:, gs]
        yoff = _dot(cmg, stg.astype(BF16))
        for rr in range(2):
            pair = g * 2 + rr
            ps = slice(pair * LANES, (pair + 1) * LANES)
            ha, hb = 2 * pair, 2 * pair + 1
            seg = (jnp.where(lo2, a_cs[:, ha:ha + 1], a_cs[:, hb:hb + 1])
                   - jnp.concatenate([a_cs_t[ha:ha + 1, :], a_cs_t[hb:hb + 1, :]], axis=1))
            m2 = (cb2 * jnp.exp(jnp.where(causal2, seg, -jnp.inf))).astype(BF16)
            ydiag = _dot(m2, jnp.concatenate([xd_lo[:, ps], xd_hi[:, ps]], axis=0))
            y_sc[:, ps] = ydiag + yoff[:, rr * LANES:(rr + 1) * LANES] * eacs[:, ps]
        st_sc[:, gs] = stg * cdec[:, gs] + _dot_tn(bmg, xdw[:, gs])

    y = y_sc[...] + dskip_ref[...] * xs
    zz = z_ref[...]
    y = y * (zz * jax.nn.sigmoid(zz))
    for g in range(SSM_GROUPS):
        gs = slice(g * GROUP_W, (g + 1) * GROUP_W)
        yg = y[:, gs]
        yg = yg * lax.rsqrt(jnp.mean(yg * yg, axis=-1, keepdims=True) + EPS)
        o_ref[:, gs] = (yg * norm_ref[:, gs]).astype(o_ref.dtype)

    @pl.when(c == pl.num_programs(1) - 1)
    def _():
        sT_ref[...] = st_sc[...]
        hT_ref[...] = xp_sc[HIST0:8, :]


def _ssd(proj3, small3, s0, h0, cw, cb, alog, dskip, norm, expand, *, nb, b_off, shared_init, n_pad):
    nc = proj3.shape[1] // CHUNK
    init = (lambda b, c: (0, 0, 0)) if shared_init else (lambda b, c: (b, 0, 0))
    const = lambda b, c: (0, 0)
    return pl.pallas_call(
        functools.partial(_ssd_kernel, n_pad=n_pad),
        grid=(nb, nc),
        in_specs=[
            pl.BlockSpec((None, CHUNK, D_CONV), lambda b, c: (b + b_off, c, P_XBC // D_CONV)),
            pl.BlockSpec((None, CHUNK, D_SSM), lambda b, c: (b + b_off, c, P_Z // D_SSM)),
            pl.BlockSpec((None, CHUNK, LANES), lambda b, c: (b + b_off, c, 0)),
            pl.BlockSpec((None, D_STATE, D_SSM), init),
            pl.BlockSpec((None, CONV_W - 1, D_CONV), init),
            pl.BlockSpec((CONV_W, D_CONV), const),
            pl.BlockSpec((1, D_CONV), const),
            pl.BlockSpec((1, SSM_HEADS), const),
            pl.BlockSpec((1, D_SSM), const),
            pl.BlockSpec((1, D_SSM), const),
            pl.BlockSpec((3 * SSM_HEADS, D_SSM), const),
        ],
        out_specs=[
            pl.BlockSpec((None, CHUNK, D_SSM), lambda b, c: (b, c, 0)),
            pl.BlockSpec((None, D_STATE, D_SSM), lambda b, c: (b, 0, 0)),
            pl.BlockSpec((None, CONV_W - 1, D_CONV), lambda b, c: (b, 0, 0)),
        ],
        out_shape=[
            jax.ShapeDtypeStruct((nb, nc * CHUNK, D_SSM), BF16),
            jax.ShapeDtypeStruct((nb, D_STATE, D_SSM), F32),
            jax.ShapeDtypeStruct((nb, CONV_W - 1, D_CONV), F32),
        ],
        scratch_shapes=[
            pltpu.VMEM((D_STATE, D_SSM), F32),
            pltpu.VMEM((8 + CHUNK, D_CONV), F32),
            pltpu.VMEM((CHUNK, D_SSM), F32),
        ],
        compiler_params=_params("parallel", "arbitrary"),
        name="ssd",
    )(proj3, proj3, small3, s0, h0, cw, cb, alog, dskip, norm, expand)


def _cumsum_kernel(x_ref, o_ref):
    n = x_ref.shape[-1] // CUMSUM_BLK
    r = lax.broadcasted_iota(jnp.int32, (CUMSUM_BLK, CUMSUM_BLK), 0)
    c = lax.broadcasted_iota(jnp.int32, (CUMSUM_BLK, CUMSUM_BLK), 1)
    upper = jnp.where(r <= c, 1.0, 0.0).astype(BF16)
    carry = jnp.zeros((ATTN_HEADS, 1), F32)
    for i in range(n):
        blk = slice(i * CUMSUM_BLK, (i + 1) * CUMSUM_BLK)
        cs = sum(_dot(piece, upper) for piece in _split3(x_ref[:, blk])) + carry
        o_ref[:, blk] = cs
        carry = cs[:, CUMSUM_BLK - 1:CUMSUM_BLK]


def _cumsum_t(x_t):
    b, h, l = x_t.shape
    spec = pl.BlockSpec((None, h, l), lambda i: (i, 0, 0))
    return pl.pallas_call(
        _cumsum_kernel, grid=(b,), in_specs=[spec], out_specs=spec,
        out_shape=jax.ShapeDtypeStruct(x_t.shape, F32),
        compiler_params=_params("parallel"), name="cumsum",
    )(x_t)


def _cum_logf(lf_t):
    l = lf_t.shape[2]
    lp = -(-l // CUMSUM_BLK) * CUMSUM_BLK
    return _cumsum_t(jnp.pad(lf_t, ((0, 0), (0, 0), (0, lp - l))))[:, :, :l]


BIAS_PIECES = 3
LOG2E = 1.4426950408889634


def _bias_lane0(head):
    return ATTN_HEAD_DIM if head % 2 == 0 else 0


def _bias_placement():
    sq = np.zeros((BIAS_PIECES, ATTN_HEADS, ATTN_HEADS * LANES), np.float32)
    sk = np.zeros_like(sq)
    one_q = np.zeros((1, ATTN_HEADS * LANES), np.float32)
    one_k = np.zeros_like(one_q)
    for h in range(ATTN_HEADS):
        base = h * LANES + _bias_lane0(h)
        for piece in range(BIAS_PIECES):
            sk[piece, h, base + piece] = -1.0
            one_q[0, base + piece] = 1.0
            sq[piece, h, base + BIAS_PIECES + piece] = 1.0
            one_k[0, base + BIAS_PIECES + piece] = 1.0
    flat = lambda a: jnp.asarray(a.reshape(BIAS_PIECES * ATTN_HEADS, ATTN_HEADS * LANES), BF16)
    return flat(sq), flat(sk), jnp.asarray(one_q), jnp.asarray(one_k)


def _attn_prep_kernel(q_ref, k_ref, v_ref, c_ref, sq_ref, sk_ref, oneq_ref, onek_ref, qo_ref, ko_ref, kt_ref,
                      vt_ref):
    _prep_rows(q_ref[...], k_ref[...], v_ref[...], c_ref[...], sq_ref, sk_ref, oneq_ref, onek_ref,
               qo_ref, ko_ref, kt_ref, vt_ref, slice(None))


def _prep_rows(q, k, v, c, sq_ref, sk_ref, oneq_ref, onek_ref, qo_ref, ko_ref, kt_ref, vt_ref, rs):
    pieces = jnp.concatenate(_split3(c * LOG2E), axis=1)
    bias_q = _dot(pieces, sq_ref[...]) + oneq_ref[...]
    bias_k = _dot(pieces, sk_ref[...]) + onek_ref[...]
    lane = lax.broadcasted_iota(jnp.int32, (1, LANES), 1)
    lo = lane < ATTN_HEAD_DIM
    for p in range(HEAD_PAIRS):
        ps = slice(p * LANES, (p + 1) * LANES)
        q2 = q[:, ps] * (ATTN_HEAD_DIM ** -0.5 * LOG2E)
        k2 = k[:, ps]
        for hh in range(2):
            h = 2 * p + hh
            hs = slice(h * LANES, (h + 1) * LANES)
            sel = lo if hh == 0 else jnp.logical_not(lo)
            qo_ref[h, rs, :] = (jnp.where(sel, q2, 0.0) + bias_q[:, hs]).astype(BF16)
            ko_ref[h, rs, :] = (jnp.where(sel, k2, 0.0) + bias_k[:, hs]).astype(BF16)
        for src, dst in ((k2, kt_ref), (v[:, ps], vt_ref)):
            x_t = jnp.transpose(src)
            dst[2 * p, :, rs] = x_t[:ATTN_HEAD_DIM]
            dst[2 * p + 1, :, rs] = x_t[ATTN_HEAD_DIM:]


QKV_HALVES = 2


def _qkv_prep_kernel(u_ref, w_ref, c_ref, sq_ref, sk_ref, oneq_ref, onek_ref, qo_ref, ko_ref, kt_ref, vt_ref):
    rows = u_ref.shape[0] // QKV_HALVES
    for s in range(QKV_HALVES):
        rs = slice(s * rows, (s + 1) * rows)
        qkv = _dot(u_ref[rs, :], w_ref[...])
        q, k, v = (qkv[:, n * D_ATTN:(n + 1) * D_ATTN] for n in range(3))
        _prep_rows(q, k, v, c_ref[rs, :], sq_ref, sk_ref, oneq_ref, onek_ref, qo_ref, ko_ref, kt_ref, vt_ref, rs)


def _qkv_prep(u, w_qkv, c_rows, consts, *, tile):
    nb, l, _ = c_rows.shape
    nt = l // tile
    sq, sk, one_q, one_k = consts
    c2 = lambda b, i: (0, 0)
    return pl.pallas_call(
        _qkv_prep_kernel,
        grid=(nb, nt),
        in_specs=[
            pl.BlockSpec((tile, D_MODEL), lambda b, i: (b * nt + i, 0)),
            pl.BlockSpec(w_qkv.shape, c2, pipeline_mode=pl.Buffered(1)),
            pl.BlockSpec((None, tile, ATTN_HEADS), lambda b, i: (b, i, 0)),
            pl.BlockSpec(sq.shape, c2), pl.BlockSpec(sk.shape, c2),
            pl.BlockSpec(one_q.shape, c2), pl.BlockSpec(one_k.shape, c2),
        ],
        out_specs=[
            pl.BlockSpec((None, ATTN_HEADS, tile, LANES), lambda b, i: (b, 0, i, 0)),
            pl.BlockSpec((None, ATTN_HEADS, tile, LANES), lambda b, i: (b, 0, i, 0)),
            pl.BlockSpec((None, ATTN_HEADS, ATTN_HEAD_DIM, tile), lambda b, i: (b, 0, 0, i)),
            pl.BlockSpec((None, ATTN_HEADS, ATTN_HEAD_DIM, tile), lambda b, i: (b, 0, 0, i)),
        ],
        out_shape=[
            jax.ShapeDtypeStruct((nb, ATTN_HEADS, l, LANES), BF16),
            jax.ShapeDtypeStruct((nb, ATTN_HEADS, l, LANES), BF16),
            jax.ShapeDtypeStruct((nb, ATTN_HEADS, ATTN_HEAD_DIM, l), F32),
            jax.ShapeDtypeStruct((nb, ATTN_HEADS, ATTN_HEAD_DIM, l), F32),
        ],
        compiler_params=_params("parallel", "parallel"),
        name="qkv_prep",
    )(u, w_qkv, c_rows, sq, sk, one_q, one_k)


def _attn_prep(proj3, c_rows, consts, *, nb, b_off, tile):
    l = c_rows.shape[1]
    col = lambda off: off // D_ATTN
    sq, sk, one_q, one_k = consts
    c2 = lambda b, i: (0, 0)
    return pl.pallas_call(
        _attn_prep_kernel,
        grid=(nb, l // tile),
        in_specs=[
            pl.BlockSpec((None, tile, D_ATTN), lambda b, i: (b + b_off, i, col(P_Q))),
            pl.BlockSpec((None, tile, D_ATTN), lambda b, i: (b + b_off, i, col(P_K))),
            pl.BlockSpec((None, tile, D_ATTN), lambda b, i: (b + b_off, i, col(P_V))),
            pl.BlockSpec((None, tile, ATTN_HEADS), lambda b, i: (b, i, 0)),
            pl.BlockSpec(sq.shape, c2), pl.BlockSpec(sk.shape, c2),
            pl.BlockSpec(one_q.shape, c2), pl.BlockSpec(one_k.shape, c2),
        ],
        out_specs=[
            pl.BlockSpec((None, ATTN_HEADS, tile, LANES), lambda b, i: (b, 0, i, 0)),
            pl.BlockSpec((None, ATTN_HEADS, tile, LANES), lambda b, i: (b, 0, i, 0)),
            pl.BlockSpec((None, ATTN_HEADS, ATTN_HEAD_DIM, tile), lambda b, i: (b, 0, 0, i)),
            pl.BlockSpec((None, ATTN_HEADS, ATTN_HEAD_DIM, tile), lambda b, i: (b, 0, 0, i)),
        ],
        out_shape=[
            jax.ShapeDtypeStruct((nb, ATTN_HEADS, l, LANES), BF16),
            jax.ShapeDtypeStruct((nb, ATTN_HEADS, l, LANES), BF16),
            jax.ShapeDtypeStruct((nb, ATTN_HEADS, ATTN_HEAD_DIM, l), F32),
            jax.ShapeDtypeStruct((nb, ATTN_HEADS, ATTN_HEAD_DIM, l), F32),
        ],
        compiler_params=_params("parallel", "parallel"),
        name="attn_prep",
    )(proj3, proj3, proj3, c_rows, sq, sk, one_q, one_k)


QBLK = 256
ACC_ROWS = ATTN_HEAD_DIM + 16


def _attn_prompt_kernel(it_ref, jt_ref, q_ref, k_ref, vt_ref, km_ref, vtm_ref, o_ref, m_sc, acc_sc,
                        *, meta_lo, meta_hi):
    t = pl.program_id(1)
    i = it_ref[t]
    j = jt_ref[t]

    def v_rows(v_t):
        return jnp.concatenate([v_t.astype(BF16), jnp.ones((ACC_ROWS - ATTN_HEAD_DIM, v_t.shape[1]), BF16)], axis=0)

    def softmax_stage(h, scores, first):
        m_old_all = None if first else m_sc[h]
        out = []
        for r, (s_t, v_aug) in enumerate(scores):
            qblk = s_t.shape[1]
            mx = jnp.max(s_t, axis=0, keepdims=True)
            if first:
                m_new, alpha = mx, None
            else:
                m_old = m_old_all[:, r * qblk:(r + 1) * qblk]
                m_new = jnp.maximum(m_old, mx)
                alpha = jnp.exp2(m_old - m_new)
            out.append((jnp.exp2(s_t - m_new).astype(BF16), v_aug, alpha, m_new))
        m_sc[h] = jnp.concatenate([o[3] for o in out], axis=1)
        return out

    def pv_stage(h, probs, first):
        acc_old_all = None if first else acc_sc[h]
        acc_out = []
        for r, (pr, v_aug, alpha, _) in enumerate(probs):
            qblk = pr.shape[1]
            pv = _dot(v_aug, pr)
            acc_out.append(pv if first else acc_old_all[:, r * qblk:(r + 1) * qblk] * alpha + pv)
        acc_sc[h] = jnp.concatenate(acc_out, axis=1)

    def run(score_fn, first, qblk):
        scores = lambda h: [score_fn(h, r, q_ref[h, r * qblk:(r + 1) * qblk, :]) for r in range(ATT_T // qblk)]
        s_next = scores(0)
        p_cur = softmax_stage(0, s_next, first)
        s_next = scores(1)
        for h in range(ATTN_HEADS):
            s_cur = s_next
            if h + 2 < ATTN_HEADS:
                s_next = scores(h + 2)
            p_prev = p_cur
            if h + 1 < ATTN_HEADS:
                p_cur = softmax_stage(h + 1, s_cur, first)
            pv_stage(h, p_prev, first)

    @pl.when(j == 0)
    def _():
        krow = lax.broadcasted_iota(jnp.int32, (km_ref.shape[1], 1), 0)
        is_meta = jnp.logical_and(krow >= meta_lo, krow < meta_hi)
        run(lambda h, r, q: (jnp.where(is_meta, _dot_nt(km_ref[h], q), -jnp.inf), v_rows(vtm_ref[h])), True, ATT_T)

    @pl.when(j < i)
    def _():
        run(lambda h, r, q: (_dot_nt(k_ref[h], q), v_rows(vt_ref[h])), False, ATT_T)

    @pl.when(j == i)
    def _():
        def diag_scores(h, r, q):
            nk = (r + 1) * QBLK
            s_t = _dot_nt(k_ref[h, :nk, :], q)
            krow = lax.broadcasted_iota(jnp.int32, (nk, QBLK), 0)
            qcol = lax.broadcasted_iota(jnp.int32, (nk, QBLK), 1) + r * QBLK
            return jnp.where(krow <= qcol, s_t, -jnp.inf), v_rows(vt_ref[h, :, :nk])

        run(diag_scores, False, QBLK)
        d = ATTN_HEAD_DIM
        for p in range(HEAD_PAIRS):
            o_t = jnp.concatenate([acc_sc[h, :d, :] / acc_sc[h, d:d + 1, :] for h in (2 * p, 2 * p + 1)], axis=0)
            o_ref[:, p * LANES:(p + 1) * LANES] = jnp.transpose(o_t).astype(o_ref.dtype)


def _attn_prompt(q_aug, k_aug, v_t, km_aug, vm_t, *, meta_lo):
    nb, _, l, _ = q_aug.shape
    nt = l // ATT_T
    it = np.array([i for i in range(nt) for _ in range(i + 1)], np.int32)
    jt = np.array([j for i in range(nt) for j in range(i + 1)], np.int32)
    mrows = km_aug.shape[2]
    grid_spec = pltpu.PrefetchScalarGridSpec(
        num_scalar_prefetch=2,
        grid=(nb, len(it)),
        in_specs=[
            pl.BlockSpec((None, ATTN_HEADS, ATT_T, LANES), lambda b, t, it, jt: (b, 0, it[t], 0)),
            pl.BlockSpec((None, ATTN_HEADS, ATT_T, LANES), lambda b, t, it, jt: (b, 0, jt[t], 0)),
            pl.BlockSpec((None, ATTN_HEADS, ATTN_HEAD_DIM, ATT_T), lambda b, t, it, jt: (b, 0, 0, jt[t])),
            pl.BlockSpec((None, ATTN_HEADS, mrows, LANES), lambda b, t, it, jt: (0, 0, 0, 0)),
            pl.BlockSpec((None, ATTN_HEADS, ATTN_HEAD_DIM, mrows), lambda b, t, it, jt: (0, 0, 0, 0)),
        ],
        out_specs=pl.BlockSpec((None, ATT_T, D_ATTN), lambda b, t, it, jt: (b, it[t], 0)),
        scratch_shapes=[
            pltpu.VMEM((ATTN_HEADS, 1, ATT_T), F32),
            pltpu.VMEM((ATTN_HEADS, ACC_ROWS, ATT_T), F32),
        ],
    )
    return pl.pallas_call(
        functools.partial(_attn_prompt_kernel, meta_lo=meta_lo, meta_hi=meta_lo + N_META),
        grid_spec=grid_spec,
        out_shape=jax.ShapeDtypeStruct((nb, l, D_ATTN), BF16),
        compiler_params=_params("parallel", "arbitrary"),
        name="attn_prompt",
    )(jnp.asarray(it), jnp.asarray(jt), q_aug, k_aug, v_t, km_aug, vm_t)


def _attn_sample_kernel(q_ref, kn_ref, vn_ref, km_ref, vm_ref, kct_ref, vct_ref, cq_ref, ckm_ref, ckc_ref,
                        ckn_ref, o_ref):
    T = CHUNK
    D = ATTN_HEAD_DIM
    row = lax.broadcasted_iota(jnp.int32, (T, T), 0)
    col = lax.broadcasted_iota(jnp.int32, (T, T), 1)
    outs = []
    for hh in range(2):
        hs = slice(hh * D, (hh + 1) * D)
        qh = (q_ref[:, hs] * (D ** -0.5)).astype(BF16)
        cq = cq_ref[hh]
        s_m = _dot_nt(qh, km_ref[:, hs].astype(BF16)) + cq - ckm_ref[hh:hh + 1, :]
        s_c = _dot(qh, kct_ref[hh].astype(BF16)) + cq - ckc_ref[hh:hh + 1, :]
        s_n = _dot_nt(qh, kn_ref[:, hs].astype(BF16)) + cq - ckn_ref[hh:hh + 1, :]
        s_n = jnp.where(col <= row, s_n, -jnp.inf)
        m = jnp.maximum(jnp.maximum(jnp.max(s_m, axis=1, keepdims=True), jnp.max(s_c, axis=1, keepdims=True)),
                        jnp.max(s_n, axis=1, keepdims=True))
        p_m = jnp.exp(s_m - m)
        p_c = jnp.exp(s_c - m)
        p_n = jnp.exp(s_n - m)
        l = (jnp.sum(p_m, axis=1, keepdims=True) + jnp.sum(p_c, axis=1, keepdims=True)
             + jnp.sum(p_n, axis=1, keepdims=True))
        o = (_dot(p_m.astype(BF16), vm_ref[:, hs].astype(BF16)) + _dot_nt(p_c.astype(BF16), vct_ref[hh].astype(BF16))
             + _dot(p_n.astype(BF16), vn_ref[:, hs].astype(BF16)))
        outs.append(o / l)
    o_ref[...] = jnp.concatenate(outs, axis=1).astype(o_ref.dtype)


def _attn_sample(proj3, proj2, meta_blk, cache_k, cache_v, cq, ckm_t, ckc_t, ckn_t, *, nb):
    past = cache_k.shape[3]
    lb = lambda off: off // LANES
    pair = lambda off: (lambda b, p: (b, 0, lb(off) + p))
    ckspec = lambda n: pl.BlockSpec((None, 2, n), lambda b, p: (b * HEAD_PAIRS + p, 0, 0))
    return pl.pallas_call(
        _attn_sample_kernel,
        grid=(nb, HEAD_PAIRS),
        in_specs=[
            pl.BlockSpec((None, CHUNK, LANES), pair(P_Q)),
            pl.BlockSpec((None, CHUNK, LANES), pair(P_K)),
            pl.BlockSpec((None, CHUNK, LANES), pair(P_V)),
            pl.BlockSpec((N_META, LANES), lambda b, p: (meta_blk, lb(P_K) + p)),
            pl.BlockSpec((N_META, LANES), lambda b, p: (meta_blk, lb(P_V) + p)),
            pl.BlockSpec((None, 2, ATTN_HEAD_DIM, past), lambda b, p: (b, p, 0, 0)),
            pl.BlockSpec((None, 2, ATTN_HEAD_DIM, past), lambda b, p: (b, p, 0, 0)),
            pl.BlockSpec((None, 2, CHUNK, 1), lambda b, p: (b, p, 0, 0)),
            ckspec(N_META), ckspec(past), ckspec(CHUNK),
        ],
        out_specs=pl.BlockSpec((None, CHUNK, LANES), lambda b, p: (b, 0, p)),
        out_shape=jax.ShapeDtypeStruct((nb, CHUNK, D_ATTN), BF16),
        compiler_params=_params("parallel", "parallel"),
        name="attn_sample",
    )(proj3, proj3, proj3, proj2, proj2, cache_k, cache_v, cq, ckm_t, ckc_t, ckn_t)


def _merge_kernel(os_ref, oa_ref, gs_ref, ga_ref, h_ref, wbs_ref, wba_ref, wo_ref, o_ref):
    bs = _dot(os_ref[...], wbs_ref[...])
    ba = _dot(oa_ref[...], wba_ref[...])
    merged = jax.nn.sigmoid(gs_ref[...]) * bs + jax.nn.sigmoid(ga_ref[...]) * ba
    o_ref[...] = h_ref[...] + _dot(merged.astype(BF16), wo_ref[...])


def _merge(o_ssm, o_attn, proj, h, w_bs, w_ba, w_o, *, tm, gs_off, ga_off):
    rows = o_ssm.shape[0]
    const = lambda i: (0, 0)
    return pl.pallas_call(
        _merge_kernel,
        grid=(rows // tm,),
        in_specs=[
            pl.BlockSpec((tm, D_SSM), lambda i: (i, 0)),
            pl.BlockSpec((tm, D_ATTN), lambda i: (i, 0)),
            pl.BlockSpec((tm, D_MODEL), lambda i: (i, gs_off // D_MODEL)),
            pl.BlockSpec((tm, D_MODEL), lambda i: (i, ga_off // D_MODEL)),
            pl.BlockSpec((tm, D_MODEL), lambda i: (i, 0)),
            pl.BlockSpec((D_SSM, D_MODEL), const),
            pl.BlockSpec((D_ATTN, D_MODEL), const),
            pl.BlockSpec((D_MODEL, D_MODEL), const),
        ],
        out_specs=pl.BlockSpec((tm, D_MODEL), lambda i: (i, 0)),
        out_shape=jax.ShapeDtypeStruct((rows, D_MODEL), F32),
        compiler_params=_params("parallel"),
        name="merge",
    )(o_ssm, o_attn, proj, proj, h, w_bs, w_ba, w_o)


def kernel(x_prompt, x_sample, cache_k, cache_v, cache_logf, state_ssm, state_conv, meta_tokens, norm_ffn1, ffn1_w_in, ffn1_w_out, norm_mix, w_in, conv_w, conv_b, dt_bias, a_log, d_skip, f_bias, ssm_norm, w_br_ssm, w_br_attn, w_out, norm_ffn2, ffn2_w_in, ffn2_w_out, norm_final):
    assert norm_ffn1.shape[0] == 1, "single-layer trunk"
    bp, seq, _ = x_prompt.shape
    bs, dseq, _ = x_sample.shape
    assert dseq == CHUNK and seq % ATT_T == 0
    n_s = bs * dseq
    rows_small = n_s + 2 * CHUNK
    meta_row0 = n_s + CHUNK - N_META
    meta_blk = meta_row0 // N_META

    row = lambda v: v.reshape(1, -1).astype(F32)
    w = w_in[0]
    w_main = jnp.concatenate([w[:, OFF_XBC:OFF_DT], w[:, :OFF_XBC], w[:, OFF_Q:OFF_F], w[:, OFF_GS:]],
                             axis=1).astype(BF16)
    pad_small = LANES - SSM_HEADS - ATTN_HEADS
    w_small = jnp.concatenate([w[:, OFF_DT:OFF_Q], w[:, OFF_F:OFF_GS], jnp.zeros((D_MODEL, pad_small), F32)],
                              axis=1).astype(BF16)
    b_small = jnp.concatenate([dt_bias[0], f_bias[0], jnp.zeros((pad_small,), F32)]).reshape(1, LANES)
    w1i, w1o = ffn1_w_in[0].astype(BF16), ffn1_w_out[0].astype(BF16)
    w2i, w2o = ffn2_w_in[0].astype(BF16), ffn2_w_out[0].astype(BF16)
    w_bs, w_ba, w_o = w_br_ssm[0].astype(BF16), w_br_attn[0].astype(BF16), w_out[0].astype(BF16)
    expand = jnp.asarray(np.tile(np.repeat(np.eye(SSM_HEADS, dtype=np.float32), SSM_HEAD_DIM, axis=1), (3, 1)),
                         BF16)
    ssd_consts = (conv_w[0], row(conv_b[0]), row(a_log[0]), row(jnp.repeat(d_skip[0], SSM_HEAD_DIM)),
                  row(ssm_norm[0]), expand)

    def front(x, w_proj, tm_ffn, tm_proj):
        h1, u = _ffn(x, row(norm_ffn1[0]), w1i, w1o, row(norm_mix[0]), tm=tm_ffn, emit_h=True, u_dtype=BF16)
        return h1, u, _proj(u, w_proj, tm=tm_proj), _small(u, w_small, b_small, tm=tm_proj)

    w_qkv = w_main[:, P_Q:P_GS]
    w_prompt = jnp.concatenate([w_main[:, :P_Q], w_main[:, P_GS:]], axis=1)
    xp = x_prompt.reshape(bp * seq, D_MODEL)
    h1_p, u_p, pm_p, ps_p = front(xp, w_prompt, 512, 1024)
    x_small = jnp.concatenate([x_sample.reshape(n_s, D_MODEL), jnp.zeros((CHUNK - N_META, D_MODEL), F32),
                               meta_tokens, jnp.zeros((CHUNK, D_MODEL), F32)], axis=0)
    h1_s, _, pm_s, ps_s = front(x_small, w_main, rows_small // 2, rows_small // 2)

    pm_p3 = pm_p.reshape(bp, seq, w_prompt.shape[1])
    ps_p3 = ps_p.reshape(bp, seq, LANES)
    pm_s3 = pm_s.reshape(rows_small // CHUNK, CHUNK, P_TOTAL)
    ps_s3 = ps_s.reshape(rows_small // CHUNK, CHUNK, LANES)

    zero_state = jnp.zeros((1, D_STATE, D_SSM), F32)
    zero_hist = jnp.zeros((1, CONV_W - 1, D_CONV), F32)
    _, st_m, hist_m = _ssd(pm_s3, ps_s3, zero_state, zero_hist, *ssd_consts,
                           nb=1, b_off=bs, shared_init=True, n_pad=CHUNK - N_META)
    o_ssm_p, st_p, hist_p = _ssd(pm_p3, ps_p3, st_m, hist_m, *ssd_consts,
                                 nb=bp, b_off=0, shared_init=True, n_pad=0)
    st_s0 = jnp.transpose(state_ssm[0].reshape(bs, D_SSM, D_STATE), (0, 2, 1))
    o_ssm_s, st_s, hist_s = _ssd(pm_s3, ps_s3, st_s0, state_conv[0], *ssd_consts,
                                 nb=bs, b_off=0, shared_init=False, n_pad=0)

    t_last = lambda a: jnp.swapaxes(a, -1, -2)
    lf_m = t_last(ps_s[meta_row0:meta_row0 + N_META, S_F:S_F + ATTN_HEADS])
    lf_p = t_last(ps_p3[:, :, S_F:S_F + ATTN_HEADS])
    lf_s = ps_s[:n_s, S_F:S_F + ATTN_HEADS].reshape(bs, dseq, ATTN_HEADS)
    lf_all_p = jnp.concatenate([jnp.broadcast_to(lf_m[None], (bp, ATTN_HEADS, N_META)), lf_p], axis=2)
    lf_all_s = jnp.concatenate([jnp.broadcast_to(lf_m[None], (bs, ATTN_HEADS, N_META)),
                                t_last(cache_logf[0].astype(F32)), t_last(lf_s)], axis=2)
    past = cache_logf.shape[2]
    c_p = _cum_logf(lf_all_p)
    c_s = _cum_logf(lf_all_s)

    prep_consts = _bias_placement()
    q_aug, k_aug, k_t, v_t = _qkv_prep(u_p, w_qkv, t_last(c_p[:, :, N_META:]), prep_consts, tile=ATT_T)
    meta_tile = 2 * CHUNK
    meta_lo = meta_row0 - n_s
    c_meta = jnp.pad(t_last(c_p[:1, :, :N_META]), ((0, 0), (meta_lo, meta_tile - meta_lo - N_META), (0, 0)))
    _, km_aug, km_t, vm_t = _attn_prep(pm_s.reshape(rows_small // meta_tile, meta_tile, P_TOTAL), c_meta,
                                       prep_consts, nb=1, b_off=n_s // meta_tile, tile=meta_tile)
    o_attn_p = _attn_prompt(q_aug, k_aug, v_t, km_aug, vm_t, meta_lo=meta_lo)
    per_pair = lambda c: c.reshape(bs * HEAD_PAIRS, 2, c.shape[2])
    feat_major = lambda a: jnp.transpose(a, (0, 2, 3, 1))
    o_attn_s = _attn_sample(pm_s3, pm_s, meta_blk, feat_major(cache_k[0]), feat_major(cache_v[0]),
                            c_s[:, :, N_META + past:, None],
                            per_pair(c_s[:, :, :N_META]), per_pair(c_s[:, :, N_META:N_META + past]),
                            per_pair(c_s[:, :, N_META + past:]), nb=bs)

    def back(o_ssm, o_attn, pm, h1, tm, gs_off, ga_off):
        h2 = _merge(o_ssm, o_attn, pm, h1, w_bs, w_ba, w_o, tm=tm, gs_off=gs_off, ga_off=ga_off)
        return _ffn(h2, row(norm_ffn2[0]), w2i, w2o, row(norm_final), tm=tm, emit_h=False, u_dtype=F32)[0]

    y_prompt = back(o_ssm_p.reshape(bp * seq, D_SSM), o_attn_p.reshape(bp * seq, D_ATTN), pm_p, h1_p, 512,
                    PP_GS, PP_GA)
    y_sample = back(o_ssm_s.reshape(n_s, D_SSM), o_attn_s.reshape(n_s, D_ATTN), pm_s, h1_s, 512, P_GS, P_GA)

    heads = lambda a: a.reshape(1, a.shape[0], a.shape[1], ATTN_HEADS, ATTN_HEAD_DIM)

    def with_meta(x_t, m_t):
        m_t = jnp.broadcast_to(m_t[:, :, :, meta_lo:meta_lo + N_META], (bp, ATTN_HEADS, ATTN_HEAD_DIM, N_META))
        return jnp.transpose(jnp.concatenate([m_t, x_t], axis=3), (0, 3, 1, 2))[None]

    k_prompt = with_meta(k_t, km_t)
    v_prompt = with_meta(v_t, vm_t)
    logf_prompt = t_last(lf_all_p)
    k_sample = heads(pm_s[:n_s, P_K:P_K + D_ATTN].reshape(bs, dseq, D_ATTN))
    v_sample = heads(pm_s[:n_s, P_V:P_V + D_ATTN].reshape(bs, dseq, D_ATTN))
    state_out = lambda st: jnp.transpose(st, (0, 2, 1)).reshape(1, st.shape[0], SSM_HEADS, SSM_HEAD_DIM, D_STATE)
    return (y_prompt.reshape(bp, seq, D_MODEL), y_sample.reshape(bs, dseq, D_MODEL),
            k_prompt, v_prompt, logf_prompt[None], state_out(st_p), hist_p[None],
            k_sample, v_sample, lf_s[None], state_out(st_s), hist_s[None])
```

```python
import functools

import numpy as np
import jax
import jax.numpy as jnp
from jax import lax
from jax.experimental import pallas as pl
from jax.experimental.pallas import tpu as pltpu

F32 = jnp.float32
BF16 = jnp.bfloat16

D_MODEL = 1024
N_META = 16
CHUNK = 64
EPS = 1e-6
D_SSM = 2048
SSM_HEADS = 32
SSM_HEAD_DIM = 64
SSM_GROUPS = 8
D_STATE = 128
CONV_W = 4
D_CONV = D_SSM + 2 * SSM_GROUPS * D_STATE
ATTN_HEADS = 16
ATTN_HEAD_DIM = 64
D_ATTN = ATTN_HEADS * ATTN_HEAD_DIM
D_FF = 2816
HEAD_PAIRS = ATTN_HEADS // 2
LANES = 128
GROUP_W = D_SSM // SSM_GROUPS

OFF_XBC = D_SSM
OFF_DT = OFF_XBC + D_CONV
OFF_Q = OFF_DT + SSM_HEADS
OFF_K = OFF_Q + D_ATTN
OFF_V = OFF_K + D_ATTN
OFF_F = OFF_V + D_ATTN
OFF_GS = OFF_F + ATTN_HEADS
OFF_GA = OFF_GS + D_MODEL

P_XBC = 0
P_Z = D_CONV
P_Q = P_Z + D_SSM
P_K = P_Q + D_ATTN
P_V = P_K + D_ATTN
P_GS = P_V + D_ATTN
P_GA = P_GS + D_MODEL
P_TOTAL = P_GA + D_MODEL
S_DT = 0
S_F = SSM_HEADS

VMEM_LIMIT = 56 * 1024 * 1024

PROJ_COL_TILES = 4
PP_GS = P_Q
PP_GA = PP_GS + D_MODEL
ATT_T = 512
CUMSUM_BLK = 256


def _dot(a, b):
    return jnp.dot(a, b, preferred_element_type=F32)


def _dot_nt(a, b):
    return lax.dot_general(a, b, (((1,), (1,)), ((), ())), preferred_element_type=F32)


def _dot_tn(a, b):
    return lax.dot_general(a, b, (((0,), (0,)), ((), ())), preferred_element_type=F32)


def _split3(x):
    hi = x.astype(BF16)
    r = x - hi.astype(F32)
    mid = r.astype(BF16)
    lo = (r - mid.astype(F32)).astype(BF16)
    return hi, mid, lo


def _rmsnorm(x, g):
    return x * lax.rsqrt(jnp.mean(x * x, axis=-1, keepdims=True) + EPS) * g


def _softplus(x):
    return jnp.maximum(x, 0.0) + jnp.log1p(jnp.exp(-jnp.abs(x)))


def _params(*sem):
    return pltpu.CompilerParams(dimension_semantics=sem, vmem_limit_bytes=VMEM_LIMIT)


FFN_HALVES = 2


def _ffn_kernel(x_ref, g1_ref, wi_ref, wo_ref, g2_ref, *out_refs, emit_h):
    h_ref, u_ref = out_refs if emit_h else (None,) + out_refs
    rows = x_ref.shape[0] // FFN_HALVES
    for s in range(FFN_HALVES):
        rs = slice(s * rows, (s + 1) * rows)
        x = x_ref[rs, :]
        xn = _rmsnorm(x, g1_ref[...]).astype(BF16)
        a = _dot(xn, wi_ref[:, :D_FF])
        b = _dot(xn, wi_ref[:, D_FF:])
        g = (a * jax.nn.sigmoid(a) * b).astype(BF16)
        h = x + 0.5 * _dot(g, wo_ref[...])
        if emit_h:
            h_ref[rs, :] = h
        u_ref[rs, :] = _rmsnorm(h, g2_ref[...]).astype(u_ref.dtype)


def _ffn(x, g1, w_in, w_out, g2, *, tm, emit_h, u_dtype):
    rows = x.shape[0]
    row_spec = pl.BlockSpec((tm, D_MODEL), lambda i: (i, 0))
    vec_spec = pl.BlockSpec((1, D_MODEL), lambda i: (0, 0))
    resident = lambda shape: pl.BlockSpec(shape, lambda i: (0, 0), pipeline_mode=pl.Buffered(1))
    out_shape = [jax.ShapeDtypeStruct((rows, D_MODEL), u_dtype)]
    out_specs = [row_spec]
    if emit_h:
        out_shape = [jax.ShapeDtypeStruct((rows, D_MODEL), F32)] + out_shape
        out_specs = [row_spec] + out_specs
    return pl.pallas_call(
        functools.partial(_ffn_kernel, emit_h=emit_h),
        grid=(rows // tm,),
        in_specs=[row_spec, vec_spec, resident((D_MODEL, 2 * D_FF)), resident((D_FF, D_MODEL)), vec_spec],
        out_specs=out_specs,
        out_shape=out_shape,
        compiler_params=_params("parallel"),
        name="ffn",
    )(x, g1, w_in, w_out, g2)


def _proj_kernel(u_ref, w_ref, o_ref):
    o_ref[...] = _dot(u_ref[...], w_ref[...])


def _proj(u, w, *, tm):
    rows = u.shape[0]
    width = w.shape[1]
    tn = width // PROJ_COL_TILES
    return pl.pallas_call(
        _proj_kernel,
        grid=(PROJ_COL_TILES, rows // tm),
        in_specs=[pl.BlockSpec((tm, D_MODEL), lambda j, i: (i, 0)),
                  pl.BlockSpec((D_MODEL, tn), lambda j, i: (0, j))],
        out_specs=pl.BlockSpec((tm, tn), lambda j, i: (i, j)),
        out_shape=jax.ShapeDtypeStruct((rows, width), F32),
        compiler_params=_params("parallel", "parallel"),
        name="proj",
    )(u, w)


def _small_kernel(u_ref, w_ref, b_ref, o_ref):
    x = _dot(u_ref[...], w_ref[...]) + b_ref[...]
    lane = lax.broadcasted_iota(jnp.int32, (1, LANES), 1)
    sgn = jnp.where(lane < S_F, 1.0, -1.0)
    o_ref[...] = sgn * _softplus(sgn * x)


def _small(u, w, b, *, tm):
    rows = u.shape[0]
    return pl.pallas_call(
        _small_kernel,
        grid=(rows // tm,),
        in_specs=[pl.BlockSpec((tm, D_MODEL), lambda i: (i, 0)),
                  pl.BlockSpec((D_MODEL, LANES), lambda i: (0, 0)),
                  pl.BlockSpec((1, LANES), lambda i: (0, 0))],
        out_specs=pl.BlockSpec((tm, LANES), lambda i: (i, 0)),
        out_shape=jax.ShapeDtypeStruct((rows, LANES), F32),
        compiler_params=_params("parallel"),
        name="small",
    )(u, w, b)


HIST0 = 8 - (CONV_W - 1)
SSD_CHUNKS_PER_STEP = 8


def _ssd_chunk(rs, xbc_ref, z_ref, dt_ref, cw_ref, cb_ref, alog_ref, dskip_ref, norm_ref, e_ref, o_ref,
               st_sc, xp_sc, y_sc, n_pad):
    T = CHUNK
    xraw = xbc_ref[rs, :]
    xp_sc[8:8 + T, :] = xraw
    cw = cw_ref[...]
    xp = xp_sc[...]
    conv = cb_ref[...]
    for j in range(CONV_W - 1):
        conv = conv + cw[j:j + 1] * pltpu.roll(xp, CONV_W - 1 - j, axis=0)[8:8 + T, :]
    conv = conv + cw[3:4] * xraw
    xp_sc[HIST0:8, :] = xraw[T - (CONV_W - 1):T, :]
    xc = conv * jax.nn.sigmoid(conv)
    dtv = dt_ref[rs, S_DT:S_DT + SSM_HEADS]
    if n_pad:
        valid = lax.broadcasted_iota(jnp.int32, (T, 1), 0) >= n_pad
        xc = jnp.where(valid, xc, 0.0)
        dtv = jnp.where(valid, dtv, 0.0)

    a = -jnp.exp(alog_ref[...])
    adt = a * dtv
    row = lax.broadcasted_iota(jnp.int32, (T, T), 0)
    col = lax.broadcasted_iota(jnp.int32, (T, T), 1)
    causal = col <= row
    tril = jnp.where(causal, 1.0, 0.0).astype(BF16)
    r32 = lax.broadcasted_iota(jnp.int32, (SSM_HEADS, SSM_HEADS), 0)
    c32 = lax.broadcasted_iota(jnp.int32, (SSM_HEADS, SSM_HEADS), 1)
    eye = jnp.where(r32 == c32, 1.0, 0.0).astype(BF16)
    expand3 = e_ref[...]

    def expand(v):
        return _dot(jnp.concatenate(_split3(v), axis=1), expand3)

    a_cs = sum(_dot(tril, piece) for piece in _split3(adt))
    a_cs_t = sum(_dot_nt(eye, piece) for piece in _split3(a_cs))
    a_last = a_cs[T - 1:T, :]
    dt_e = expand(dtv)
    wdec_e = expand(jnp.exp(a_last - a_cs) * dtv)
    eacs = expand(jnp.exp(a_cs))
    cdec = eacs[T - 1:T, :]

    xs = xc[:, :D_SSM]
    bm = xc[:, D_SSM:D_SSM + SSM_GROUPS * D_STATE].astype(BF16)
    cm = xc[:, D_SSM + SSM_GROUPS * D_STATE:].astype(BF16)
    xd = xs * dt_e
    xdw = (xs * wdec_e).astype(BF16)
    lane = lax.broadcasted_iota(jnp.int32, (1, D_SSM), 1)
    lo = (lane & SSM_HEAD_DIM) == 0
    xd_lo = jnp.where(lo, xd, 0.0).astype(BF16)
    xd_hi = jnp.where(lo, 0.0, xd).astype(BF16)
    lo2 = lax.broadcasted_iota(jnp.int32, (1, LANES), 1) < SSM_HEAD_DIM
    step2 = lax.broadcasted_iota(jnp.int32, (T, LANES), 1) & (SSM_HEAD_DIM - 1)
    causal2 = step2 <= lax.broadcasted_iota(jnp.int32, (T, LANES), 0)

    for g in range(SSM_GROUPS):
        gs = slice(g * GROUP_W, (g + 1) * GROUP_W)
        cmg = cm[:, g * D_STATE:(g + 1) * D_STATE]
        bmg = bm[:, g * D_STATE:(g + 1) * D_STATE]
        cb2 = _dot_nt(cmg, jnp.concatenate([bmg, bmg], axis=0))
        stg = st_sc[:, gs]
        yoff = _dot(cmg, stg.astype(BF16))
        for rr in range(2):
            pair = g * 2 + rr
            ps = slice(pair * LANES, (pair + 1) * LANES)
            ha, hb = 2 * pair, 2 * pair + 1
            seg = (jnp.where(lo2, a_cs[:, ha:ha + 1], a_cs[:, hb:hb + 1])
                   - jnp.concatenate([a_cs_t[ha:ha + 1, :], a_cs_t[hb:hb + 1, :]], axis=1))
            m2 = (cb2 * jnp.exp(jnp.where(causal2, seg, -jnp.inf))).astype(BF16)
            ydiag = _dot(m2, jnp.concatenate([xd_lo[:, ps], xd_hi[:, ps]], axis=0))
            y_sc[:, ps] = ydiag + yoff[:, rr * LANES:(rr + 1) * LANES] * eacs[:, ps]
        st_sc[:, gs] = stg * cdec[:, gs] + _dot_tn(bmg, xdw[:, gs])

    y = y_sc[...] + dskip_ref[...] * xs
    zz = z_ref[rs, :]
    y = y * (zz * jax.nn.sigmoid(zz))
    for g in range(SSM_GROUPS):
        gs = slice(g * GROUP_W, (g + 1) * GROUP_W)
        yg = y[:, gs]
        yg = yg * lax.rsqrt(jnp.mean(yg * yg, axis=-1, keepdims=True) + EPS)
        o_ref[rs, gs] = (yg * norm_ref[:, gs]).astype(o_ref.dtype)


def _ssd_kernel(xbc_ref, z_ref, dt_ref, s0_ref, h0_ref, cw_ref, cb_ref, alog_ref, dskip_ref, norm_ref,
                e_ref, o_ref, sT_ref, hT_ref, st_sc, xp_sc, y_sc, *, n_pad, cps):
    c = pl.program_id(1)

    @pl.when(c == 0)
    def _():
        st_sc[...] = s0_ref[...]
        xp_sc[HIST0:8, :] = h0_ref[...]

    for ci in range(cps):
        _ssd_chunk(slice(ci * CHUNK, (ci + 1) * CHUNK), xbc_ref, z_ref, dt_ref, cw_ref, cb_ref, alog_ref,
                   dskip_ref, norm_ref, e_ref, o_ref, st_sc, xp_sc, y_sc, n_pad)

    @pl.when(c == pl.num_programs(1) - 1)
    def _():
        sT_ref[...] = st_sc[...]
        hT_ref[...] = xp_sc[HIST0:8, :]


def _ssd(proj3, small3, s0, h0, cw, cb, alog, dskip, norm, expand, *, nb, b_off, shared_init, n_pad, cps):
    rows = cps * CHUNK
    nc = proj3.shape[1] // rows
    init = (lambda b, c: (0, 0, 0)) if shared_init else (lambda b, c: (b, 0, 0))
    const = lambda b, c: (0, 0)
    return pl.pallas_call(
        functools.partial(_ssd_kernel, n_pad=n_pad, cps=cps),
        grid=(nb, nc),
        in_specs=[
            pl.BlockSpec((None, rows, D_CONV), lambda b, c: (b + b_off, c, P_XBC // D_CONV)),
            pl.BlockSpec((None, rows, D_SSM), lambda b, c: (b + b_off, c, P_Z // D_SSM)),
            pl.BlockSpec((None, rows, LANES), lambda b, c: (b + b_off, c, 0)),
            pl.BlockSpec((None, D_STATE, D_SSM), init),
            pl.BlockSpec((None, CONV_W - 1, D_CONV), init),
            pl.BlockSpec((CONV_W, D_CONV), const),
            pl.BlockSpec((1, D_CONV), const),
            pl.BlockSpec((1, SSM_HEADS), const),
            pl.BlockSpec((1, D_SSM), const),
            pl.BlockSpec((1, D_SSM), const),
            pl.BlockSpec((3 * SSM_HEADS, D_SSM), const),
        ],
        out_specs=[
            pl.BlockSpec((None, rows, D_SSM), lambda b, c: (b, c, 0)),
            pl.BlockSpec((None, D_STATE, D_SSM), lambda b, c: (b, 0, 0)),
            pl.BlockSpec((None, CONV_W - 1, D_CONV), lambda b, c: (b, 0, 0)),
        ],
        out_shape=[
            jax.ShapeDtypeStruct((nb, nc * rows, D_SSM), BF16),
            jax.ShapeDtypeStruct((nb, D_STATE, D_SSM), F32),
            jax.ShapeDtypeStruct((nb, CONV_W - 1, D_CONV), F32),
        ],
        scratch_shapes=[
            pltpu.VMEM((D_STATE, D_SSM), F32),
            pltpu.VMEM((8 + CHUNK, D_CONV), F32),
            pltpu.VMEM((CHUNK, D_SSM), F32),
        ],
        compiler_params=_params("parallel", "arbitrary"),
        name="ssd",
    )(proj3, proj3, small3, s0, h0, cw, cb, alog, dskip, norm, expand)


def _cumsum_kernel(x_ref, o_ref):
    n = x_ref.shape[-1] // CUMSUM_BLK
    r = lax.broadcasted_iota(jnp.int32, (CUMSUM_BLK, CUMSUM_BLK), 0)
    c = lax.broadcasted_iota(jnp.int32, (CUMSUM_BLK, CUMSUM_BLK), 1)
    upper = jnp.where(r <= c, 1.0, 0.0).astype(BF16)
    carry = jnp.zeros((ATTN_HEADS, 1), F32)
    for i in range(n):
        blk = slice(i * CUMSUM_BLK, (i + 1) * CUMSUM_BLK)
        cs = sum(_dot(piece, upper) for piece in _split3(x_ref[:, blk])) + carry
        o_ref[:, blk] = cs
        carry = cs[:, CUMSUM_BLK - 1:CUMSUM_BLK]


def _cumsum_t(x_t):
    b, h, l = x_t.shape
    spec = pl.BlockSpec((None, h, l), lambda i: (i, 0, 0))
    return pl.pallas_call(
        _cumsum_kernel, grid=(b,), in_specs=[spec], out_specs=spec,
        out_shape=jax.ShapeDtypeStruct(x_t.shape, F32),
        compiler_params=_params("parallel"), name="cumsum",
    )(x_t)


def _cum_logf(lf_t):
    l = lf_t.shape[2]
    lp = -(-l // CUMSUM_BLK) * CUMSUM_BLK
    return _cumsum_t(jnp.pad(lf_t, ((0, 0), (0, 0), (0, lp - l))))[:, :, :l]


BIAS_PIECES = 3
LOG2E = 1.4426950408889634


def _bias_lane0(head):
    return ATTN_HEAD_DIM if head % 2 == 0 else 0


def _bias_placement():
    sq = np.zeros((BIAS_PIECES, ATTN_HEADS, ATTN_HEADS * LANES), np.float32)
    sk = np.zeros_like(sq)
    one_q = np.zeros((1, ATTN_HEADS * LANES), np.float32)
    one_k = np.zeros_like(one_q)
    for h in range(ATTN_HEADS):
        base = h * LANES + _bias_lane0(h)
        for piece in range(BIAS_PIECES):
            sk[piece, h, base + piece] = -1.0
            one_q[0, base + piece] = 1.0
            sq[piece, h, base + BIAS_PIECES + piece] = 1.0
            one_k[0, base + BIAS_PIECES + piece] = 1.0
    flat = lambda a: jnp.asarray(a.reshape(BIAS_PIECES * ATTN_HEADS, ATTN_HEADS * LANES), BF16)
    return flat(sq), flat(sk), jnp.asarray(one_q), jnp.asarray(one_k)


def _attn_prep_kernel(q_ref, k_ref, v_ref, c_ref, sq_ref, sk_ref, oneq_ref, onek_ref, qo_ref, ko_ref, kt_ref,
                      vt_ref):
    _prep_rows(q_ref[...], k_ref[...], v_ref[...], c_ref[...], sq_ref, sk_ref, oneq_ref, onek_ref,
               qo_ref, ko_ref, kt_ref, vt_ref, slice(None))


def _prep_rows(q, k, v, c, sq_ref, sk_ref, oneq_ref, onek_ref, qo_ref, ko_ref, kt_ref, vt_ref, rs):
    pieces = jnp.concatenate(_split3(c * LOG2E), axis=1)
    bias_q = _dot(pieces, sq_ref[...]) + oneq_ref[...]
    bias_k = _dot(pieces, sk_ref[...]) + onek_ref[...]
    lane = lax.broadcasted_iota(jnp.int32, (1, LANES), 1)
    lo = lane < ATTN_HEAD_DIM
    for p in range(HEAD_PAIRS):
        ps = slice(p * LANES, (p + 1) * LANES)
        q2 = q[:, ps] * (ATTN_HEAD_DIM ** -0.5 * LOG2E)
        k2 = k[:, ps]
        for hh in range(2):
            h = 2 * p + hh
            hs = slice(h * LANES, (h + 1) * LANES)
            sel = lo if hh == 0 else jnp.logical_not(lo)
            qo_ref[h, rs, :] = (jnp.where(sel, q2, 0.0) + bias_q[:, hs]).astype(BF16)
            ko_ref[h, rs, :] = (jnp.where(sel, k2, 0.0) + bias_k[:, hs]).astype(BF16)
        for src, dst in ((k2, kt_ref), (v[:, ps], vt_ref)):
            x_t = jnp.transpose(src)
            dst[2 * p, :, rs] = x_t[:ATTN_HEAD_DIM]
            dst[2 * p + 1, :, rs] = x_t[ATTN_HEAD_DIM:]


QKV_HALVES = 2


def _qkv_prep_kernel(u_ref, w_ref, c_ref, sq_ref, sk_ref, oneq_ref, onek_ref, qo_ref, ko_ref, kt_ref, vt_ref):
    rows = u_ref.shape[0] // QKV_HALVES
    for s in range(QKV_HALVES):
        rs = slice(s * rows, (s + 1) * rows)
        qkv = _dot(u_ref[rs, :], w_ref[...])
        q, k, v = (qkv[:, n * D_ATTN:(n + 1) * D_ATTN] for n in range(3))
        _prep_rows(q, k, v, c_ref[rs, :], sq_ref, sk_ref, oneq_ref, onek_ref, qo_ref, ko_ref, kt_ref, vt_ref, rs)


def _qkv_prep(u, w_qkv, c_rows, consts, *, tile):
    nb, l, _ = c_rows.shape
    nt = l // tile
    sq, sk, one_q, one_k = consts
    c2 = lambda b, i: (0, 0)
    return pl.pallas_call(
        _qkv_prep_kernel,
        grid=(nb, nt),
        in_specs=[
            pl.BlockSpec((tile, D_MODEL), lambda b, i: (b * nt + i, 0)),
            pl.BlockSpec(w_qkv.shape, c2, pipeline_mode=pl.Buffered(1)),
            pl.BlockSpec((None, tile, ATTN_HEADS), lambda b, i: (b, i, 0)),
            pl.BlockSpec(sq.shape, c2), pl.BlockSpec(sk.shape, c2),
            pl.BlockSpec(one_q.shape, c2), pl.BlockSpec(one_k.shape, c2),
        ],
        out_specs=[
            pl.BlockSpec((None, ATTN_HEADS, tile, LANES), lambda b, i: (b, 0, i, 0)),
            pl.BlockSpec((None, ATTN_HEADS, tile, LANES), lambda b, i: (b, 0, i, 0)),
            pl.BlockSpec((None, ATTN_HEADS, ATTN_HEAD_DIM, tile), lambda b, i: (b, 0, 0, i)),
            pl.BlockSpec((None, ATTN_HEADS, ATTN_HEAD_DIM, tile), lambda b, i: (b, 0, 0, i)),
        ],
        out_shape=[
            jax.ShapeDtypeStruct((nb, ATTN_HEADS, l, LANES), BF16),
            jax.ShapeDtypeStruct((nb, ATTN_HEADS, l, LANES), BF16),
            jax.ShapeDtypeStruct((nb, ATTN_HEADS, ATTN_HEAD_DIM, l), F32),
            jax.ShapeDtypeStruct((nb, ATTN_HEADS, ATTN_HEAD_DIM, l), F32),
        ],
        compiler_params=_params("parallel", "parallel"),
        name="qkv_prep",
    )(u, w_qkv, c_rows, sq, sk, one_q, one_k)


def _attn_prep(proj3, c_rows, consts, *, nb, b_off, tile):
    l = c_rows.shape[1]
    col = lambda off: off // D_ATTN
    sq, sk, one_q, one_k = consts
    c2 = lambda b, i: (0, 0)
    return pl.pallas_call(
        _attn_prep_kernel,
        grid=(nb, l // tile),
        in_specs=[
            pl.BlockSpec((None, tile, D_ATTN), lambda b, i: (b + b_off, i, col(P_Q))),
            pl.BlockSpec((None, tile, D_ATTN), lambda b, i: (b + b_off, i, col(P_K))),
            pl.BlockSpec((None, tile, D_ATTN), lambda b, i: (b + b_off, i, col(P_V))),
            pl.BlockSpec((None, tile, ATTN_HEADS), lambda b, i: (b, i, 0)),
            pl.BlockSpec(sq.shape, c2), pl.BlockSpec(sk.shape, c2),
            pl.BlockSpec(one_q.shape, c2), pl.BlockSpec(one_k.shape, c2),
        ],
        out_specs=[
            pl.BlockSpec((None, ATTN_HEADS, tile, LANES), lambda b, i: (b, 0, i, 0)),
            pl.BlockSpec((None, ATTN_HEADS, tile, LANES), lambda b, i: (b, 0, i, 0)),
            pl.BlockSpec((None, ATTN_HEADS, ATTN_HEAD_DIM, tile), lambda b, i: (b, 0, 0, i)),
            pl.BlockSpec((None, ATTN_HEADS, ATTN_HEAD_DIM, tile), lambda b, i: (b, 0, 0, i)),
        ],
        out_shape=[
            jax.ShapeDtypeStruct((nb, ATTN_HEADS, l, LANES), BF16),
            jax.ShapeDtypeStruct((nb, ATTN_HEADS, l, LANES), BF16),
            jax.ShapeDtypeStruct((nb, ATTN_HEADS, ATTN_HEAD_DIM, l), F32),
            jax.ShapeDtypeStruct((nb, ATTN_HEADS, ATTN_HEAD_DIM, l), F32),
        ],
        compiler_params=_params("parallel", "parallel"),
        name="attn_prep",
    )(proj3, proj3, proj3, c_rows, sq, sk, one_q, one_k)


QBLK = 256
ACC_ROWS = ATTN_HEAD_DIM + 16


def _attn_prompt_kernel(it_ref, jt_ref, q_ref, k_ref, vt_ref, km_ref, vtm_ref, o_ref, m_sc, acc_sc,
                        *, meta_lo, meta_hi):
    t = pl.program_id(1)
    i = it_ref[t]
    j = jt_ref[t]

    def v_rows(v_t):
        return jnp.concatenate([v_t.astype(BF16), jnp.ones((ACC_ROWS - ATTN_HEAD_DIM, v_t.shape[1]), BF16)], axis=0)

    def softmax_stage(h, scores, first):
        m_old_all = None if first else m_sc[h]
        out = []
        for r, (s_t, v_aug) in enumerate(scores):
            qblk = s_t.shape[1]
            mx = jnp.max(s_t, axis=0, keepdims=True)
            if first:
                m_new, alpha = mx, None
            else:
                m_old = m_old_all[:, r * qblk:(r + 1) * qblk]
                m_new = jnp.maximum(m_old, mx)
                alpha = jnp.exp2(m_old - m_new)
            out.append((jnp.exp2(s_t - m_new).astype(BF16), v_aug, alpha, m_new))
        m_sc[h] = jnp.concatenate([o[3] for o in out], axis=1)
        return out

    def pv_stage(h, probs, first):
        acc_old_all = None if first else acc_sc[h]
        acc_out = []
        for r, (pr, v_aug, alpha, _) in enumerate(probs):
            qblk = pr.shape[1]
            pv = _dot(v_aug, pr)
            acc_out.append(pv if first else acc_old_all[:, r * qblk:(r + 1) * qblk] * alpha + pv)
        acc_sc[h] = jnp.concatenate(acc_out, axis=1)

    def run(score_fn, first, qblk):
        scores = lambda h: [score_fn(h, r, q_ref[h, r * qblk:(r + 1) * qblk, :]) for r in range(ATT_T // qblk)]
        s_next = scores(0)
        p_cur = softmax_stage(0, s_next, first)
        s_next = scores(1)
        for h in range(ATTN_HEADS):
            s_cur = s_next
            if h + 2 < ATTN_HEADS:
                s_next = scores(h + 2)
            p_prev = p_cur
            if h + 1 < ATTN_HEADS:
                p_cur = softmax_stage(h + 1, s_cur, first)
            pv_stage(h, p_prev, first)

    @pl.when(j == 0)
    def _():
        krow = lax.broadcasted_iota(jnp.int32, (km_ref.shape[1], 1), 0)
        is_meta = jnp.logical_and(krow >= meta_lo, krow < meta_hi)
        run(lambda h, r, q: (jnp.where(is_meta, _dot_nt(km_ref[h], q), -jnp.inf), v_rows(vtm_ref[h])), True, ATT_T)

    @pl.when(j < i)
    def _():
        run(lambda h, r, q: (_dot_nt(k_ref[h], q), v_rows(vt_ref[h])), False, ATT_T)

    @pl.when(j == i)
    def _():
        def diag_scores(h, r, q):
            nk = (r + 1) * QBLK
            s_t = _dot_nt(k_ref[h, :nk, :], q)
            krow = lax.broadcasted_iota(jnp.int32, (nk, QBLK), 0)
            qcol = lax.broadcasted_iota(jnp.int32, (nk, QBLK), 1) + r * QBLK
            return jnp.where(krow <= qcol, s_t, -jnp.inf), v_rows(vt_ref[h, :, :nk])

        run(diag_scores, False, QBLK)
        d = ATTN_HEAD_DIM
        for p in range(HEAD_PAIRS):
            o_t = jnp.concatenate([acc_sc[h, :d, :] / acc_sc[h, d:d + 1, :] for h in (2 * p, 2 * p + 1)], axis=0)
            o_ref[:, p * LANES:(p + 1) * LANES] = jnp.transpose(o_t).astype(o_ref.dtype)


def _attn_prompt(q_aug, k_aug, v_t, km_aug, vm_t, *, meta_lo):
    nb, _, l, _ = q_aug.shape
    nt = l // ATT_T
    it = np.array([i for i in range(nt) for _ in range(i + 1)], np.int32)
    jt = np.array([j for i in range(nt) for j in range(i + 1)], np.int32)
    mrows = km_aug.shape[2]
    grid_spec = pltpu.PrefetchScalarGridSpec(
        num_scalar_prefetch=2,
        grid=(nb, len(it)),
        in_specs=[
            pl.BlockSpec((None, ATTN_HEADS, ATT_T, LANES), lambda b, t, it, jt: (b, 0, it[t], 0)),
            pl.BlockSpec((None, ATTN_HEADS, ATT_T, LANES), lambda b, t, it, jt: (b, 0, jt[t], 0)),
            pl.BlockSpec((None, ATTN_HEADS, ATTN_HEAD_DIM, ATT_T), lambda b, t, it, jt: (b, 0, 0, jt[t])),
            pl.BlockSpec((None, ATTN_HEADS, mrows, LANES), lambda b, t, it, jt: (0, 0, 0, 0)),
            pl.BlockSpec((None, ATTN_HEADS, ATTN_HEAD_DIM, mrows), lambda b, t, it, jt: (0, 0, 0, 0)),
        ],
        out_specs=pl.BlockSpec((None, ATT_T, D_ATTN), lambda b, t, it, jt: (b, it[t], 0)),
        scratch_shapes=[
            pltpu.VMEM((ATTN_HEADS, 1, ATT_T), F32),
            pltpu.VMEM((ATTN_HEADS, ACC_ROWS, ATT_T), F32),
        ],
    )
    return pl.pallas_call(
        functools.partial(_attn_prompt_kernel, meta_lo=meta_lo, meta_hi=meta_lo + N_META),
        grid_spec=grid_spec,
        out_shape=jax.ShapeDtypeStruct((nb, l, D_ATTN), BF16),
        compiler_params=_params("parallel", "arbitrary"),
        name="attn_prompt",
    )(jnp.asarray(it), jnp.asarray(jt), q_aug, k_aug, v_t, km_aug, vm_t)


def _attn_sample_kernel(q_ref, kn_ref, vn_ref, km_ref, vm_ref, kct_ref, vct_ref, cq_ref, ckm_ref, ckc_ref,
                        ckn_ref, o_ref):
    T = CHUNK
    D = ATTN_HEAD_DIM
    row = lax.broadcasted_iota(jnp.int32, (T, T), 0)
    col = lax.broadcasted_iota(jnp.int32, (T, T), 1)
    outs = []
    for hh in range(2):
        hs = slice(hh * D, (hh + 1) * D)
        qh = (q_ref[:, hs] * (D ** -0.5)).astype(BF16)
        cq = cq_ref[hh]
        s_m = _dot_nt(qh, km_ref[:, hs].astype(BF16)) + cq - ckm_ref[hh:hh + 1, :]
        s_c = _dot(qh, kct_ref[hh].astype(BF16)) + cq - ckc_ref[hh:hh + 1, :]
        s_n = _dot_nt(qh, kn_ref[:, hs].astype(BF16)) + cq - ckn_ref[hh:hh + 1, :]
        s_n = jnp.where(col <= row, s_n, -jnp.inf)
        m = jnp.maximum(jnp.maximum(jnp.max(s_m, axis=1, keepdims=True), jnp.max(s_c, axis=1, keepdims=True)),
                        jnp.max(s_n, axis=1, keepdims=True))
        p_m = jnp.exp(s_m - m)
        p_c = jnp.exp(s_c - m)
        p_n = jnp.exp(s_n - m)
        l = (jnp.sum(p_m, axis=1, keepdims=True) + jnp.sum(p_c, axis=1, keepdims=True)
             + jnp.sum(p_n, axis=1, keepdims=True))
        o = (_dot(p_m.astype(BF16), vm_ref[:, hs].astype(BF16)) + _dot_nt(p_c.astype(BF16), vct_ref[hh].astype(BF16))
             + _dot(p_n.astype(BF16), vn_ref[:, hs].astype(BF16)))
        outs.append(o / l)
    o_ref[...] = jnp.concatenate(outs, axis=1).astype(o_ref.dtype)


def _attn_sample(proj3, proj2, meta_blk, cache_k, cache_v, cq, ckm_t, ckc_t, ckn_t, *, nb):
    past = cache_k.shape[3]
    lb = lambda off: off // LANES
    pair = lambda off: (lambda b, p: (b, 0, lb(off) + p))
    ckspec = lambda n: pl.BlockSpec((None, 2, n), lambda b, p: (b * HEAD_PAIRS + p, 0, 0))
    return pl.pallas_call(
        _attn_sample_kernel,
        grid=(nb, HEAD_PAIRS),
        in_specs=[
            pl.BlockSpec((None, CHUNK, LANES), pair(P_Q)),
            pl.BlockSpec((None, CHUNK, LANES), pair(P_K)),
            pl.BlockSpec((None, CHUNK, LANES), pair(P_V)),
            pl.BlockSpec((N_META, LANES), lambda b, p: (meta_blk, lb(P_K) + p)),
            pl.BlockSpec((N_META, LANES), lambda b, p: (meta_blk, lb(P_V) + p)),
            pl.BlockSpec((None, 2, ATTN_HEAD_DIM, past), lambda b, p: (b, p, 0, 0)),
            pl.BlockSpec((None, 2, ATTN_HEAD_DIM, past), lambda b, p: (b, p, 0, 0)),
            pl.BlockSpec((None, 2, CHUNK, 1), lambda b, p: (b, p, 0, 0)),
            ckspec(N_META), ckspec(past), ckspec(CHUNK),
        ],
        out_specs=pl.BlockSpec((None, CHUNK, LANES), lambda b, p: (b, 0, p)),
        out_shape=jax.ShapeDtypeStruct((nb, CHUNK, D_ATTN), BF16),
        compiler_params=_params("parallel", "parallel"),
        name="attn_sample",
    )(proj3, proj3, proj3, proj2, proj2, cache_k, cache_v, cq, ckm_t, ckc_t, ckn_t)


def _merge_kernel(os_ref, oa_ref, gs_ref, ga_ref, h_ref, wbs_ref, wba_ref, wo_ref, o_ref):
    bs = _dot(os_ref[...], wbs_ref[...])
    ba = _dot(oa_ref[...], wba_ref[...])
    merged = jax.nn.sigmoid(gs_ref[...]) * bs + jax.nn.sigmoid(ga_ref[...]) * ba
    o_ref[...] = h_ref[...] + _dot(merged.astype(BF16), wo_ref[...])


def _merge(o_ssm, o_attn, proj, h, w_bs, w_ba, w_o, *, tm, gs_off, ga_off):
    rows = o_ssm.shape[0]
    const = lambda i: (0, 0)
    return pl.pallas_call(
        _merge_kernel,
        grid=(rows // tm,),
        in_specs=[
            pl.BlockSpec((tm, D_SSM), lambda i: (i, 0)),
            pl.BlockSpec((tm, D_ATTN), lambda i: (i, 0)),
            pl.BlockSpec((tm, D_MODEL), lambda i: (i, gs_off // D_MODEL)),
            pl.BlockSpec((tm, D_MODEL), lambda i: (i, ga_off // D_MODEL)),
            pl.BlockSpec((tm, D_MODEL), lambda i: (i, 0)),
            pl.BlockSpec((D_SSM, D_MODEL), const),
            pl.BlockSpec((D_ATTN, D_MODEL), const),
            pl.BlockSpec((D_MODEL, D_MODEL), const),
        ],
        out_specs=pl.BlockSpec((tm, D_MODEL), lambda i: (i, 0)),
        out_shape=jax.ShapeDtypeStruct((rows, D_MODEL), F32),
        compiler_params=_params("parallel"),
        name="merge",
    )(o_ssm, o_attn, proj, proj, h, w_bs, w_ba, w_o)


def kernel(x_prompt, x_sample, cache_k, cache_v, cache_logf, state_ssm, state_conv, meta_tokens, norm_ffn1, ffn1_w_in, ffn1_w_out, norm_mix, w_in, conv_w, conv_b, dt_bias, a_log, d_skip, f_bias, ssm_norm, w_br_ssm, w_br_attn, w_out, norm_ffn2, ffn2_w_in, ffn2_w_out, norm_final):
    assert norm_ffn1.shape[0] == 1, "single-layer trunk"
    bp, seq, _ = x_prompt.shape
    bs, dseq, _ = x_sample.shape
    assert dseq == CHUNK and seq % ATT_T == 0
    n_s = bs * dseq
    rows_small = n_s + 2 * CHUNK
    meta_row0 = n_s + CHUNK - N_META
    meta_blk = meta_row0 // N_META

    row = lambda v: v.reshape(1, -1).astype(F32)
    w = w_in[0]
    w_main = jnp.concatenate([w[:, OFF_XBC:OFF_DT], w[:, :OFF_XBC], w[:, OFF_Q:OFF_F], w[:, OFF_GS:]],
                             axis=1).astype(BF16)
    pad_small = LANES - SSM_HEADS - ATTN_HEADS
    w_small = jnp.concatenate([w[:, OFF_DT:OFF_Q], w[:, OFF_F:OFF_GS], jnp.zeros((D_MODEL, pad_small), F32)],
                              axis=1).astype(BF16)
    b_small = jnp.concatenate([dt_bias[0], f_bias[0], jnp.zeros((pad_small,), F32)]).reshape(1, LANES)
    w1i, w1o = ffn1_w_in[0].astype(BF16), ffn1_w_out[0].astype(BF16)
    w2i, w2o = ffn2_w_in[0].astype(BF16), ffn2_w_out[0].astype(BF16)
    w_bs, w_ba, w_o = w_br_ssm[0].astype(BF16), w_br_attn[0].astype(BF16), w_out[0].astype(BF16)
    expand = jnp.asarray(np.tile(np.repeat(np.eye(SSM_HEADS, dtype=np.float32), SSM_HEAD_DIM, axis=1), (3, 1)),
                         BF16)
    ssd_consts = (conv_w[0], row(conv_b[0]), row(a_log[0]), row(jnp.repeat(d_skip[0], SSM_HEAD_DIM)),
                  row(ssm_norm[0]), expand)

    def front(x, w_proj, tm_ffn, tm_proj):
        h1, u = _ffn(x, row(norm_ffn1[0]), w1i, w1o, row(norm_mix[0]), tm=tm_ffn, emit_h=True, u_dtype=BF16)
        return h1, u, _proj(u, w_proj, tm=tm_proj), _small(u, w_small, b_small, tm=tm_proj)

    w_qkv = w_main[:, P_Q:P_GS]
    w_prompt = jnp.concatenate([w_main[:, :P_Q], w_main[:, P_GS:]], axis=1)
    xp = x_prompt.reshape(bp * seq, D_MODEL)
    h1_p, u_p, pm_p, ps_p = front(xp, w_prompt, 512, 1024)
    x_small = jnp.concatenate([x_sample.reshape(n_s, D_MODEL), jnp.zeros((CHUNK - N_META, D_MODEL), F32),
                               meta_tokens, jnp.zeros((CHUNK, D_MODEL), F32)], axis=0)
    h1_s, _, pm_s, ps_s = front(x_small, w_main, rows_small // 2, rows_small // 2)

    pm_p3 = pm_p.reshape(bp, seq, w_prompt.shape[1])
    ps_p3 = ps_p.reshape(bp, seq, LANES)
    pm_s3 = pm_s.reshape(rows_small // CHUNK, CHUNK, P_TOTAL)
    ps_s3 = ps_s.reshape(rows_small // CHUNK, CHUNK, LANES)

    zero_state = jnp.zeros((1, D_STATE, D_SSM), F32)
    zero_hist = jnp.zeros((1, CONV_W - 1, D_CONV), F32)
    _, st_m, hist_m = _ssd(pm_s3, ps_s3, zero_state, zero_hist, *ssd_consts,
                           nb=1, b_off=bs, shared_init=True, n_pad=CHUNK - N_META, cps=1)
    o_ssm_p, st_p, hist_p = _ssd(pm_p3, ps_p3, st_m, hist_m, *ssd_consts,
                                 nb=bp, b_off=0, shared_init=True, n_pad=0, cps=SSD_CHUNKS_PER_STEP)
    st_s0 = jnp.transpose(state_ssm[0].reshape(bs, D_SSM, D_STATE), (0, 2, 1))
    o_ssm_s, st_s, hist_s = _ssd(pm_s3, ps_s3, st_s0, state_conv[0], *ssd_consts,
                                 nb=bs, b_off=0, shared_init=False, n_pad=0, cps=1)

    t_last = lambda a: jnp.swapaxes(a, -1, -2)
    lf_m = t_last(ps_s[meta_row0:meta_row0 + N_META, S_F:S_F + ATTN_HEADS])
    lf_p = t_last(ps_p3[:, :, S_F:S_F + ATTN_HEADS])
    lf_s = ps_s[:n_s, S_F:S_F + ATTN_HEADS].reshape(bs, dseq, ATTN_HEADS)
    lf_all_p = jnp.concatenate([jnp.broadcast_to(lf_m[None], (bp, ATTN_HEADS, N_META)), lf_p], axis=2)
    lf_all_s = jnp.concatenate([jnp.broadcast_to(lf_m[None], (bs, ATTN_HEADS, N_META)),
                                t_last(cache_logf[0].astype(F32)), t_last(lf_s)], axis=2)
    past = cache_logf.shape[2]
    c_p = _cum_logf(lf_all_p)
    c_s = _cum_logf(lf_all_s)

    prep_consts = _bias_placement()
    q_aug, k_aug, k_t, v_t = _qkv_prep(u_p, w_qkv, t_last(c_p[:, :, N_META:]), prep_consts, tile=ATT_T)
    meta_tile = 2 * CHUNK
    meta_lo = meta_row0 - n_s
    c_meta = jnp.pad(t_last(c_p[:1, :, :N_META]), ((0, 0), (meta_lo, meta_tile - meta_lo - N_META), (0, 0)))
    _, km_aug, km_t, vm_t = _attn_prep(pm_s.reshape(rows_small // meta_tile, meta_tile, P_TOTAL), c_meta,
                                       prep_consts, nb=1, b_off=n_s // meta_tile, tile=meta_tile)
    o_attn_p = _attn_prompt(q_aug, k_aug, v_t, km_aug, vm_t, meta_lo=meta_lo)
    per_pair = lambda c: c.reshape(bs * HEAD_PAIRS, 2, c.shape[2])
    feat_major = lambda a: jnp.transpose(a, (0, 2, 3, 1))
    o_attn_s = _attn_sample(pm_s3, pm_s, meta_blk, feat_major(cache_k[0]), feat_major(cache_v[0]),
                            c_s[:, :, N_META + past:, None],
                            per_pair(c_s[:, :, :N_META]), per_pair(c_s[:, :, N_META:N_META + past]),
                            per_pair(c_s[:, :, N_META + past:]), nb=bs)

    def back(o_ssm, o_attn, pm, h1, tm, gs_off, ga_off):
        h2 = _merge(o_ssm, o_attn, pm, h1, w_bs, w_ba, w_o, tm=tm, gs_off=gs_off, ga_off=ga_off)
        return _ffn(h2, row(norm_ffn2[0]), w2i, w2o, row(norm_final), tm=tm, emit_h=False, u_dtype=F32)[0]

    y_prompt = back(o_ssm_p.reshape(bp * seq, D_SSM), o_attn_p.reshape(bp * seq, D_ATTN), pm_p, h1_p, 512,
                    PP_GS, PP_GA)
    y_sample = back(o_ssm_s.reshape(n_s, D_SSM), o_attn_s.reshape(n_s, D_ATTN), pm_s, h1_s, 512, P_GS, P_GA)

    heads = lambda a: a.reshape(1, a.shape[0], a.shape[1], ATTN_HEADS, ATTN_HEAD_DIM)

    def with_meta(x_t, m_t):
        m_t = jnp.broadcast_to(m_t[:, :, :, meta_lo:meta_lo + N_META], (bp, ATTN_HEADS, ATTN_HEAD_DIM, N_META))
        return jnp.transpose(jnp.concatenate([m_t, x_t], axis=3), (0, 3, 1, 2))[None]

    k_prompt = with_meta(k_t, km_t)
    v_prompt = with_meta(v_t, vm_t)
    logf_prompt = t_last(lf_all_p)
    k_sample = heads(pm_s[:n_s, P_K:P_K + D_ATTN].reshape(bs, dseq, D_ATTN))
    v_sample = heads(pm_s[:n_s, P_V:P_V + D_ATTN].reshape(bs, dseq, D_ATTN))
    state_out = lambda st: jnp.transpose(st, (0, 2, 1)).reshape(1, st.shape[0], SSM_HEADS, SSM_HEAD_DIM, D_STATE)
    return (y_prompt.reshape(bp, seq, D_MODEL), y_sample.reshape(bs, dseq, D_MODEL),
            k_prompt, v_prompt, logf_prompt[None], state_out(st_p), hist_p[None],
            k_sample, v_sample, lf_s[None], state_out(st_s), hist_s[None])
```

```python
import functools

import numpy as np
import jax
import jax.numpy as jnp
from jax import lax
from jax.experimental import pallas as pl
from jax.experimental.pallas import tpu as pltpu

F32 = jnp.float32
BF16 = jnp.bfloat16

D_MODEL = 1024
N_META = 16
CHUNK = 64
EPS = 1e-6
D_SSM = 2048
SSM_HEADS = 32
SSM_HEAD_DIM = 64
SSM_GROUPS = 8
D_STATE = 128
CONV_W = 4
D_CONV = D_SSM + 2 * SSM_GROUPS * D_STATE
ATTN_HEADS = 16
ATTN_HEAD_DIM = 64
D_ATTN = ATTN_HEADS * ATTN_HEAD_DIM
D_FF = 2816
HEAD_PAIRS = ATTN_HEADS // 2
LANES = 128
GROUP_W = D_SSM // SSM_GROUPS

OFF_XBC = D_SSM
OFF_DT = OFF_XBC + D_CONV
OFF_Q = OFF_DT + SSM_HEADS
OFF_K = OFF_Q + D_ATTN
OFF_V = OFF_K + D_ATTN
OFF_F = OFF_V + D_ATTN
OFF_GS = OFF_F + ATTN_HEADS
OFF_GA = OFF_GS + D_MODEL

P_XBC = 0
P_Z = D_CONV
P_Q = P_Z + D_SSM
P_K = P_Q + D_ATTN
P_V = P_K + D_ATTN
P_GS = P_V + D_ATTN
P_GA = P_GS + D_MODEL
P_TOTAL = P_GA + D_MODEL
S_DT = 0
S_F = SSM_HEADS

VMEM_LIMIT = 56 * 1024 * 1024

PROJ_COL_TILES = 4
PP_GS = P_Q
PP_GA = PP_GS + D_MODEL
ATT_T = 512
CUMSUM_BLK = 256


def _dot(a, b):
    return jnp.dot(a, b, preferred_element_type=F32)


def _dot_nt(a, b):
    return lax.dot_general(a, b, (((1,), (1,)), ((), ())), preferred_element_type=F32)


def _dot_tn(a, b):
    return lax.dot_general(a, b, (((0,), (0,)), ((), ())), preferred_element_type=F32)


def _split3(x):
    hi = x.astype(BF16)
    r = x - hi.astype(F32)
    mid = r.astype(BF16)
    lo = (r - mid.astype(F32)).astype(BF16)
    return hi, mid, lo


def _rmsnorm(x, g):
    return x * lax.rsqrt(jnp.mean(x * x, axis=-1, keepdims=True) + EPS) * g


def _softplus(x):
    return jnp.maximum(x, 0.0) + jnp.log1p(jnp.exp(-jnp.abs(x)))


def _params(*sem):
    return pltpu.CompilerParams(dimension_semantics=sem, vmem_limit_bytes=VMEM_LIMIT)


FFN_HALVES = 2


def _ffn_kernel(x_ref, g1_ref, wi_ref, wo_ref, g2_ref, *out_refs, emit_h):
    h_ref, u_ref = out_refs if emit_h else (None,) + out_refs
    rows = x_ref.shape[0] // FFN_HALVES
    for s in range(FFN_HALVES):
        rs = slice(s * rows, (s + 1) * rows)
        x = x_ref[rs, :]
        xn = _rmsnorm(x, g1_ref[...]).astype(BF16)
        a = _dot(xn, wi_ref[:, :D_FF])
        b = _dot(xn, wi_ref[:, D_FF:])
        g = (a * jax.nn.sigmoid(a) * b).astype(BF16)
        h = x + 0.5 * _dot(g, wo_ref[...])
        if emit_h:
            h_ref[rs, :] = h
        u_ref[rs, :] = _rmsnorm(h, g2_ref[...]).astype(u_ref.dtype)


def _ffn(x, g1, w_in, w_out, g2, *, tm, emit_h, u_dtype):
    rows = x.shape[0]
    row_spec = pl.BlockSpec((tm, D_MODEL), lambda i: (i, 0))
    vec_spec = pl.BlockSpec((1, D_MODEL), lambda i: (0, 0))
    resident = lambda shape: pl.BlockSpec(shape, lambda i: (0, 0), pipeline_mode=pl.Buffered(1))
    out_shape = [jax.ShapeDtypeStruct((rows, D_MODEL), u_dtype)]
    out_specs = [row_spec]
    if emit_h:
        out_shape = [jax.ShapeDtypeStruct((rows, D_MODEL), F32)] + out_shape
        out_specs = [row_spec] + out_specs
    return pl.pallas_call(
        functools.partial(_ffn_kernel, emit_h=emit_h),
        grid=(rows // tm,),
        in_specs=[row_spec, vec_spec, resident((D_MODEL, 2 * D_FF)), resident((D_FF, D_MODEL)), vec_spec],
        out_specs=out_specs,
        out_shape=out_shape,
        compiler_params=_params("parallel"),
        name="ffn",
    )(x, g1, w_in, w_out, g2)


def _proj_kernel(u_ref, w_ref, o_ref):
    o_ref[...] = _dot(u_ref[...], w_ref[...])


def _proj(u, w, *, tm):
    rows = u.shape[0]
    width = w.shape[1]
    tn = width // PROJ_COL_TILES
    return pl.pallas_call(
        _proj_kernel,
        grid=(PROJ_COL_TILES, rows // tm),
        in_specs=[pl.BlockSpec((tm, D_MODEL), lambda j, i: (i, 0)),
                  pl.BlockSpec((D_MODEL, tn), lambda j, i: (0, j))],
        out_specs=pl.BlockSpec((tm, tn), lambda j, i: (i, j)),
        out_shape=jax.ShapeDtypeStruct((rows, width), F32),
        compiler_params=_params("parallel", "parallel"),
        name="proj",
    )(u, w)


def _small_kernel(u_ref, w_ref, b_ref, o_ref):
    x = _dot(u_ref[...], w_ref[...]) + b_ref[...]
    lane = lax.broadcasted_iota(jnp.int32, (1, LANES), 1)
    sgn = jnp.where(lane < S_F, 1.0, -1.0)
    o_ref[...] = sgn * _softplus(sgn * x)


def _small(u, w, b, *, tm):
    rows = u.shape[0]
    return pl.pallas_call(
        _small_kernel,
        grid=(rows // tm,),
        in_specs=[pl.BlockSpec((tm, D_MODEL), lambda i: (i, 0)),
                  pl.BlockSpec((D_MODEL, LANES), lambda i: (0, 0)),
                  pl.BlockSpec((1, LANES), lambda i: (0, 0))],
        out_specs=pl.BlockSpec((tm, LANES), lambda i: (i, 0)),
        out_shape=jax.ShapeDtypeStruct((rows, LANES), F32),
        compiler_params=_params("parallel"),
        name="small",
    )(u, w, b)


HIST0 = 8 - (CONV_W - 1)
SSD_CHUNKS_PER_STEP = 8


def _ssd_chunk(rs, xbc_ref, z_ref, dt_ref, cw_ref, cb_ref, alog_ref, dskip_ref, norm_ref, e_ref, o_ref,
               st_sc, xp_sc, y_sc, n_pad):
    T = CHUNK
    xraw = xbc_ref[rs, :]
    xp_sc[8:8 + T, :] = xraw
    cw = cw_ref[...]
    xp = xp_sc[...]
    conv = cb_ref[...]
    for j in range(CONV_W - 1):
        conv = conv + cw[j:j + 1] * pltpu.roll(xp, CONV_W - 1 - j, axis=0)[8:8 + T, :]
    conv = conv + cw[3:4] * xraw
    xp_sc[HIST0:8, :] = xraw[T - (CONV_W - 1):T, :]
    xc = conv * jax.nn.sigmoid(conv)
    dtv = dt_ref[rs, S_DT:S_DT + SSM_HEADS]
    if n_pad:
        valid = lax.broadcasted_iota(jnp.int32, (T, 1), 0) >= n_pad
        xc = jnp.where(valid, xc, 0.0)
        dtv = jnp.where(valid, dtv, 0.0)

    a = -jnp.exp(alog_ref[...])
    adt = a * dtv
    row = lax.broadcasted_iota(jnp.int32, (T, T), 0)
    col = lax.broadcasted_iota(jnp.int32, (T, T), 1)
    causal = col <= row
    tril = jnp.where(causal, 1.0, 0.0).astype(BF16)
    r32 = lax.broadcasted_iota(jnp.int32, (SSM_HEADS, SSM_HEADS), 0)
    c32 = lax.broadcasted_iota(jnp.int32, (SSM_HEADS, SSM_HEADS), 1)
    eye = jnp.where(r32 == c32, 1.0, 0.0).astype(BF16)
    expand3 = e_ref[...]

    def expand(v):
        return _dot(jnp.concatenate(_split3(v), axis=1), expand3)

    a_cs = sum(_dot(tril, piece) for piece in _split3(adt))
    a_cs_t = sum(_dot_nt(eye, piece) for piece in _split3(a_cs))
    a_last = a_cs[T - 1:T, :]
    dt_e = expand(dtv)
    wdec_e = expand(jnp.exp(a_last - a_cs) * dtv)
    eacs = expand(jnp.exp(a_cs))
    cdec = eacs[T - 1:T, :]

    xs = xc[:, :D_SSM]
    bm = xc[:, D_SSM:D_SSM + SSM_GROUPS * D_STATE].astype(BF16)
    cm = xc[:, D_SSM + SSM_GROUPS * D_STATE:].astype(BF16)
    xd = xs * dt_e
    xdw = (xs * wdec_e).astype(BF16)
    lane = lax.broadcasted_iota(jnp.int32, (1, D_SSM), 1)
    lo = (lane & SSM_HEAD_DIM) == 0
    xd_lo = jnp.where(lo, xd, 0.0).astype(BF16)
    xd_hi = jnp.where(lo, 0.0, xd).astype(BF16)
    lo2 = lax.broadcasted_iota(jnp.int32, (1, LANES), 1) < SSM_HEAD_DIM
    step2 = lax.broadcasted_iota(jnp.int32, (T, LANES), 1) & (SSM_HEAD_DIM - 1)
    causal2 = step2 <= lax.broadcasted_iota(jnp.int32, (T, LANES), 0)

    for g in range(SSM_GROUPS):
        gs = slice(g * GROUP_W, (g + 1) * GROUP_W)
        cmg = cm[:, g * D_STATE:(g + 1) * D_STATE]
        bmg = bm[:, g * D_STATE:(g + 1) * D_STATE]
        cb2 = _dot_nt(cmg, jnp.concatenate([bmg, bmg], axis=0))
        stg = st_sc[:, gs]
        yoff = _dot(cmg, stg.astype(BF16))
        for rr in range(2):
            pair = g * 2 + rr
            ps = slice(pair * LANES, (pair + 1) * LANES)
            ha, hb = 2 * pair, 2 * pair + 1
            seg = (jnp.where(lo2, a_cs[:, ha:ha + 1], a_cs[:, hb:hb + 1])
                   - jnp.concatenate([a_cs_t[ha:ha + 1, :], a_cs_t[hb:hb + 1, :]], axis=1))
            m2 = (cb2 * jnp.exp(jnp.where(causal2, seg, -jnp.inf))).astype(BF16)
            ydiag = _dot(m2, jnp.concatenate([xd_lo[:, ps], xd_hi[:, ps]], axis=0))
            y_sc[:, ps] = ydiag + yoff[:, rr * LANES:(rr + 1) * LANES] * eacs[:, ps]
        st_sc[:, gs] = stg * cdec[:, gs] + _dot_tn(bmg, xdw[:, gs])

    y = y_sc[...] + dskip_ref[...] * xs
    zz = z_ref[rs, :]
    y = y * (zz * jax.nn.sigmoid(zz))
    for g in range(SSM_GROUPS):
        gs = slice(g * GROUP_W, (g + 1) * GROUP_W)
        yg = y[:, gs]
        yg = yg * lax.rsqrt(jnp.mean(yg * yg, axis=-1, keepdims=True) + EPS)
        o_ref[rs, gs] = (yg * norm_ref[:, gs]).astype(o_ref.dtype)


def _ssd_kernel(xbc_ref, z_ref, dt_ref, s0_ref, h0_ref, cw_ref, cb_ref, alog_ref, dskip_ref, norm_ref,
                e_ref, o_ref, sT_ref, hT_ref, st_sc, xp_sc, y_sc, *, n_pad, cps):
    c = pl.program_id(1)

    @pl.when(c == 0)
    def _():
        st_sc[...] = s0_ref[...]
        xp_sc[HIST0:8, :] = h0_ref[...]

    for ci in range(cps):
        _ssd_chunk(slice(ci * CHUNK, (ci + 1) * CHUNK), xbc_ref, z_ref, dt_ref, cw_ref, cb_ref, alog_ref,
                   dskip_ref, norm_ref, e_ref, o_ref, st_sc, xp_sc, y_sc, n_pad)

    @pl.when(c == pl.num_programs(1) - 1)
    def _():
        sT_ref[...] = st_sc[...]
        hT_ref[...] = xp_sc[HIST0:8, :]


def _ssd(proj3, small3, s0, h0, cw, cb, alog, dskip, norm, expand, *, nb, b_off, shared_init, n_pad, cps):
    rows = cps * CHUNK
    nc = proj3.shape[1] // rows
    init = (lambda b, c: (0, 0, 0)) if shared_init else (lambda b, c: (b, 0, 0))
    const = lambda b, c: (0, 0)
    return pl.pallas_call(
        functools.partial(_ssd_kernel, n_pad=n_pad, cps=cps),
        grid=(nb, nc),
        in_specs=[
            pl.BlockSpec((None, rows, D_CONV), lambda b, c: (b + b_off, c, P_XBC // D_CONV)),
            pl.BlockSpec((None, rows, D_SSM), lambda b, c: (b + b_off, c, P_Z // D_SSM)),
            pl.BlockSpec((None, rows, LANES), lambda b, c: (b + b_off, c, 0)),
            pl.BlockSpec((None, D_STATE, D_SSM), init),
            pl.BlockSpec((None, CONV_W - 1, D_CONV), init),
            pl.BlockSpec((CONV_W, D_CONV), const),
            pl.BlockSpec((1, D_CONV), const),
            pl.BlockSpec((1, SSM_HEADS), const),
            pl.BlockSpec((1, D_SSM), const),
            pl.BlockSpec((1, D_SSM), const),
            pl.BlockSpec((3 * SSM_HEADS, D_SSM), const),
        ],
        out_specs=[
            pl.BlockSpec((None, rows, D_SSM), lambda b, c: (b, c, 0)),
            pl.BlockSpec((None, D_STATE, D_SSM), lambda b, c: (b, 0, 0)),
            pl.BlockSpec((None, CONV_W - 1, D_CONV), lambda b, c: (b, 0, 0)),
        ],
        out_shape=[
            jax.ShapeDtypeStruct((nb, nc * rows, D_SSM), BF16),
            jax.ShapeDtypeStruct((nb, D_STATE, D_SSM), F32),
            jax.ShapeDtypeStruct((nb, CONV_W - 1, D_CONV), F32),
        ],
        scratch_shapes=[
            pltpu.VMEM((D_STATE, D_SSM), F32),
            pltpu.VMEM((8 + CHUNK, D_CONV), F32),
            pltpu.VMEM((CHUNK, D_SSM), F32),
        ],
        compiler_params=_params("parallel", "arbitrary"),
        name="ssd",
    )(proj3, proj3, small3, s0, h0, cw, cb, alog, dskip, norm, expand)


def _cumsum_kernel(x_ref, o_ref):
    n = x_ref.shape[-1] // CUMSUM_BLK
    r = lax.broadcasted_iota(jnp.int32, (CUMSUM_BLK, CUMSUM_BLK), 0)
    c = lax.broadcasted_iota(jnp.int32, (CUMSUM_BLK, CUMSUM_BLK), 1)
    upper = jnp.where(r <= c, 1.0, 0.0).astype(BF16)
    carry = jnp.zeros((ATTN_HEADS, 1), F32)
    for i in range(n):
        blk = slice(i * CUMSUM_BLK, (i + 1) * CUMSUM_BLK)
        cs = sum(_dot(piece, upper) for piece in _split3(x_ref[:, blk])) + carry
        o_ref[:, blk] = cs
        carry = cs[:, CUMSUM_BLK - 1:CUMSUM_BLK]


def _cumsum_t(x_t):
    b, h, l = x_t.shape
    spec = pl.BlockSpec((None, h, l), lambda i: (i, 0, 0))
    return pl.pallas_call(
        _cumsum_kernel, grid=(b,), in_specs=[spec], out_specs=spec,
        out_shape=jax.ShapeDtypeStruct(x_t.shape, F32),
        compiler_params=_params("parallel"), name="cumsum",
    )(x_t)


def _cum_logf(lf_t):
    l = lf_t.shape[2]
    lp = -(-l // CUMSUM_BLK) * CUMSUM_BLK
    return _cumsum_t(jnp.pad(lf_t, ((0, 0), (0, 0), (0, lp - l))))[:, :, :l]


BIAS_PIECES = 3
LOG2E = 1.4426950408889634


def _bias_lane0(head):
    return ATTN_HEAD_DIM if head % 2 == 0 else 0


def _bias_placement():
    sq = np.zeros((BIAS_PIECES, ATTN_HEADS, ATTN_HEADS * LANES), np.float32)
    sk = np.zeros_like(sq)
    one_q = np.zeros((1, ATTN_HEADS * LANES), np.float32)
    one_k = np.zeros_like(one_q)
    for h in range(ATTN_HEADS):
        base = h * LANES + _bias_lane0(h)
        for piece in range(BIAS_PIECES):
            sk[piece, h, base + piece] = -1.0
            one_q[0, base + piece] = 1.0
            sq[piece, h, base + BIAS_PIECES + piece] = 1.0
            one_k[0, base + BIAS_PIECES + piece] = 1.0
    flat = lambda a: jnp.asarray(a.reshape(BIAS_PIECES * ATTN_HEADS, ATTN_HEADS * LANES), BF16)
    return flat(sq), flat(sk), jnp.asarray(one_q), jnp.asarray(one_k)


def _attn_prep_kernel(q_ref, k_ref, v_ref, c_ref, sq_ref, sk_ref, oneq_ref, onek_ref, qo_ref, ko_ref, kt_ref,
                      vt_ref):
    _prep_rows(q_ref[...], k_ref[...], v_ref[...], c_ref[...], sq_ref, sk_ref, oneq_ref, onek_ref,
               qo_ref, ko_ref, kt_ref, vt_ref, slice(None))


def _prep_rows(q, k, v, c, sq_ref, sk_ref, oneq_ref, onek_ref, qo_ref, ko_ref, kt_ref, vt_ref, rs):
    pieces = jnp.concatenate(_split3(c * LOG2E), axis=1)
    bias_q = _dot(pieces, sq_ref[...]) + oneq_ref[...]
    bias_k = _dot(pieces, sk_ref[...]) + onek_ref[...]
    lane = lax.broadcasted_iota(jnp.int32, (1, LANES), 1)
    lo = lane < ATTN_HEAD_DIM
    for p in range(HEAD_PAIRS):
        ps = slice(p * LANES, (p + 1) * LANES)
        q2 = q[:, ps] * (ATTN_HEAD_DIM ** -0.5 * LOG2E)
        k2 = k[:, ps]
        for hh in range(2):
            h = 2 * p + hh
            hs = slice(h * LANES, (h + 1) * LANES)
            sel = lo if hh == 0 else jnp.logical_not(lo)
            qo_ref[h, rs, :] = (jnp.where(sel, q2, 0.0) + bias_q[:, hs]).astype(BF16)
            ko_ref[h, rs, :] = (jnp.where(sel, k2, 0.0) + bias_k[:, hs]).astype(BF16)
        for src, dst in ((k2, kt_ref), (v[:, ps], vt_ref)):
            x_t = jnp.transpose(src)
            dst[2 * p, :, rs] = x_t[:ATTN_HEAD_DIM]
            dst[2 * p + 1, :, rs] = x_t[ATTN_HEAD_DIM:]


QKV_HALVES = 2


def _qkv_prep_kernel(u_ref, w_ref, c_ref, sq_ref, sk_ref, oneq_ref, onek_ref, qo_ref, ko_ref, kt_ref, vt_ref):
    rows = u_ref.shape[0] // QKV_HALVES
    for s in range(QKV_HALVES):
        rs = slice(s * rows, (s + 1) * rows)
        qkv = _dot(u_ref[rs, :], w_ref[...])
        q, k, v = (qkv[:, n * D_ATTN:(n + 1) * D_ATTN] for n in range(3))
        _prep_rows(q, k, v, c_ref[rs, :], sq_ref, sk_ref, oneq_ref, onek_ref, qo_ref, ko_ref, kt_ref, vt_ref, rs)


def _qkv_prep(u, w_qkv, c_rows, consts, *, tile):
    nb, l, _ = c_rows.shape
    nt = l // tile
    sq, sk, one_q, one_k = consts
    c2 = lambda b, i: (0, 0)
    return pl.pallas_call(
        _qkv_prep_kernel,
        grid=(nb, nt),
        in_specs=[
            pl.BlockSpec((tile, D_MODEL), lambda b, i: (b * nt + i, 0)),
            pl.BlockSpec(w_qkv.shape, c2, pipeline_mode=pl.Buffered(1)),
            pl.BlockSpec((None, tile, ATTN_HEADS), lambda b, i: (b, i, 0)),
            pl.BlockSpec(sq.shape, c2), pl.BlockSpec(sk.shape, c2),
            pl.BlockSpec(one_q.shape, c2), pl.BlockSpec(one_k.shape, c2),
        ],
        out_specs=[
            pl.BlockSpec((None, ATTN_HEADS, tile, LANES), lambda b, i: (b, 0, i, 0)),
            pl.BlockSpec((None, ATTN_HEADS, tile, LANES), lambda b, i: (b, 0, i, 0)),
            pl.BlockSpec((None, ATTN_HEADS, ATTN_HEAD_DIM, tile), lambda b, i: (b, 0, 0, i)),
            pl.BlockSpec((None, ATTN_HEADS, ATTN_HEAD_DIM, tile), lambda b, i: (b, 0, 0, i)),
        ],
        out_shape=[
            jax.ShapeDtypeStruct((nb, ATTN_HEADS, l, LANES), BF16),
            jax.ShapeDtypeStruct((nb, ATTN_HEADS, l, LANES), BF16),
            jax.ShapeDtypeStruct((nb, ATTN_HEADS, ATTN_HEAD_DIM, l), F32),
            jax.ShapeDtypeStruct((nb, ATTN_HEADS, ATTN_HEAD_DIM, l), F32),
        ],
        compiler_params=_params("parallel", "parallel"),
        name="qkv_prep",
    )(u, w_qkv, c_rows, sq, sk, one_q, one_k)


def _attn_prep(proj3, c_rows, consts, *, nb, b_off, tile):
    l = c_rows.shape[1]
    col = lambda off: off // D_ATTN
    sq, sk, one_q, one_k = consts
    c2 = lambda b, i: (0, 0)
    return pl.pallas_call(
        _attn_prep_kernel,
        grid=(nb, l // tile),
        in_specs=[
            pl.BlockSpec((None, tile, D_ATTN), lambda b, i: (b + b_off, i, col(P_Q))),
            pl.BlockSpec((None, tile, D_ATTN), lambda b, i: (b + b_off, i, col(P_K))),
            pl.BlockSpec((None, tile, D_ATTN), lambda b, i: (b + b_off, i, col(P_V))),
            pl.BlockSpec((None, tile, ATTN_HEADS), lambda b, i: (b, i, 0)),
            pl.BlockSpec(sq.shape, c2), pl.BlockSpec(sk.shape, c2),
            pl.BlockSpec(one_q.shape, c2), pl.BlockSpec(one_k.shape, c2),
        ],
        out_specs=[
            pl.BlockSpec((None, ATTN_HEADS, tile, LANES), lambda b, i: (b, 0, i, 0)),
            pl.BlockSpec((None, ATTN_HEADS, tile, LANES), lambda b, i: (b, 0, i, 0)),
            pl.BlockSpec((None, ATTN_HEADS, ATTN_HEAD_DIM, tile), lambda b, i: (b, 0, 0, i)),
            pl.BlockSpec((None, ATTN_HEADS, ATTN_HEAD_DIM, tile), lambda b, i: (b, 0, 0, i)),
        ],
        out_shape=[
            jax.ShapeDtypeStruct((nb, ATTN_HEADS, l, LANES), BF16),
            jax.ShapeDtypeStruct((nb, ATTN_HEADS, l, LANES), BF16),
            jax.ShapeDtypeStruct((nb, ATTN_HEADS, ATTN_HEAD_DIM, l), F32),
            jax.ShapeDtypeStruct((nb, ATTN_HEADS, ATTN_HEAD_DIM, l), F32),
        ],
        compiler_params=_params("parallel", "parallel"),
        name="attn_prep",
    )(proj3, proj3, proj3, c_rows, sq, sk, one_q, one_k)


QBLK = 256
ACC_ROWS = ATTN_HEAD_DIM + 16


def _attn_prompt_kernel(it_ref, jt_ref, q_ref, k_ref, vt_ref, km_ref, vtm_ref, o_ref, m_sc, acc_sc,
                        *, meta_lo, meta_hi):
    t = pl.program_id(1)
    i = it_ref[t]
    j = jt_ref[t]

    def v_rows(v_t):
        return jnp.concatenate([v_t.astype(BF16), jnp.ones((ACC_ROWS - ATTN_HEAD_DIM, v_t.shape[1]), BF16)], axis=0)

    def softmax_stage(h, scores, first):
        m_old_all = None if first else m_sc[h]
        out = []
        for r, (s_t, v_aug) in enumerate(scores):
            qblk = s_t.shape[1]
            mx = jnp.max(s_t, axis=0, keepdims=True)
            if first:
                m_new, alpha = mx, None
            else:
                m_old = m_old_all[:, r * qblk:(r + 1) * qblk]
                m_new = jnp.maximum(m_old, mx)
                alpha = jnp.exp2(m_old - m_new)
            out.append((jnp.exp2(s_t - m_new).astype(BF16), v_aug, alpha, m_new))
        m_sc[h] = jnp.concatenate([o[3] for o in out], axis=1)
        return out

    def pv_stage(h, probs, first):
        acc_old_all = None if first else acc_sc[h]
        acc_out = []
        for r, (pr, v_aug, alpha, _) in enumerate(probs):
            qblk = pr.shape[1]
            pv = _dot(v_aug, pr)
            acc_out.append(pv if first else acc_old_all[:, r * qblk:(r + 1) * qblk] * alpha + pv)
        acc_sc[h] = jnp.concatenate(acc_out, axis=1)

    def run(score_fn, first, qblk):
        scores = lambda h: [score_fn(h, r, q_ref[h, r * qblk:(r + 1) * qblk, :]) for r in range(ATT_T // qblk)]
        s_next = scores(0)
        p_cur = softmax_stage(0, s_next, first)
        s_next = scores(1)
        for h in range(ATTN_HEADS):
            s_cur = s_next
            if h + 2 < ATTN_HEADS:
                s_next = scores(h + 2)
            p_prev = p_cur
            if h + 1 < ATTN_HEADS:
                p_cur = softmax_stage(h + 1, s_cur, first)
            pv_stage(h, p_prev, first)

    @pl.when(j == 0)
    def _():
        krow = lax.broadcasted_iota(jnp.int32, (km_ref.shape[1], 1), 0)
        is_meta = jnp.logical_and(krow >= meta_lo, krow < meta_hi)
        run(lambda h, r, q: (jnp.where(is_meta, _dot_nt(km_ref[h], q), -jnp.inf), v_rows(vtm_ref[h])), True, ATT_T)

    @pl.when(j < i)
    def _():
        run(lambda h, r, q: (_dot_nt(k_ref[h], q), v_rows(vt_ref[h])), False, ATT_T)

    @pl.when(j == i)
    def _():
        def diag_scores(h, r, q):
            nk = (r + 1) * QBLK
            s_t = _dot_nt(k_ref[h, :nk, :], q)
            krow = lax.broadcasted_iota(jnp.int32, (nk, QBLK), 0)
            qcol = lax.broadcasted_iota(jnp.int32, (nk, QBLK), 1) + r * QBLK
            return jnp.where(krow <= qcol, s_t, -jnp.inf), v_rows(vt_ref[h, :, :nk])

        run(diag_scores, False, QBLK)
        d = ATTN_HEAD_DIM
        for p in range(HEAD_PAIRS):
            o_t = jnp.concatenate([acc_sc[h, :d, :] / acc_sc[h, d:d + 1, :] for h in (2 * p, 2 * p + 1)], axis=0)
            o_ref[:, p * LANES:(p + 1) * LANES] = jnp.transpose(o_t).astype(o_ref.dtype)


def _attn_prompt(q_aug, k_aug, v_t, km_aug, vm_t, *, meta_lo):
    nb, _, l, _ = q_aug.shape
    nt = l // ATT_T
    it = np.array([i for i in range(nt) for _ in range(i + 1)], np.int32)
    jt = np.array([j for i in range(nt) for j in range(i + 1)], np.int32)
    mrows = km_aug.shape[2]
    grid_spec = pltpu.PrefetchScalarGridSpec(
        num_scalar_prefetch=2,
        grid=(nb, len(it)),
        in_specs=[
            pl.BlockSpec((None, ATTN_HEADS, ATT_T, LANES), lambda b, t, it, jt: (b, 0, it[t], 0)),
            pl.BlockSpec((None, ATTN_HEADS, ATT_T, LANES), lambda b, t, it, jt: (b, 0, jt[t], 0)),
            pl.BlockSpec((None, ATTN_HEADS, ATTN_HEAD_DIM, ATT_T), lambda b, t, it, jt: (b, 0, 0, jt[t])),
            pl.BlockSpec((None, ATTN_HEADS, mrows, LANES), lambda b, t, it, jt: (0, 0, 0, 0)),
            pl.BlockSpec((None, ATTN_HEADS, ATTN_HEAD_DIM, mrows), lambda b, t, it, jt: (0, 0, 0, 0)),
        ],
        out_specs=pl.BlockSpec((None, ATT_T, D_ATTN), lambda b, t, it, jt: (b, it[t], 0)),
        scratch_shapes=[
            pltpu.VMEM((ATTN_HEADS, 1, ATT_T), F32),
            pltpu.VMEM((ATTN_HEADS, ACC_ROWS, ATT_T), F32),
        ],
    )
    return pl.pallas_call(
        functools.partial(_attn_prompt_kernel, meta_lo=meta_lo, meta_hi=meta_lo + N_META),
        grid_spec=grid_spec,
        out_shape=jax.ShapeDtypeStruct((nb, l, D_ATTN), BF16),
        compiler_params=_params("parallel", "arbitrary"),
        name="attn_prompt",
    )(jnp.asarray(it), jnp.asarray(jt), q_aug, k_aug, v_t, km_aug, vm_t)


SAMPLE_HEADS = 8


def _attn_sample_kernel(q_ref, kn_ref, vn_ref, km_ref, vm_ref, kct_ref, vct_ref, cq_ref, ckm_ref, ckc_ref,
                        ckn_ref, o_ref):
    T = CHUNK
    D = ATTN_HEAD_DIM
    extra = ACC_ROWS - D
    row = lax.broadcasted_iota(jnp.int32, (T, T), 0)
    col = lax.broadcasted_iota(jnp.int32, (T, T), 1)

    def v_cols(v):
        return jnp.concatenate([v.astype(BF16), jnp.ones((v.shape[0], extra), BF16)], axis=1)

    def v_rows(v_t):
        return jnp.concatenate([v_t.astype(BF16), jnp.ones((extra, v_t.shape[1]), BF16)], axis=0)

    scores = []
    for hh in range(SAMPLE_HEADS):
        hs = slice(hh * D, (hh + 1) * D)
        qh = (q_ref[:, hs] * (D ** -0.5 * LOG2E)).astype(BF16)
        t_m = _dot_nt(qh, km_ref[:, hs].astype(BF16)) - ckm_ref[hh:hh + 1, :] * LOG2E
        t_c = _dot(qh, kct_ref[hh].astype(BF16)) - ckc_ref[hh:hh + 1, :] * LOG2E
        t_n = _dot_nt(qh, kn_ref[:, hs].astype(BF16)) - ckn_ref[hh:hh + 1, :] * LOG2E
        scores.append((t_m, t_c, jnp.where(col <= row, t_n, -jnp.inf)))
    outs = []
    for hh, (t_m, t_c, t_n) in enumerate(scores):
        hs = slice(hh * D, (hh + 1) * D)
        cq = cq_ref[hh] * LOG2E
        m = cq + jnp.maximum(jnp.maximum(jnp.max(t_m, axis=1, keepdims=True), jnp.max(t_c, axis=1, keepdims=True)),
                             jnp.max(t_n, axis=1, keepdims=True))
        shift = cq - m
        acc = (_dot(jnp.exp2(t_m + shift).astype(BF16), v_cols(vm_ref[:, hs]))
               + _dot_nt(jnp.exp2(t_c + shift).astype(BF16), v_rows(vct_ref[hh]))
               + _dot(jnp.exp2(t_n + shift).astype(BF16), v_cols(vn_ref[:, hs])))
        outs.append(acc[:, :D] / acc[:, D:D + 1])
    o_ref[...] = jnp.concatenate(outs, axis=1).astype(o_ref.dtype)


def _attn_sample(proj3, proj2, meta_blk, cache_k, cache_v, cq, ckm_t, ckc_t, ckn_t, *, nb):
    past = cache_k.shape[3]
    hg = SAMPLE_HEADS
    wg = hg * ATTN_HEAD_DIM
    groups = ATTN_HEADS // hg
    lb = lambda off: off // wg
    grp = lambda off: (lambda b, p: (b, 0, lb(off) + p))
    ckspec = lambda n: pl.BlockSpec((None, hg, n), lambda b, p: (b * groups + p, 0, 0))
    return pl.pallas_call(
        _attn_sample_kernel,
        grid=(nb, groups),
        in_specs=[
            pl.BlockSpec((None, CHUNK, wg), grp(P_Q)),
            pl.BlockSpec((None, CHUNK, wg), grp(P_K)),
            pl.BlockSpec((None, CHUNK, wg), grp(P_V)),
            pl.BlockSpec((N_META, wg), lambda b, p: (meta_blk, lb(P_K) + p)),
            pl.BlockSpec((N_META, wg), lambda b, p: (meta_blk, lb(P_V) + p)),
            pl.BlockSpec((None, hg, ATTN_HEAD_DIM, past), lambda b, p: (b, p, 0, 0)),
            pl.BlockSpec((None, hg, ATTN_HEAD_DIM, past), lambda b, p: (b, p, 0, 0)),
            pl.BlockSpec((None, hg, CHUNK, 1), lambda b, p: (b, p, 0, 0)),
            ckspec(N_META), ckspec(past), ckspec(CHUNK),
        ],
        out_specs=pl.BlockSpec((None, CHUNK, wg), lambda b, p: (b, 0, p)),
        out_shape=jax.ShapeDtypeStruct((nb, CHUNK, D_ATTN), BF16),
        compiler_params=_params("parallel", "parallel"),
        name="attn_sample",
    )(proj3, proj3, proj3, proj2, proj2, cache_k, cache_v, cq, ckm_t, ckc_t, ckn_t)


def _merge_kernel(os_ref, oa_ref, gs_ref, ga_ref, h_ref, wbs_ref, wba_ref, wo_ref, o_ref):
    bs = _dot(os_ref[...], wbs_ref[...])
    ba = _dot(oa_ref[...], wba_ref[...])
    merged = jax.nn.sigmoid(gs_ref[...]) * bs + jax.nn.sigmoid(ga_ref[...]) * ba
    o_ref[...] = h_ref[...] + _dot(merged.astype(BF16), wo_ref[...])


def _merge(o_ssm, o_attn, proj, h, w_bs, w_ba, w_o, *, tm, gs_off, ga_off):
    rows = o_ssm.shape[0]
    const = lambda i: (0, 0)
    return pl.pallas_call(
        _merge_kernel,
        grid=(rows // tm,),
        in_specs=[
            pl.BlockSpec((tm, D_SSM), lambda i: (i, 0)),
            pl.BlockSpec((tm, D_ATTN), lambda i: (i, 0)),
            pl.BlockSpec((tm, D_MODEL), lambda i: (i, gs_off // D_MODEL)),
            pl.BlockSpec((tm, D_MODEL), lambda i: (i, ga_off // D_MODEL)),
            pl.BlockSpec((tm, D_MODEL), lambda i: (i, 0)),
            pl.BlockSpec((D_SSM, D_MODEL), const),
            pl.BlockSpec((D_ATTN, D_MODEL), const),
            pl.BlockSpec((D_MODEL, D_MODEL), const),
        ],
        out_specs=pl.BlockSpec((tm, D_MODEL), lambda i: (i, 0)),
        out_shape=jax.ShapeDtypeStruct((rows, D_MODEL), F32),
        compiler_params=_params("parallel"),
        name="merge",
    )(o_ssm, o_attn, proj, proj, h, w_bs, w_ba, w_o)


def kernel(x_prompt, x_sample, cache_k, cache_v, cache_logf, state_ssm, state_conv, meta_tokens, norm_ffn1, ffn1_w_in, ffn1_w_out, norm_mix, w_in, conv_w, conv_b, dt_bias, a_log, d_skip, f_bias, ssm_norm, w_br_ssm, w_br_attn, w_out, norm_ffn2, ffn2_w_in, ffn2_w_out, norm_final):
    assert norm_ffn1.shape[0] == 1, "single-layer trunk"
    bp, seq, _ = x_prompt.shape
    bs, dseq, _ = x_sample.shape
    assert dseq == CHUNK and seq % ATT_T == 0
    n_s = bs * dseq
    rows_small = n_s + 2 * CHUNK
    meta_row0 = n_s + CHUNK - N_META
    meta_blk = meta_row0 // N_META

    row = lambda v: v.reshape(1, -1).astype(F32)
    w = w_in[0]
    w_main = jnp.concatenate([w[:, OFF_XBC:OFF_DT], w[:, :OFF_XBC], w[:, OFF_Q:OFF_F], w[:, OFF_GS:]],
                             axis=1).astype(BF16)
    pad_small = LANES - SSM_HEADS - ATTN_HEADS
    w_small = jnp.concatenate([w[:, OFF_DT:OFF_Q], w[:, OFF_F:OFF_GS], jnp.zeros((D_MODEL, pad_small), F32)],
                              axis=1).astype(BF16)
    b_small = jnp.concatenate([dt_bias[0], f_bias[0], jnp.zeros((pad_small,), F32)]).reshape(1, LANES)
    w1i, w1o = ffn1_w_in[0].astype(BF16), ffn1_w_out[0].astype(BF16)
    w2i, w2o = ffn2_w_in[0].astype(BF16), ffn2_w_out[0].astype(BF16)
    w_bs, w_ba, w_o = w_br_ssm[0].astype(BF16), w_br_attn[0].astype(BF16), w_out[0].astype(BF16)
    expand = jnp.asarray(np.tile(np.repeat(np.eye(SSM_HEADS, dtype=np.float32), SSM_HEAD_DIM, axis=1), (3, 1)),
                         BF16)
    ssd_consts = (conv_w[0], row(conv_b[0]), row(a_log[0]), row(jnp.repeat(d_skip[0], SSM_HEAD_DIM)),
                  row(ssm_norm[0]), expand)

    def front(x, w_proj, tm_ffn, tm_proj):
        h1, u = _ffn(x, row(norm_ffn1[0]), w1i, w1o, row(norm_mix[0]), tm=tm_ffn, emit_h=True, u_dtype=BF16)
        return h1, u, _proj(u, w_proj, tm=tm_proj), _small(u, w_small, b_small, tm=tm_proj)

    w_qkv = w_main[:, P_Q:P_GS]
    w_prompt = jnp.concatenate([w_main[:, :P_Q], w_main[:, P_GS:]], axis=1)
    xp = x_prompt.reshape(bp * seq, D_MODEL)
    h1_p, u_p, pm_p, ps_p = front(xp, w_prompt, 512, 1024)
    x_small = jnp.concatenate([x_sample.reshape(n_s, D_MODEL), jnp.zeros((CHUNK - N_META, D_MODEL), F32),
                               meta_tokens, jnp.zeros((CHUNK, D_MODEL), F32)], axis=0)
    h1_s, _, pm_s, ps_s = front(x_small, w_main, rows_small // 2, rows_small // 2)

    pm_p3 = pm_p.reshape(bp, seq, w_prompt.shape[1])
    ps_p3 = ps_p.reshape(bp, seq, LANES)
    pm_s3 = pm_s.reshape(rows_small // CHUNK, CHUNK, P_TOTAL)
    ps_s3 = ps_s.reshape(rows_small // CHUNK, CHUNK, LANES)

    zero_state = jnp.zeros((1, D_STATE, D_SSM), F32)
    zero_hist = jnp.zeros((1, CONV_W - 1, D_CONV), F32)
    _, st_m, hist_m = _ssd(pm_s3, ps_s3, zero_state, zero_hist, *ssd_consts,
                           nb=1, b_off=bs, shared_init=True, n_pad=CHUNK - N_META, cps=1)
    o_ssm_p, st_p, hist_p = _ssd(pm_p3, ps_p3, st_m, hist_m, *ssd_consts,
                                 nb=bp, b_off=0, shared_init=True, n_pad=0, cps=SSD_CHUNKS_PER_STEP)
    st_s0 = jnp.transpose(state_ssm[0].reshape(bs, D_SSM, D_STATE), (0, 2, 1))
    o_ssm_s, st_s, hist_s = _ssd(pm_s3, ps_s3, st_s0, state_conv[0], *ssd_consts,
                                 nb=bs, b_off=0, shared_init=False, n_pad=0, cps=1)

    t_last = lambda a: jnp.swapaxes(a, -1, -2)
    lf_m = t_last(ps_s[meta_row0:meta_row0 + N_META, S_F:S_F + ATTN_HEADS])
    lf_p = t_last(ps_p3[:, :, S_F:S_F + ATTN_HEADS])
    lf_s = ps_s[:n_s, S_F:S_F + ATTN_HEADS].reshape(bs, dseq, ATTN_HEADS)
    lf_all_p = jnp.concatenate([jnp.broadcast_to(lf_m[None], (bp, ATTN_HEADS, N_META)), lf_p], axis=2)
    lf_all_s = jnp.concatenate([jnp.broadcast_to(lf_m[None], (bs, ATTN_HEADS, N_META)),
                                t_last(cache_logf[0].astype(F32)), t_last(lf_s)], axis=2)
    past = cache_logf.shape[2]
    c_p = _cum_logf(lf_all_p)
    c_s = _cum_logf(lf_all_s)

    prep_consts = _bias_placement()
    q_aug, k_aug, k_t, v_t = _qkv_prep(u_p, w_qkv, t_last(c_p[:, :, N_META:]), prep_consts, tile=ATT_T)
    meta_tile = 2 * CHUNK
    meta_lo = meta_row0 - n_s
    c_meta = jnp.pad(t_last(c_p[:1, :, :N_META]), ((0, 0), (meta_lo, meta_tile - meta_lo - N_META), (0, 0)))
    _, km_aug, km_t, vm_t = _attn_prep(pm_s.reshape(rows_small // meta_tile, meta_tile, P_TOTAL), c_meta,
                                       prep_consts, nb=1, b_off=n_s // meta_tile, tile=meta_tile)
    o_attn_p = _attn_prompt(q_aug, k_aug, v_t, km_aug, vm_t, meta_lo=meta_lo)
    per_pair = lambda c: c.reshape(bs * ATTN_HEADS // SAMPLE_HEADS, SAMPLE_HEADS, c.shape[2])
    feat_major = lambda a: jnp.transpose(a, (0, 2, 3, 1))
    o_attn_s = _attn_sample(pm_s3, pm_s, meta_blk, feat_major(cache_k[0]), feat_major(cache_v[0]),
                            c_s[:, :, N_META + past:, None],
                            per_pair(c_s[:, :, :N_META]), per_pair(c_s[:, :, N_META:N_META + past]),
                            per_pair(c_s[:, :, N_META + past:]), nb=bs)

    def back(o_ssm, o_attn, pm, h1, tm, gs_off, ga_off):
        h2 = _merge(o_ssm, o_attn, pm, h1, w_bs, w_ba, w_o, tm=tm, gs_off=gs_off, ga_off=ga_off)
        return _ffn(h2, row(norm_ffn2[0]), w2i, w2o, row(norm_final), tm=tm, emit_h=False, u_dtype=F32)[0]

    y_prompt = back(o_ssm_p.reshape(bp * seq, D_SSM), o_attn_p.reshape(bp * seq, D_ATTN), pm_p, h1_p, 512,
                    PP_GS, PP_GA)
    y_sample = back(o_ssm_s.reshape(n_s, D_SSM), o_attn_s.reshape(n_s, D_ATTN), pm_s, h1_s, 512, P_GS, P_GA)

    heads = lambda a: a.reshape(1, a.shape[0], a.shape[1], ATTN_HEADS, ATTN_HEAD_DIM)

    def with_meta(x_t, m_t):
        m_t = jnp.broadcast_to(m_t[:, :, :, meta_lo:meta_lo + N_META], (bp, ATTN_HEADS, ATTN_HEAD_DIM, N_META))
        return jnp.transpose(jnp.concatenate([m_t, x_t], axis=3), (0, 3, 1, 2))[None]

    k_prompt = with_meta(k_t, km_t)
    v_prompt = with_meta(v_t, vm_t)
    logf_prompt = t_last(lf_all_p)
    k_sample = heads(pm_s[:n_s, P_K:P_K + D_ATTN].reshape(bs, dseq, D_ATTN))
    v_sample = heads(pm_s[:n_s, P_V:P_V + D_ATTN].reshape(bs, dseq, D_ATTN))
    state_out = lambda st: jnp.transpose(st, (0, 2, 1)).reshape(1, st.shape[0], SSM_HEADS, SSM_HEAD_DIM, D_STATE)
    return (y_prompt.reshape(bp, seq, D_MODEL), y_sample.reshape(bs, dseq, D_MODEL),
            k_prompt, v_prompt, logf_prompt[None], state_out(st_p), hist_p[None],
            k_sample, v_sample, lf_s[None], state_out(st_s), hist_s[None])
```

```python
import functools

import numpy as np
import jax
import jax.numpy as jnp
from jax import lax
from jax.experimental import pallas as pl
from jax.experimental.pallas import tpu as pltpu

F32 = jnp.float32
BF16 = jnp.bfloat16

D_MODEL = 1024
N_META = 16
CHUNK = 64
EPS = 1e-6
D_SSM = 2048
SSM_HEADS = 32
SSM_HEAD_DIM = 64
SSM_GROUPS = 8
D_STATE = 128
CONV_W = 4
D_CONV = D_SSM + 2 * SSM_GROUPS * D_STATE
ATTN_HEADS = 16
ATTN_HEAD_DIM = 64
D_ATTN = ATTN_HEADS * ATTN_HEAD_DIM
D_FF = 2816
HEAD_PAIRS = ATTN_HEADS // 2
LANES = 128
GROUP_W = D_SSM // SSM_GROUPS

OFF_XBC = D_SSM
OFF_DT = OFF_XBC + D_CONV
OFF_Q = OFF_DT + SSM_HEADS
OFF_K = OFF_Q + D_ATTN
OFF_V = OFF_K + D_ATTN
OFF_F = OFF_V + D_ATTN
OFF_GS = OFF_F + ATTN_HEADS
OFF_GA = OFF_GS + D_MODEL

P_XBC = 0
P_Z = D_CONV
P_Q = P_Z + D_SSM
P_K = P_Q + D_ATTN
P_V = P_K + D_ATTN
P_GS = P_V + D_ATTN
P_GA = P_GS + D_MODEL
P_TOTAL = P_GA + D_MODEL
S_DT = 0
S_F = SSM_HEADS

VMEM_LIMIT = 56 * 1024 * 1024

PROJ_COL_TILES = 4
PP_GS = P_Q
PP_GA = PP_GS + D_MODEL
ATT_T = 512
CUMSUM_BLK = 256


def _dot(a, b):
    return jnp.dot(a, b, preferred_element_type=F32)


def _dot_nt(a, b):
    return lax.dot_general(a, b, (((1,), (1,)), ((), ())), preferred_element_type=F32)


def _dot_tn(a, b):
    return lax.dot_general(a, b, (((0,), (0,)), ((), ())), preferred_element_type=F32)


def _split3(x):
    hi = x.astype(BF16)
    r = x - hi.astype(F32)
    mid = r.astype(BF16)
    lo = (r - mid.astype(F32)).astype(BF16)
    return hi, mid, lo


def _rmsnorm(x, g):
    return x * lax.rsqrt(jnp.mean(x * x, axis=-1, keepdims=True) + EPS) * g


def _softplus(x):
    return jnp.maximum(x, 0.0) + jnp.log1p(jnp.exp(-jnp.abs(x)))


def _params(*sem):
    return pltpu.CompilerParams(dimension_semantics=sem, vmem_limit_bytes=VMEM_LIMIT)


FFN_HALVES = 2


def _ffn_kernel(x_ref, g1_ref, wi_ref, wo_ref, g2_ref, *out_refs, emit_h):
    h_ref, u_ref = out_refs if emit_h else (None,) + out_refs
    rows = x_ref.shape[0] // FFN_HALVES
    for s in range(FFN_HALVES):
        rs = slice(s * rows, (s + 1) * rows)
        x = x_ref[rs, :]
        xn = _rmsnorm(x, g1_ref[...]).astype(BF16)
        a = _dot(xn, wi_ref[:, :D_FF])
        b = _dot(xn, wi_ref[:, D_FF:])
        g = (a * jax.nn.sigmoid(a) * b).astype(BF16)
        h = x + 0.5 * _dot(g, wo_ref[...])
        if emit_h:
            h_ref[rs, :] = h
        u_ref[rs, :] = _rmsnorm(h, g2_ref[...]).astype(u_ref.dtype)


def _ffn(x, g1, w_in, w_out, g2, *, tm, emit_h, u_dtype):
    rows = x.shape[0]
    row_spec = pl.BlockSpec((tm, D_MODEL), lambda i: (i, 0))
    vec_spec = pl.BlockSpec((1, D_MODEL), lambda i: (0, 0))
    resident = lambda shape: pl.BlockSpec(shape, lambda i: (0, 0), pipeline_mode=pl.Buffered(1))
    out_shape = [jax.ShapeDtypeStruct((rows, D_MODEL), u_dtype)]
    out_specs = [row_spec]
    if emit_h:
        out_shape = [jax.ShapeDtypeStruct((rows, D_MODEL), F32)] + out_shape
        out_specs = [row_spec] + out_specs
    return pl.pallas_call(
        functools.partial(_ffn_kernel, emit_h=emit_h),
        grid=(rows // tm,),
        in_specs=[row_spec, vec_spec, resident((D_MODEL, 2 * D_FF)), resident((D_FF, D_MODEL)), vec_spec],
        out_specs=out_specs,
        out_shape=out_shape,
        compiler_params=_params("parallel"),
        name="ffn",
    )(x, g1, w_in, w_out, g2)


def _proj_kernel(u_ref, w_ref, o_ref):
    o_ref[...] = _dot(u_ref[...], w_ref[...])


def _proj(u, w, *, tm):
    rows = u.shape[0]
    width = w.shape[1]
    tn = width // PROJ_COL_TILES
    return pl.pallas_call(
        _proj_kernel,
        grid=(PROJ_COL_TILES, rows // tm),
        in_specs=[pl.BlockSpec((tm, D_MODEL), lambda j, i: (i, 0)),
                  pl.BlockSpec((D_MODEL, tn), lambda j, i: (0, j))],
        out_specs=pl.BlockSpec((tm, tn), lambda j, i: (i, j)),
        out_shape=jax.ShapeDtypeStruct((rows, width), F32),
        compiler_params=_params("parallel", "parallel"),
        name="proj",
    )(u, w)


def _small_kernel(u_ref, w_ref, b_ref, o_ref):
    x = _dot(u_ref[...], w_ref[...]) + b_ref[...]
    lane = lax.broadcasted_iota(jnp.int32, (1, LANES), 1)
    sgn = jnp.where(lane < S_F, 1.0, -1.0)
    o_ref[...] = sgn * _softplus(sgn * x)


def _small(u, w, b, *, tm):
    rows = u.shape[0]
    return pl.pallas_call(
        _small_kernel,
        grid=(rows // tm,),
        in_specs=[pl.BlockSpec((tm, D_MODEL), lambda i: (i, 0)),
                  pl.BlockSpec((D_MODEL, LANES), lambda i: (0, 0)),
                  pl.BlockSpec((1, LANES), lambda i: (0, 0))],
        out_specs=pl.BlockSpec((tm, LANES), lambda i: (i, 0)),
        out_shape=jax.ShapeDtypeStruct((rows, LANES), F32),
        compiler_params=_params("parallel"),
        name="small",
    )(u, w, b)


HIST0 = 8 - (CONV_W - 1)
SSD_CHUNKS_PER_STEP = 8


def _ssd_chunk(rs, xbc_ref, z_ref, dt_ref, cw_ref, cb_ref, alog_ref, dskip_ref, norm_ref, e_ref, o_ref,
               st_sc, xp_sc, y_sc, n_pad):
    T = CHUNK
    xraw = xbc_ref[rs, :]
    xp_sc[8:8 + T, :] = xraw
    cw = cw_ref[...]
    xp = xp_sc[...]
    conv = cb_ref[...]
    for j in range(CONV_W - 1):
        conv = conv + cw[j:j + 1] * pltpu.roll(xp, CONV_W - 1 - j, axis=0)[8:8 + T, :]
    conv = conv + cw[3:4] * xraw
    xp_sc[HIST0:8, :] = xraw[T - (CONV_W - 1):T, :]
    xc = conv * jax.nn.sigmoid(conv)
    dtv = dt_ref[rs, S_DT:S_DT + SSM_HEADS]
    if n_pad:
        valid = lax.broadcasted_iota(jnp.int32, (T, 1), 0) >= n_pad
        xc = jnp.where(valid, xc, 0.0)
        dtv = jnp.where(valid, dtv, 0.0)

    a = -jnp.exp(alog_ref[...])
    adt = a * dtv
    row = lax.broadcasted_iota(jnp.int32, (T, T), 0)
    col = lax.broadcasted_iota(jnp.int32, (T, T), 1)
    causal = col <= row
    tril = jnp.where(causal, 1.0, 0.0).astype(BF16)
    r32 = lax.broadcasted_iota(jnp.int32, (SSM_HEADS, SSM_HEADS), 0)
    c32 = lax.broadcasted_iota(jnp.int32, (SSM_HEADS, SSM_HEADS), 1)
    eye = jnp.where(r32 == c32, 1.0, 0.0).astype(BF16)
    expand3 = e_ref[...]

    def expand(v):
        return _dot(jnp.concatenate(_split3(v), axis=1), expand3)

    a_cs = sum(_dot(tril, piece) for piece in _split3(adt))
    a_cs_t = sum(_dot_nt(eye, piece) for piece in _split3(a_cs))
    a_last = a_cs[T - 1:T, :]
    dt_e = expand(dtv)
    wdec_e = expand(jnp.exp(a_last - a_cs) * dtv)
    eacs = expand(jnp.exp(a_cs))
    cdec = eacs[T - 1:T, :]

    xs = xc[:, :D_SSM]
    bm = xc[:, D_SSM:D_SSM + SSM_GROUPS * D_STATE].astype(BF16)
    cm = xc[:, D_SSM + SSM_GROUPS * D_STATE:].astype(BF16)
    xd = xs * dt_e
    xdw = (xs * wdec_e).astype(BF16)
    lane = lax.broadcasted_iota(jnp.int32, (1, D_SSM), 1)
    lo = (lane & SSM_HEAD_DIM) == 0
    xd_lo = jnp.where(lo, xd, 0.0).astype(BF16)
    xd_hi = jnp.where(lo, 0.0, xd).astype(BF16)
    lo2 = lax.broadcasted_iota(jnp.int32, (1, LANES), 1) < SSM_HEAD_DIM
    step2 = lax.broadcasted_iota(jnp.int32, (T, LANES), 1) & (SSM_HEAD_DIM - 1)
    causal2 = step2 <= lax.broadcasted_iota(jnp.int32, (T, LANES), 0)

    for g in range(SSM_GROUPS):
        gs = slice(g * GROUP_W, (g + 1) * GROUP_W)
        cmg = cm[:, g * D_STATE:(g + 1) * D_STATE]
        bmg = bm[:, g * D_STATE:(g + 1) * D_STATE]
        cb2 = _dot_nt(cmg, jnp.concatenate([bmg, bmg], axis=0))
        stg = st_sc[:, gs]
        yoff = _dot(cmg, stg.astype(BF16))
        for rr in range(2):
            pair = g * 2 + rr
            ps = slice(pair * LANES, (pair + 1) * LANES)
            ha, hb = 2 * pair, 2 * pair + 1
            seg = (jnp.where(lo2, a_cs[:, ha:ha + 1], a_cs[:, hb:hb + 1])
                   - jnp.concatenate([a_cs_t[ha:ha + 1, :], a_cs_t[hb:hb + 1, :]], axis=1))
            m2 = (cb2 * jnp.exp(jnp.where(causal2, seg, -jnp.inf))).astype(BF16)
            ydiag = _dot(m2, jnp.concatenate([xd_lo[:, ps], xd_hi[:, ps]], axis=0))
            y_sc[:, ps] = ydiag + yoff[:, rr * LANES:(rr + 1) * LANES] * eacs[:, ps]
        st_sc[:, gs] = stg * cdec[:, gs] + _dot_tn(bmg, xdw[:, gs])

    y = y_sc[...] + dskip_ref[...] * xs
    zz = z_ref[rs, :]
    y = y * (zz * jax.nn.sigmoid(zz))
    for g in range(SSM_GROUPS):
        gs = slice(g * GROUP_W, (g + 1) * GROUP_W)
        yg = y[:, gs]
        yg = yg * lax.rsqrt(jnp.mean(yg * yg, axis=-1, keepdims=True) + EPS)
        o_ref[rs, gs] = (yg * norm_ref[:, gs]).astype(o_ref.dtype)


def _ssd_kernel(xbc_ref, z_ref, dt_ref, s0_ref, h0_ref, cw_ref, cb_ref, alog_ref, dskip_ref, norm_ref,
                e_ref, o_ref, sT_ref, hT_ref, st_sc, xp_sc, y_sc, *, n_pad, cps):
    c = pl.program_id(1)

    @pl.when(c == 0)
    def _():
        st_sc[...] = s0_ref[...]
        xp_sc[HIST0:8, :] = h0_ref[...]

    for ci in range(cps):
        _ssd_chunk(slice(ci * CHUNK, (ci + 1) * CHUNK), xbc_ref, z_ref, dt_ref, cw_ref, cb_ref, alog_ref,
                   dskip_ref, norm_ref, e_ref, o_ref, st_sc, xp_sc, y_sc, n_pad)

    @pl.when(c == pl.num_programs(1) - 1)
    def _():
        sT_ref[...] = st_sc[...]
        hT_ref[...] = xp_sc[HIST0:8, :]


def _ssd(proj3, small3, s0, h0, cw, cb, alog, dskip, norm, expand, *, nb, b_off, shared_init, n_pad, cps):
    rows = cps * CHUNK
    nc = proj3.shape[1] // rows
    init = (lambda b, c: (0, 0, 0)) if shared_init else (lambda b, c: (b, 0, 0))
    const = lambda b, c: (0, 0)
    return pl.pallas_call(
        functools.partial(_ssd_kernel, n_pad=n_pad, cps=cps),
        grid=(nb, nc),
        in_specs=[
            pl.BlockSpec((None, rows, D_CONV), lambda b, c: (b + b_off, c, P_XBC // D_CONV)),
            pl.BlockSpec((None, rows, D_SSM), lambda b, c: (b + b_off, c, P_Z // D_SSM)),
            pl.BlockSpec((None, rows, LANES), lambda b, c: (b + b_off, c, 0)),
            pl.BlockSpec((None, D_STATE, D_SSM), init),
            pl.BlockSpec((None, CONV_W - 1, D_CONV), init),
            pl.BlockSpec((CONV_W, D_CONV), const),
            pl.BlockSpec((1, D_CONV), const),
            pl.BlockSpec((1, SSM_HEADS), const),
            pl.BlockSpec((1, D_SSM), const),
            pl.BlockSpec((1, D_SSM), const),
            pl.BlockSpec((3 * SSM_HEADS, D_SSM), const),
        ],
        out_specs=[
            pl.BlockSpec((None, rows, D_SSM), lambda b, c: (b, c, 0)),
            pl.BlockSpec((None, D_STATE, D_SSM), lambda b, c: (b, 0, 0)),
            pl.BlockSpec((None, CONV_W - 1, D_CONV), lambda b, c: (b, 0, 0)),
        ],
        out_shape=[
            jax.ShapeDtypeStruct((nb, nc * rows, D_SSM), BF16),
            jax.ShapeDtypeStruct((nb, D_STATE, D_SSM), F32),
            jax.ShapeDtypeStruct((nb, CONV_W - 1, D_CONV), F32),
        ],
        scratch_shapes=[
            pltpu.VMEM((D_STATE, D_SSM), F32),
            pltpu.VMEM((8 + CHUNK, D_CONV), F32),
            pltpu.VMEM((CHUNK, D_SSM), F32),
        ],
        compiler_params=_params("parallel", "arbitrary"),
        name="ssd",
    )(proj3, proj3, small3, s0, h0, cw, cb, alog, dskip, norm, expand)


def _cumsum_kernel(x_ref, o_ref):
    n = x_ref.shape[-1] // CUMSUM_BLK
    r = lax.broadcasted_iota(jnp.int32, (CUMSUM_BLK, CUMSUM_BLK), 0)
    c = lax.broadcasted_iota(jnp.int32, (CUMSUM_BLK, CUMSUM_BLK), 1)
    upper = jnp.where(r <= c, 1.0, 0.0).astype(BF16)
    carry = jnp.zeros((ATTN_HEADS, 1), F32)
    for i in range(n):
        blk = slice(i * CUMSUM_BLK, (i + 1) * CUMSUM_BLK)
        cs = sum(_dot(piece, upper) for piece in _split3(x_ref[:, blk])) + carry
        o_ref[:, blk] = cs
        carry = cs[:, CUMSUM_BLK - 1:CUMSUM_BLK]


def _cumsum_t(x_t):
    b, h, l = x_t.shape
    spec = pl.BlockSpec((None, h, l), lambda i: (i, 0, 0))
    return pl.pallas_call(
        _cumsum_kernel, grid=(b,), in_specs=[spec], out_specs=spec,
        out_shape=jax.ShapeDtypeStruct(x_t.shape, F32),
        compiler_params=_params("parallel"), name="cumsum",
    )(x_t)


def _cum_logf(lf_t):
    l = lf_t.shape[2]
    lp = -(-l // CUMSUM_BLK) * CUMSUM_BLK
    return _cumsum_t(jnp.pad(lf_t, ((0, 0), (0, 0), (0, lp - l))))[:, :, :l]


BIAS_PIECES = 3
LOG2E = 1.4426950408889634


def _bias_lane0(head):
    return ATTN_HEAD_DIM if head % 2 == 0 else 0


def _bias_placement():
    sq = np.zeros((BIAS_PIECES, ATTN_HEADS, ATTN_HEADS * LANES), np.float32)
    sk = np.zeros_like(sq)
    one_q = np.zeros((1, ATTN_HEADS * LANES), np.float32)
    one_k = np.zeros_like(one_q)
    for h in range(ATTN_HEADS):
        base = h * LANES + _bias_lane0(h)
        for piece in range(BIAS_PIECES):
            sk[piece, h, base + piece] = -1.0
            one_q[0, base + piece] = 1.0
            sq[piece, h, base + BIAS_PIECES + piece] = 1.0
            one_k[0, base + BIAS_PIECES + piece] = 1.0
    flat = lambda a: jnp.asarray(a.reshape(BIAS_PIECES * ATTN_HEADS, ATTN_HEADS * LANES), BF16)
    return flat(sq), flat(sk), jnp.asarray(one_q), jnp.asarray(one_k)


def _attn_prep_kernel(q_ref, k_ref, v_ref, c_ref, sq_ref, sk_ref, oneq_ref, onek_ref, qo_ref, ko_ref, kt_ref,
                      vt_ref):
    _prep_rows(q_ref[...], k_ref[...], v_ref[...], c_ref[...], sq_ref, sk_ref, oneq_ref, onek_ref,
               qo_ref, ko_ref, kt_ref, vt_ref, slice(None))


def _prep_rows(q, k, v, c, sq_ref, sk_ref, oneq_ref, onek_ref, qo_ref, ko_ref, kt_ref, vt_ref, rs, norm_out=None):
    max_sq = lambda x: jnp.max(jnp.sum(x * x, axis=1, keepdims=True), axis=0, keepdims=True)
    qn, kn = [], []
    pieces = jnp.concatenate(_split3(c * LOG2E), axis=1)
    bias_q = _dot(pieces, sq_ref[...]) + oneq_ref[...]
    bias_k = _dot(pieces, sk_ref[...]) + onek_ref[...]
    lane = lax.broadcasted_iota(jnp.int32, (1, LANES), 1)
    lo = lane < ATTN_HEAD_DIM
    for p in range(HEAD_PAIRS):
        ps = slice(p * LANES, (p + 1) * LANES)
        q2 = q[:, ps] * (ATTN_HEAD_DIM ** -0.5 * LOG2E)
        k2 = k[:, ps]
        for hh in range(2):
            h = 2 * p + hh
            hs = slice(h * LANES, (h + 1) * LANES)
            sel = lo if hh == 0 else jnp.logical_not(lo)
            qh = jnp.where(sel, q2, 0.0)
            kh = jnp.where(sel, k2, 0.0)
            qo_ref[h, rs, :] = (qh + bias_q[:, hs]).astype(BF16)
            ko_ref[h, rs, :] = (kh + bias_k[:, hs]).astype(BF16)
            if norm_out is not None:
                qn.append(max_sq(qh))
                kn.append(max_sq(kh))
        for src, dst in ((k2, kt_ref), (v[:, ps], vt_ref)):
            x_t = jnp.transpose(src)
            dst[2 * p, :, rs] = x_t[:ATTN_HEAD_DIM]
            dst[2 * p + 1, :, rs] = x_t[ATTN_HEAD_DIM:]
    if norm_out is not None:
        qn_ref, kn_ref, n = norm_out
        qn_ref[n:n + 1, :] = jnp.concatenate(qn, axis=1)
        kn_ref[n:n + 1, :] = jnp.concatenate(kn, axis=1)


QKV_HALVES = 2


def _qkv_prep_kernel(u_ref, w_ref, c_ref, sq_ref, sk_ref, oneq_ref, onek_ref, qo_ref, ko_ref, kt_ref, vt_ref,
                     qn_ref, kn_ref):
    rows = u_ref.shape[0] // QKV_HALVES
    for s in range(QKV_HALVES):
        rs = slice(s * rows, (s + 1) * rows)
        qkv = _dot(u_ref[rs, :], w_ref[...])
        q, k, v = (qkv[:, n * D_ATTN:(n + 1) * D_ATTN] for n in range(3))
        _prep_rows(q, k, v, c_ref[rs, :], sq_ref, sk_ref, oneq_ref, onek_ref, qo_ref, ko_ref, kt_ref, vt_ref, rs,
                   norm_out=(qn_ref, kn_ref, s))


def _qkv_prep(u, w_qkv, c_rows, consts, *, tile):
    nb, l, _ = c_rows.shape
    nt = l // tile
    sq, sk, one_q, one_k = consts
    c2 = lambda b, i: (0, 0)
    return pl.pallas_call(
        _qkv_prep_kernel,
        grid=(nb, nt),
        in_specs=[
            pl.BlockSpec((tile, D_MODEL), lambda b, i: (b * nt + i, 0)),
            pl.BlockSpec(w_qkv.shape, c2, pipeline_mode=pl.Buffered(1)),
            pl.BlockSpec((None, tile, ATTN_HEADS), lambda b, i: (b, i, 0)),
            pl.BlockSpec(sq.shape, c2), pl.BlockSpec(sk.shape, c2),
            pl.BlockSpec(one_q.shape, c2), pl.BlockSpec(one_k.shape, c2),
        ],
        out_specs=[
            pl.BlockSpec((None, ATTN_HEADS, tile, LANES), lambda b, i: (b, 0, i, 0)),
            pl.BlockSpec((None, ATTN_HEADS, tile, LANES), lambda b, i: (b, 0, i, 0)),
            pl.BlockSpec((None, ATTN_HEADS, ATTN_HEAD_DIM, tile), lambda b, i: (b, 0, 0, i)),
            pl.BlockSpec((None, ATTN_HEADS, ATTN_HEAD_DIM, tile), lambda b, i: (b, 0, 0, i)),
            pl.BlockSpec((None, None, QKV_HALVES, ATTN_HEADS), lambda b, i: (b, i, 0, 0)),
            pl.BlockSpec((None, None, QKV_HALVES, ATTN_HEADS), lambda b, i: (b, i, 0, 0)),
        ],
        out_shape=[
            jax.ShapeDtypeStruct((nb, ATTN_HEADS, l, LANES), BF16),
            jax.ShapeDtypeStruct((nb, ATTN_HEADS, l, LANES), BF16),
            jax.ShapeDtypeStruct((nb, ATTN_HEADS, ATTN_HEAD_DIM, l), F32),
            jax.ShapeDtypeStruct((nb, ATTN_HEADS, ATTN_HEAD_DIM, l), F32),
            jax.ShapeDtypeStruct((nb, nt, QKV_HALVES, ATTN_HEADS), F32),
            jax.ShapeDtypeStruct((nb, nt, QKV_HALVES, ATTN_HEADS), F32),
        ],
        compiler_params=_params("parallel", "parallel"),
        name="qkv_prep",
    )(u, w_qkv, c_rows, sq, sk, one_q, one_k)


def _attn_prep(proj3, c_rows, consts, *, nb, b_off, tile):
    l = c_rows.shape[1]
    col = lambda off: off // D_ATTN
    sq, sk, one_q, one_k = consts
    c2 = lambda b, i: (0, 0)
    return pl.pallas_call(
        _attn_prep_kernel,
        grid=(nb, l // tile),
        in_specs=[
            pl.BlockSpec((None, tile, D_ATTN), lambda b, i: (b + b_off, i, col(P_Q))),
            pl.BlockSpec((None, tile, D_ATTN), lambda b, i: (b + b_off, i, col(P_K))),
            pl.BlockSpec((None, tile, D_ATTN), lambda b, i: (b + b_off, i, col(P_V))),
            pl.BlockSpec((None, tile, ATTN_HEADS), lambda b, i: (b, i, 0)),
            pl.BlockSpec(sq.shape, c2), pl.BlockSpec(sk.shape, c2),
            pl.BlockSpec(one_q.shape, c2), pl.BlockSpec(one_k.shape, c2),
        ],
        out_specs=[
            pl.BlockSpec((None, ATTN_HEADS, tile, LANES), lambda b, i: (b, 0, i, 0)),
            pl.BlockSpec((None, ATTN_HEADS, tile, LANES), lambda b, i: (b, 0, i, 0)),
            pl.BlockSpec((None, ATTN_HEADS, ATTN_HEAD_DIM, tile), lambda b, i: (b, 0, 0, i)),
            pl.BlockSpec((None, ATTN_HEADS, ATTN_HEAD_DIM, tile), lambda b, i: (b, 0, 0, i)),
        ],
        out_shape=[
            jax.ShapeDtypeStruct((nb, ATTN_HEADS, l, LANES), BF16),
            jax.ShapeDtypeStruct((nb, ATTN_HEADS, l, LANES), BF16),
            jax.ShapeDtypeStruct((nb, ATTN_HEADS, ATTN_HEAD_DIM, l), F32),
            jax.ShapeDtypeStruct((nb, ATTN_HEADS, ATTN_HEAD_DIM, l), F32),
        ],
        compiler_params=_params("parallel", "parallel"),
        name="attn_prep",
    )(proj3, proj3, proj3, c_rows, sq, sk, one_q, one_k)


QBLK = 256
ACC_ROWS = ATTN_HEAD_DIM + 16


def _attn_prompt_kernel(it_ref, jt_ref, dead_ref, q_ref, k_ref, vt_ref, km_ref, vtm_ref, o_ref, m_sc, acc_sc,
                        *, meta_lo, meta_hi):
    t = pl.program_id(1)
    i = it_ref[t]
    j = jt_ref[t]
    dead = dead_ref[pl.program_id(0) * pl.num_programs(1) + t]

    def v_rows(v_t):
        return jnp.concatenate([v_t.astype(BF16), jnp.ones((ACC_ROWS - ATTN_HEAD_DIM, v_t.shape[1]), BF16)], axis=0)

    def softmax_stage(h, scores, first):
        m_old_all = None if first else m_sc[h]
        out = []
        for r, (s_t, v_aug) in enumerate(scores):
            qblk = s_t.shape[1]
            mx = jnp.max(s_t, axis=0, keepdims=True)
            if first:
                m_new, alpha = mx, None
            else:
                m_old = m_old_all[:, r * qblk:(r + 1) * qblk]
                m_new = jnp.maximum(m_old, mx)
                alpha = jnp.exp2(m_old - m_new)
            out.append((jnp.exp2(s_t - m_new).astype(BF16), v_aug, alpha, m_new))
        m_sc[h] = jnp.concatenate([o[3] for o in out], axis=1)
        return out

    def pv_stage(h, probs, first):
        acc_old_all = None if first else acc_sc[h]
        acc_out = []
        for r, (pr, v_aug, alpha, _) in enumerate(probs):
            qblk = pr.shape[1]
            pv = _dot(v_aug, pr)
            acc_out.append(pv if first else acc_old_all[:, r * qblk:(r + 1) * qblk] * alpha + pv)
        acc_sc[h] = jnp.concatenate(acc_out, axis=1)

    def run(score_fn, first, qblk):
        scores = lambda h: [score_fn(h, r, q_ref[h, r * qblk:(r + 1) * qblk, :]) for r in range(ATT_T // qblk)]
        s_next = scores(0)
        p_cur = softmax_stage(0, s_next, first)
        s_next = scores(1)
        for h in range(ATTN_HEADS):
            s_cur = s_next
            if h + 2 < ATTN_HEADS:
                s_next = scores(h + 2)
            p_prev = p_cur
            if h + 1 < ATTN_HEADS:
                p_cur = softmax_stage(h + 1, s_cur, first)
            pv_stage(h, p_prev, first)

    @pl.when(j == 0)
    def _():
        krow = lax.broadcasted_iota(jnp.int32, (km_ref.shape[1], 1), 0)
        is_meta = jnp.logical_and(krow >= meta_lo, krow < meta_hi)
        run(lambda h, r, q: (jnp.where(is_meta, _dot_nt(km_ref[h], q), -jnp.inf), v_rows(vtm_ref[h])), True, ATT_T)

    @pl.when(jnp.logical_and(j < i, dead == 0))
    def _():
        run(lambda h, r, q: (_dot_nt(k_ref[h], q), v_rows(vt_ref[h])), False, ATT_T)

    @pl.when(j == i)
    def _():
        def diag_scores(h, r, q):
            nk = (r + 1) * QBLK
            s_t = _dot_nt(k_ref[h, :nk, :], q)
            krow = lax.broadcasted_iota(jnp.int32, (nk, QBLK), 0)
            qcol = lax.broadcasted_iota(jnp.int32, (nk, QBLK), 1) + r * QBLK
            return jnp.where(krow <= qcol, s_t, -jnp.inf), v_rows(vt_ref[h, :, :nk])

        run(diag_scores, False, QBLK)
        d = ATTN_HEAD_DIM
        for p in range(HEAD_PAIRS):
            o_t = jnp.concatenate([acc_sc[h, :d, :] / acc_sc[h, d:d + 1, :] for h in (2 * p, 2 * p + 1)], axis=0)
            o_ref[:, p * LANES:(p + 1) * LANES] = jnp.transpose(o_t).astype(o_ref.dtype)


UNDERFLOW_LOG2 = 170.0
NORM_SLACK = 1.01


def _dead_tiles(qn2, kn2, c_rows):
    an = jnp.swapaxes(jnp.sqrt(jnp.max(qn2, axis=2)), 1, 2) * NORM_SLACK
    kn = jnp.swapaxes(jnp.sqrt(jnp.max(kn2, axis=2)), 1, 2) * NORM_SLACK
    c_first = c_rows[:, :, ::ATT_T]
    c_last = c_rows[:, :, ATT_T - 1::ATT_T]
    gap = (an[:, :, :, None] * (kn[:, :, None, :] + kn[:, :, :, None])
           + (c_first[:, :, :, None] - c_last[:, :, None, :]) * LOG2E)
    return jnp.all(gap < -UNDERFLOW_LOG2, axis=1)


def _attn_prompt(q_aug, k_aug, v_t, km_aug, vm_t, dead_ij, *, meta_lo):
    nb, _, l, _ = q_aug.shape
    nt = l // ATT_T
    it = np.array([i for i in range(nt) for _ in range(i + 1)], np.int32)
    jt = np.array([j for i in range(nt) for j in range(i + 1)], np.int32)
    dead = jnp.logical_and(dead_ij[:, it, jt], jnp.asarray(jt < it)[None]).astype(jnp.int32).reshape(-1)
    mrows = km_aug.shape[2]
    grid_spec = pltpu.PrefetchScalarGridSpec(
        num_scalar_prefetch=3,
        grid=(nb, len(it)),
        in_specs=[
            pl.BlockSpec((None, ATTN_HEADS, ATT_T, LANES), lambda b, t, it, jt, dd: (b, 0, it[t], 0)),
            pl.BlockSpec((None, ATTN_HEADS, ATT_T, LANES), lambda b, t, it, jt, dd: (b, 0, jt[t], 0)),
            pl.BlockSpec((None, ATTN_HEADS, ATTN_HEAD_DIM, ATT_T), lambda b, t, it, jt, dd: (b, 0, 0, jt[t])),
            pl.BlockSpec((None, ATTN_HEADS, mrows, LANES), lambda b, t, it, jt, dd: (0, 0, 0, 0)),
            pl.BlockSpec((None, ATTN_HEADS, ATTN_HEAD_DIM, mrows), lambda b, t, it, jt, dd: (0, 0, 0, 0)),
        ],
        out_specs=pl.BlockSpec((None, ATT_T, D_ATTN), lambda b, t, it, jt, dd: (b, it[t], 0)),
        scratch_shapes=[
            pltpu.VMEM((ATTN_HEADS, 1, ATT_T), F32),
            pltpu.VMEM((ATTN_HEADS, ACC_ROWS, ATT_T), F32),
        ],
    )
    return pl.pallas_call(
        functools.partial(_attn_prompt_kernel, meta_lo=meta_lo, meta_hi=meta_lo + N_META),
        grid_spec=grid_spec,
        out_shape=jax.ShapeDtypeStruct((nb, l, D_ATTN), BF16),
        compiler_params=_params("parallel", "arbitrary"),
        name="attn_prompt",
    )(jnp.asarray(it), jnp.asarray(jt), dead, q_aug, k_aug, v_t, km_aug, vm_t)


SAMPLE_HEADS = 8


def _attn_sample_kernel(q_ref, kn_ref, vn_ref, km_ref, vm_ref, kct_ref, vct_ref, cq_ref, ckm_ref, ckc_ref,
                        ckn_ref, o_ref):
    T = CHUNK
    D = ATTN_HEAD_DIM
    extra = ACC_ROWS - D
    row = lax.broadcasted_iota(jnp.int32, (T, T), 0)
    col = lax.broadcasted_iota(jnp.int32, (T, T), 1)

    def v_cols(v):
        return jnp.concatenate([v.astype(BF16), jnp.ones((v.shape[0], extra), BF16)], axis=1)

    def v_rows(v_t):
        return jnp.concatenate([v_t.astype(BF16), jnp.ones((extra, v_t.shape[1]), BF16)], axis=0)

    scores = []
    for hh in range(SAMPLE_HEADS):
        hs = slice(hh * D, (hh + 1) * D)
        qh = (q_ref[:, hs] * (D ** -0.5 * LOG2E)).astype(BF16)
        t_m = _dot_nt(qh, km_ref[:, hs].astype(BF16)) - ckm_ref[hh:hh + 1, :] * LOG2E
        t_c = _dot(qh, kct_ref[hh].astype(BF16)) - ckc_ref[hh:hh + 1, :] * LOG2E
        t_n = _dot_nt(qh, kn_ref[:, hs].astype(BF16)) - ckn_ref[hh:hh + 1, :] * LOG2E
        scores.append((t_m, t_c, jnp.where(col <= row, t_n, -jnp.inf)))
    outs = []
    for hh, (t_m, t_c, t_n) in enumerate(scores):
        hs = slice(hh * D, (hh + 1) * D)
        cq = cq_ref[hh] * LOG2E
        m = cq + jnp.maximum(jnp.maximum(jnp.max(t_m, axis=1, keepdims=True), jnp.max(t_c, axis=1, keepdims=True)),
                             jnp.max(t_n, axis=1, keepdims=True))
        shift = cq - m
        acc = (_dot(jnp.exp2(t_m + shift).astype(BF16), v_cols(vm_ref[:, hs]))
               + _dot_nt(jnp.exp2(t_c + shift).astype(BF16), v_rows(vct_ref[hh]))
               + _dot(jnp.exp2(t_n + shift).astype(BF16), v_cols(vn_ref[:, hs])))
        outs.append(acc[:, :D] / acc[:, D:D + 1])
    o_ref[...] = jnp.concatenate(outs, axis=1).astype(o_ref.dtype)


def _attn_sample(proj3, proj2, meta_blk, cache_k, cache_v, cq, ckm_t, ckc_t, ckn_t, *, nb):
    past = cache_k.shape[3]
    hg = SAMPLE_HEADS
    wg = hg * ATTN_HEAD_DIM
    groups = ATTN_HEADS // hg
    lb = lambda off: off // wg
    grp = lambda off: (lambda b, p: (b, 0, lb(off) + p))
    ckspec = lambda n: pl.BlockSpec((None, hg, n), lambda b, p: (b * groups + p, 0, 0))
    return pl.pallas_call(
        _attn_sample_kernel,
        grid=(nb, groups),
        in_specs=[
            pl.BlockSpec((None, CHUNK, wg), grp(P_Q)),
            pl.BlockSpec((None, CHUNK, wg), grp(P_K)),
            pl.BlockSpec((None, CHUNK, wg), grp(P_V)),
            pl.BlockSpec((N_META, wg), lambda b, p: (meta_blk, lb(P_K) + p)),
            pl.BlockSpec((N_META, wg), lambda b, p: (meta_blk, lb(P_V) + p)),
            pl.BlockSpec((None, hg, ATTN_HEAD_DIM, past), lambda b, p: (b, p, 0, 0)),
            pl.BlockSpec((None, hg, ATTN_HEAD_DIM, past), lambda b, p: (b, p, 0, 0)),
            pl.BlockSpec((None, hg, CHUNK, 1), lambda b, p: (b, p, 0, 0)),
            ckspec(N_META), ckspec(past), ckspec(CHUNK),
        ],
        out_specs=pl.BlockSpec((None, CHUNK, wg), lambda b, p: (b, 0, p)),
        out_shape=jax.ShapeDtypeStruct((nb, CHUNK, D_ATTN), BF16),
        compiler_params=_params("parallel", "parallel"),
        name="attn_sample",
    )(proj3, proj3, proj3, proj2, proj2, cache_k, cache_v, cq, ckm_t, ckc_t, ckn_t)


def _merge_kernel(os_ref, oa_ref, gs_ref, ga_ref, h_ref, wbs_ref, wba_ref, wo_ref, o_ref):
    bs = _dot(os_ref[...], wbs_ref[...])
    ba = _dot(oa_ref[...], wba_ref[...])
    merged = jax.nn.sigmoid(gs_ref[...]) * bs + jax.nn.sigmoid(ga_ref[...]) * ba
    o_ref[...] = h_ref[...] + _dot(merged.astype(BF16), wo_ref[...])


def _merge(o_ssm, o_attn, proj, h, w_bs, w_ba, w_o, *, tm, gs_off, ga_off):
    rows = o_ssm.shape[0]
    const = lambda i: (0, 0)
    return pl.pallas_call(
        _merge_kernel,
        grid=(rows // tm,),
        in_specs=[
            pl.BlockSpec((tm, D_SSM), lambda i: (i, 0)),
            pl.BlockSpec((tm, D_ATTN), lambda i: (i, 0)),
            pl.BlockSpec((tm, D_MODEL), lambda i: (i, gs_off // D_MODEL)),
            pl.BlockSpec((tm, D_MODEL), lambda i: (i, ga_off // D_MODEL)),
            pl.BlockSpec((tm, D_MODEL), lambda i: (i, 0)),
            pl.BlockSpec((D_SSM, D_MODEL), const),
            pl.BlockSpec((D_ATTN, D_MODEL), const),
            pl.BlockSpec((D_MODEL, D_MODEL), const),
        ],
        out_specs=pl.BlockSpec((tm, D_MODEL), lambda i: (i, 0)),
        out_shape=jax.ShapeDtypeStruct((rows, D_MODEL), F32),
        compiler_params=_params("parallel"),
        name="merge",
    )(o_ssm, o_attn, proj, proj, h, w_bs, w_ba, w_o)


def kernel(x_prompt, x_sample, cache_k, cache_v, cache_logf, state_ssm, state_conv, meta_tokens, norm_ffn1, ffn1_w_in, ffn1_w_out, norm_mix, w_in, conv_w, conv_b, dt_bias, a_log, d_skip, f_bias, ssm_norm, w_br_ssm, w_br_attn, w_out, norm_ffn2, ffn2_w_in, ffn2_w_out, norm_final):
    assert norm_ffn1.shape[0] == 1, "single-layer trunk"
    bp, seq, _ = x_prompt.shape
    bs, dseq, _ = x_sample.shape
    assert dseq == CHUNK and seq % ATT_T == 0
    n_s = bs * dseq
    rows_small = n_s + 2 * CHUNK
    meta_row0 = n_s + CHUNK - N_META
    meta_blk = meta_row0 // N_META

    row = lambda v: v.reshape(1, -1).astype(F32)
    w = w_in[0]
    w_main = jnp.concatenate([w[:, OFF_XBC:OFF_DT], w[:, :OFF_XBC], w[:, OFF_Q:OFF_F], w[:, OFF_GS:]],
                             axis=1).astype(BF16)
    pad_small = LANES - SSM_HEADS - ATTN_HEADS
    w_small = jnp.concatenate([w[:, OFF_DT:OFF_Q], w[:, OFF_F:OFF_GS], jnp.zeros((D_MODEL, pad_small), F32)],
                              axis=1).astype(BF16)
    b_small = jnp.concatenate([dt_bias[0], f_bias[0], jnp.zeros((pad_small,), F32)]).reshape(1, LANES)
    w1i, w1o = ffn1_w_in[0].astype(BF16), ffn1_w_out[0].astype(BF16)
    w2i, w2o = ffn2_w_in[0].astype(BF16), ffn2_w_out[0].astype(BF16)
    w_bs, w_ba, w_o = w_br_ssm[0].astype(BF16), w_br_attn[0].astype(BF16), w_out[0].astype(BF16)
    expand = jnp.asarray(np.tile(np.repeat(np.eye(SSM_HEADS, dtype=np.float32), SSM_HEAD_DIM, axis=1), (3, 1)),
                         BF16)
    ssd_consts = (conv_w[0], row(conv_b[0]), row(a_log[0]), row(jnp.repeat(d_skip[0], SSM_HEAD_DIM)),
                  row(ssm_norm[0]), expand)

    def front(x, w_proj, tm_ffn, tm_proj):
        h1, u = _ffn(x, row(norm_ffn1[0]), w1i, w1o, row(norm_mix[0]), tm=tm_ffn, emit_h=True, u_dtype=BF16)
        return h1, u, _proj(u, w_proj, tm=tm_proj), _small(u, w_small, b_small, tm=tm_proj)

    w_qkv = w_main[:, P_Q:P_GS]
    w_prompt = jnp.concatenate([w_main[:, :P_Q], w_main[:, P_GS:]], axis=1)
    xp = x_prompt.reshape(bp * seq, D_MODEL)
    h1_p, u_p, pm_p, ps_p = front(xp, w_prompt, 512, 1024)
    x_small = jnp.concatenate([x_sample.reshape(n_s, D_MODEL), jnp.zeros((CHUNK - N_META, D_MODEL), F32),
                               meta_tokens, jnp.zeros((CHUNK, D_MODEL), F32)], axis=0)
    h1_s, _, pm_s, ps_s = front(x_small, w_main, rows_small // 2, rows_small // 2)

    pm_p3 = pm_p.reshape(bp, seq, w_prompt.shape[1])
    ps_p3 = ps_p.reshape(bp, seq, LANES)
    pm_s3 = pm_s.reshape(rows_small // CHUNK, CHUNK, P_TOTAL)
    ps_s3 = ps_s.reshape(rows_small // CHUNK, CHUNK, LANES)

    zero_state = jnp.zeros((1, D_STATE, D_SSM), F32)
    zero_hist = jnp.zeros((1, CONV_W - 1, D_CONV), F32)
    _, st_m, hist_m = _ssd(pm_s3, ps_s3, zero_state, zero_hist, *ssd_consts,
                           nb=1, b_off=bs, shared_init=True, n_pad=CHUNK - N_META, cps=1)
    o_ssm_p, st_p, hist_p = _ssd(pm_p3, ps_p3, st_m, hist_m, *ssd_consts,
                                 nb=bp, b_off=0, shared_init=True, n_pad=0, cps=SSD_CHUNKS_PER_STEP)
    st_s0 = jnp.transpose(state_ssm[0].reshape(bs, D_SSM, D_STATE), (0, 2, 1))
    o_ssm_s, st_s, hist_s = _ssd(pm_s3, ps_s3, st_s0, state_conv[0], *ssd_consts,
                                 nb=bs, b_off=0, shared_init=False, n_pad=0, cps=1)

    t_last = lambda a: jnp.swapaxes(a, -1, -2)
    lf_m = t_last(ps_s[meta_row0:meta_row0 + N_META, S_F:S_F + ATTN_HEADS])
    lf_p = t_last(ps_p3[:, :, S_F:S_F + ATTN_HEADS])
    lf_s = ps_s[:n_s, S_F:S_F + ATTN_HEADS].reshape(bs, dseq, ATTN_HEADS)
    lf_all_p = jnp.concatenate([jnp.broadcast_to(lf_m[None], (bp, ATTN_HEADS, N_META)), lf_p], axis=2)
    lf_all_s = jnp.concatenate([jnp.broadcast_to(lf_m[None], (bs, ATTN_HEADS, N_META)),
                                t_last(cache_logf[0].astype(F32)), t_last(lf_s)], axis=2)
    past = cache_logf.shape[2]
    c_p = _cum_logf(lf_all_p)
    c_s = _cum_logf(lf_all_s)

    prep_consts = _bias_placement()
    q_aug, k_aug, k_t, v_t, qn2, kn2 = _qkv_prep(u_p, w_qkv, t_last(c_p[:, :, N_META:]), prep_consts, tile=ATT_T)
    meta_tile = 2 * CHUNK
    meta_lo = meta_row0 - n_s
    c_meta = jnp.pad(t_last(c_p[:1, :, :N_META]), ((0, 0), (meta_lo, meta_tile - meta_lo - N_META), (0, 0)))
    _, km_aug, km_t, vm_t = _attn_prep(pm_s.reshape(rows_small // meta_tile, meta_tile, P_TOTAL), c_meta,
                                       prep_consts, nb=1, b_off=n_s // meta_tile, tile=meta_tile)
    o_attn_p = _attn_prompt(q_aug, k_aug, v_t, km_aug, vm_t, _dead_tiles(qn2, kn2, c_p[:, :, N_META:]),
                            meta_lo=meta_lo)
    per_pair = lambda c: c.reshape(bs * ATTN_HEADS // SAMPLE_HEADS, SAMPLE_HEADS, c.shape[2])
    feat_major = lambda a: jnp.transpose(a, (0, 2, 3, 1))
    o_attn_s = _attn_sample(pm_s3, pm_s, meta_blk, feat_major(cache_k[0]), feat_major(cache_v[0]),
                            c_s[:, :, N_META + past:, None],
                            per_pair(c_s[:, :, :N_META]), per_pair(c_s[:, :, N_META:N_META + past]),
                            per_pair(c_s[:, :, N_META + past:]), nb=bs)

    def back(o_ssm, o_attn, pm, h1, tm, gs_off, ga_off):
        h2 = _merge(o_ssm, o_attn, pm, h1, w_bs, w_ba, w_o, tm=tm, gs_off=gs_off, ga_off=ga_off)
        return _ffn(h2, row(norm_ffn2[0]), w2i, w2o, row(norm_final), tm=tm, emit_h=False, u_dtype=F32)[0]

    y_prompt = back(o_ssm_p.reshape(bp * seq, D_SSM), o_attn_p.reshape(bp * seq, D_ATTN), pm_p, h1_p, 512,
                    PP_GS, PP_GA)
    y_sample = back(o_ssm_s.reshape(n_s, D_SSM), o_attn_s.reshape(n_s, D_ATTN), pm_s, h1_s, 512, P_GS, P_GA)

    heads = lambda a: a.reshape(1, a.shape[0], a.shape[1], ATTN_HEADS, ATTN_HEAD_DIM)

    def with_meta(x_t, m_t):
        m_t = jnp.broadcast_to(m_t[:, :, :, meta_lo:meta_lo + N_META], (bp, ATTN_HEADS, ATTN_HEAD_DIM, N_META))
        return jnp.transpose(jnp.concatenate([m_t, x_t], axis=3), (0, 3, 1, 2))[None]

    k_prompt = with_meta(k_t, km_t)
    v_prompt = with_meta(v_t, vm_t)
    logf_prompt = t_last(lf_all_p)
    k_sample = heads(pm_s[:n_s, P_K:P_K + D_ATTN].reshape(bs, dseq, D_ATTN))
    v_sample = heads(pm_s[:n_s, P_V:P_V + D_ATTN].reshape(bs, dseq, D_ATTN))
    state_out = lambda st: jnp.transpose(st, (0, 2, 1)).reshape(1, st.shape[0], SSM_HEADS, SSM_HEAD_DIM, D_STATE)
    return (y_prompt.reshape(bp, seq, D_MODEL), y_sample.reshape(bs, dseq, D_MODEL),
            k_prompt, v_prompt, logf_prompt[None], state_out(st_p), hist_p[None],
            k_sample, v_sample, lf_s[None], state_out(st_s), hist_s[None])
```

```python
import functools

import numpy as np
import jax
import jax.numpy as jnp
from jax import lax
from jax.experimental import pallas as pl
from jax.experimental.pallas import tpu as pltpu

F32 = jnp.float32
BF16 = jnp.bfloat16

D_MODEL = 1024
N_META = 16
CHUNK = 64
EPS = 1e-6
D_SSM = 2048
SSM_HEADS = 32
SSM_HEAD_DIM = 64
SSM_GROUPS = 8
D_STATE = 128
CONV_W = 4
D_CONV = D_SSM + 2 * SSM_GROUPS * D_STATE
ATTN_HEADS = 16
ATTN_HEAD_DIM = 64
D_ATTN = ATTN_HEADS * ATTN_HEAD_DIM
D_FF = 2816
HEAD_PAIRS = ATTN_HEADS // 2
LANES = 128
GROUP_W = D_SSM // SSM_GROUPS

OFF_XBC = D_SSM
OFF_DT = OFF_XBC + D_CONV
OFF_Q = OFF_DT + SSM_HEADS
OFF_K = OFF_Q + D_ATTN
OFF_V = OFF_K + D_ATTN
OFF_F = OFF_V + D_ATTN
OFF_GS = OFF_F + ATTN_HEADS
OFF_GA = OFF_GS + D_MODEL

P_XBC = 0
P_Z = D_CONV
P_Q = P_Z + D_SSM
P_K = P_Q + D_ATTN
P_V = P_K + D_ATTN
P_GS = P_V + D_ATTN
P_GA = P_GS + D_MODEL
P_TOTAL = P_GA + D_MODEL
S_DT = 0
S_F = SSM_HEADS

VMEM_LIMIT = 56 * 1024 * 1024

PROJ_COL_TILES = 4
PP_GS = P_Q
PP_GA = PP_GS + D_MODEL
ATT_T = 512
CUMSUM_BLK = 256


def _dot(a, b):
    return jnp.dot(a, b, preferred_element_type=F32)


def _dot_nt(a, b):
    return lax.dot_general(a, b, (((1,), (1,)), ((), ())), preferred_element_type=F32)


def _dot_tn(a, b):
    return lax.dot_general(a, b, (((0,), (0,)), ((), ())), preferred_element_type=F32)


def _split3(x):
    hi = x.astype(BF16)
    r = x - hi.astype(F32)
    mid = r.astype(BF16)
    lo = (r - mid.astype(F32)).astype(BF16)
    return hi, mid, lo


def _rmsnorm(x, g):
    return x * lax.rsqrt(jnp.mean(x * x, axis=-1, keepdims=True) + EPS) * g


def _softplus(x):
    return jnp.maximum(x, 0.0) + jnp.log1p(jnp.exp(-jnp.abs(x)))


def _params(*sem):
    return pltpu.CompilerParams(dimension_semantics=sem, vmem_limit_bytes=VMEM_LIMIT)


FFN_HALVES = 2


def _ffn_kernel(x_ref, g1_ref, wi_ref, wo_ref, g2_ref, *out_refs, emit_h):
    h_ref, u_ref = out_refs if emit_h else (None,) + out_refs
    rows = x_ref.shape[0] // FFN_HALVES
    for s in range(FFN_HALVES):
        rs = slice(s * rows, (s + 1) * rows)
        x = x_ref[rs, :]
        xn = _rmsnorm(x, g1_ref[...]).astype(BF16)
        a = _dot(xn, wi_ref[:, :D_FF])
        b = _dot(xn, wi_ref[:, D_FF:])
        g = (a * jax.nn.sigmoid(a) * b).astype(BF16)
        h = x + 0.5 * _dot(g, wo_ref[...])
        if emit_h:
            h_ref[rs, :] = h
        u_ref[rs, :] = _rmsnorm(h, g2_ref[...]).astype(u_ref.dtype)


def _ffn(x, g1, w_in, w_out, g2, *, tm, emit_h, u_dtype):
    rows = x.shape[0]
    row_spec = pl.BlockSpec((tm, D_MODEL), lambda i: (i, 0))
    vec_spec = pl.BlockSpec((1, D_MODEL), lambda i: (0, 0))
    resident = lambda shape: pl.BlockSpec(shape, lambda i: (0, 0), pipeline_mode=pl.Buffered(1))
    out_shape = [jax.ShapeDtypeStruct((rows, D_MODEL), u_dtype)]
    out_specs = [row_spec]
    if emit_h:
        out_shape = [jax.ShapeDtypeStruct((rows, D_MODEL), F32)] + out_shape
        out_specs = [row_spec] + out_specs
    return pl.pallas_call(
        functools.partial(_ffn_kernel, emit_h=emit_h),
        grid=(rows // tm,),
        in_specs=[row_spec, vec_spec, resident((D_MODEL, 2 * D_FF)), resident((D_FF, D_MODEL)), vec_spec],
        out_specs=out_specs,
        out_shape=out_shape,
        compiler_params=_params("parallel"),
        name="ffn",
    )(x, g1, w_in, w_out, g2)


def _proj_kernel(u_ref, w_ref, o_ref):
    o_ref[...] = _dot(u_ref[...], w_ref[...])


def _proj(u, w, *, tm):
    rows = u.shape[0]
    width = w.shape[1]
    tn = width // PROJ_COL_TILES
    return pl.pallas_call(
        _proj_kernel,
        grid=(PROJ_COL_TILES, rows // tm),
        in_specs=[pl.BlockSpec((tm, D_MODEL), lambda j, i: (i, 0)),
                  pl.BlockSpec((D_MODEL, tn), lambda j, i: (0, j))],
        out_specs=pl.BlockSpec((tm, tn), lambda j, i: (i, j)),
        out_shape=jax.ShapeDtypeStruct((rows, width), F32),
        compiler_params=_params("parallel", "parallel"),
        name="proj",
    )(u, w)


def _small_kernel(u_ref, w_ref, b_ref, o_ref):
    x = _dot(u_ref[...], w_ref[...]) + b_ref[...]
    lane = lax.broadcasted_iota(jnp.int32, (1, LANES), 1)
    sgn = jnp.where(lane < S_F, 1.0, -1.0)
    o_ref[...] = sgn * _softplus(sgn * x)


def _small(u, w, b, *, tm):
    rows = u.shape[0]
    return pl.pallas_call(
        _small_kernel,
        grid=(rows // tm,),
        in_specs=[pl.BlockSpec((tm, D_MODEL), lambda i: (i, 0)),
                  pl.BlockSpec((D_MODEL, LANES), lambda i: (0, 0)),
                  pl.BlockSpec((1, LANES), lambda i: (0, 0))],
        out_specs=pl.BlockSpec((tm, LANES), lambda i: (i, 0)),
        out_shape=jax.ShapeDtypeStruct((rows, LANES), F32),
        compiler_params=_params("parallel"),
        name="small",
    )(u, w, b)


HIST0 = 8 - (CONV_W - 1)
SSD_CHUNKS_PER_STEP = 8


def _ssd_chunk(rs, xbc_ref, z_ref, dt_ref, cw_ref, cb_ref, alog_ref, dskip_ref, norm_ref, e_ref, o_ref,
               st_sc, xp_sc, y_sc, n_pad):
    T = CHUNK
    xraw = xbc_ref[rs, :]
    xp_sc[8:8 + T, :] = xraw
    cw = cw_ref[...]
    xp = xp_sc[...]
    conv = cb_ref[...]
    for j in range(CONV_W - 1):
        conv = conv + cw[j:j + 1] * pltpu.roll(xp, CONV_W - 1 - j, axis=0)[8:8 + T, :]
    conv = conv + cw[3:4] * xraw
    xp_sc[HIST0:8, :] = xraw[T - (CONV_W - 1):T, :]
    xc = conv * jax.nn.sigmoid(conv)
    dtv = dt_ref[rs, S_DT:S_DT + SSM_HEADS]
    if n_pad:
        valid = lax.broadcasted_iota(jnp.int32, (T, 1), 0) >= n_pad
        xc = jnp.where(valid, xc, 0.0)
        dtv = jnp.where(valid, dtv, 0.0)

    a = -jnp.exp(alog_ref[...])
    adt = a * dtv
    row = lax.broadcasted_iota(jnp.int32, (T, T), 0)
    col = lax.broadcasted_iota(jnp.int32, (T, T), 1)
    causal = col <= row
    tril = jnp.where(causal, 1.0, 0.0).astype(BF16)
    r32 = lax.broadcasted_iota(jnp.int32, (SSM_HEADS, SSM_HEADS), 0)
    c32 = lax.broadcasted_iota(jnp.int32, (SSM_HEADS, SSM_HEADS), 1)
    eye = jnp.where(r32 == c32, 1.0, 0.0).astype(BF16)
    expand3 = e_ref[...]

    def expand(v):
        return _dot(jnp.concatenate(_split3(v), axis=1), expand3)

    a_cs = sum(_dot(tril, piece) for piece in _split3(adt))
    a_cs_t = sum(_dot_nt(eye, piece) for piece in _split3(a_cs))
    a_last = a_cs[T - 1:T, :]
    dt_e = expand(dtv)
    wdec_e = expand(jnp.exp(a_last - a_cs) * dtv)
    eacs = expand(jnp.exp(a_cs))
    cdec = eacs[T - 1:T, :]

    xs = xc[:, :D_SSM]
    bm = xc[:, D_SSM:D_SSM + SSM_GROUPS * D_STATE].astype(BF16)
    cm = xc[:, D_SSM + SSM_GROUPS * D_STATE:].astype(BF16)
    xd = xs * dt_e
    xdw = (xs * wdec_e).astype(BF16)
    lane = lax.broadcasted_iota(jnp.int32, (1, D_SSM), 1)
    lo = (lane & SSM_HEAD_DIM) == 0
    xd_lo = jnp.where(lo, xd, 0.0).astype(BF16)
    xd_hi = jnp.where(lo, 0.0, xd).astype(BF16)
    lo2 = lax.broadcasted_iota(jnp.int32, (1, LANES), 1) < SSM_HEAD_DIM
    step2 = lax.broadcasted_iota(jnp.int32, (T, LANES), 1) & (SSM_HEAD_DIM - 1)
    causal2 = step2 <= lax.broadcasted_iota(jnp.int32, (T, LANES), 0)

    for g in range(SSM_GROUPS):
        gs = slice(g * GROUP_W, (g + 1) * GROUP_W)
        cmg = cm[:, g * D_STATE:(g + 1) * D_STATE]
        bmg = bm[:, g * D_STATE:(g + 1) * D_STATE]
        cb2 = _dot_nt(cmg, jnp.concatenate([bmg, bmg], axis=0))
        stg = st_sc[:, gs]
        yoff = _dot(cmg, stg.astype(BF16))
        for rr in range(2):
            pair = g * 2 + rr
            ps = slice(pair * LANES, (pair + 1) * LANES)
            ha, hb = 2 * pair, 2 * pair + 1
            seg = (jnp.where(lo2, a_cs[:, ha:ha + 1], a_cs[:, hb:hb + 1])
                   - jnp.concatenate([a_cs_t[ha:ha + 1, :], a_cs_t[hb:hb + 1, :]], axis=1))
            m2 = (cb2 * jnp.exp(jnp.where(causal2, seg, -jnp.inf))).astype(BF16)
            ydiag = _dot(m2, jnp.concatenate([xd_lo[:, ps], xd_hi[:, ps]], axis=0))
            y_sc[:, ps] = ydiag + yoff[:, rr * LANES:(rr + 1) * LANES] * eacs[:, ps]
        st_sc[:, gs] = stg * cdec[:, gs] + _dot_tn(bmg, xdw[:, gs])

    y = y_sc[...] + dskip_ref[...] * xs
    zz = z_ref[rs, :]
    y = y * (zz * jax.nn.sigmoid(zz))
    for g in range(SSM_GROUPS):
        gs = slice(g * GROUP_W, (g + 1) * GROUP_W)
        yg = y[:, gs]
        yg = yg * lax.rsqrt(jnp.mean(yg * yg, axis=-1, keepdims=True) + EPS)
        o_ref[rs, gs] = (yg * norm_ref[:, gs]).astype(o_ref.dtype)


def _ssd_kernel(xbc_ref, z_ref, dt_ref, s0_ref, h0_ref, cw_ref, cb_ref, alog_ref, dskip_ref, norm_ref,
                e_ref, o_ref, sT_ref, hT_ref, st_sc, xp_sc, y_sc, *, n_pad, cps):
    c = pl.program_id(1)

    @pl.when(c == 0)
    def _():
        st_sc[...] = s0_ref[...]
        xp_sc[HIST0:8, :] = h0_ref[...]

    for ci in range(cps):
        _ssd_chunk(slice(ci * CHUNK, (ci + 1) * CHUNK), xbc_ref, z_ref, dt_ref, cw_ref, cb_ref, alog_ref,
                   dskip_ref, norm_ref, e_ref, o_ref, st_sc, xp_sc, y_sc, n_pad)

    @pl.when(c == pl.num_programs(1) - 1)
    def _():
        sT_ref[...] = st_sc[...]
        hT_ref[...] = xp_sc[HIST0:8, :]


def _ssd(proj3, small3, s0, h0, cw, cb, alog, dskip, norm, expand, *, nb, b_off, shared_init, n_pad, cps):
    rows = cps * CHUNK
    nc = proj3.shape[1] // rows
    init = (lambda b, c: (0, 0, 0)) if shared_init else (lambda b, c: (b, 0, 0))
    const = lambda b, c: (0, 0)
    return pl.pallas_call(
        functools.partial(_ssd_kernel, n_pad=n_pad, cps=cps),
        grid=(nb, nc),
        in_specs=[
            pl.BlockSpec((None, rows, D_CONV), lambda b, c: (b + b_off, c, P_XBC // D_CONV)),
            pl.BlockSpec((None, rows, D_SSM), lambda b, c: (b + b_off, c, P_Z // D_SSM)),
            pl.BlockSpec((None, rows, LANES), lambda b, c: (b + b_off, c, 0)),
            pl.BlockSpec((None, D_STATE, D_SSM), init),
            pl.BlockSpec((None, CONV_W - 1, D_CONV), init),
            pl.BlockSpec((CONV_W, D_CONV), const),
            pl.BlockSpec((1, D_CONV), const),
            pl.BlockSpec((1, SSM_HEADS), const),
            pl.BlockSpec((1, D_SSM), const),
            pl.BlockSpec((1, D_SSM), const),
            pl.BlockSpec((3 * SSM_HEADS, D_SSM), const),
        ],
        out_specs=[
            pl.BlockSpec((None, rows, D_SSM), lambda b, c: (b, c, 0)),
            pl.BlockSpec((None, D_STATE, D_SSM), lambda b, c: (b, 0, 0)),
            pl.BlockSpec((None, CONV_W - 1, D_CONV), lambda b, c: (b, 0, 0)),
        ],
        out_shape=[
            jax.ShapeDtypeStruct((nb, nc * rows, D_SSM), BF16),
            jax.ShapeDtypeStruct((nb, D_STATE, D_SSM), F32),
            jax.ShapeDtypeStruct((nb, CONV_W - 1, D_CONV), F32),
        ],
        scratch_shapes=[
            pltpu.VMEM((D_STATE, D_SSM), F32),
            pltpu.VMEM((8 + CHUNK, D_CONV), F32),
            pltpu.VMEM((CHUNK, D_SSM), F32),
        ],
        compiler_params=_params("parallel", "arbitrary"),
        name="ssd",
    )(proj3, proj3, small3, s0, h0, cw, cb, alog, dskip, norm, expand)


def _cumsum_kernel(x_ref, o_ref):
    n = x_ref.shape[-1] // CUMSUM_BLK
    r = lax.broadcasted_iota(jnp.int32, (CUMSUM_BLK, CUMSUM_BLK), 0)
    c = lax.broadcasted_iota(jnp.int32, (CUMSUM_BLK, CUMSUM_BLK), 1)
    upper = jnp.where(r <= c, 1.0, 0.0).astype(BF16)
    carry = jnp.zeros((ATTN_HEADS, 1), F32)
    for i in range(n):
        blk = slice(i * CUMSUM_BLK, (i + 1) * CUMSUM_BLK)
        cs = sum(_dot(piece, upper) for piece in _split3(x_ref[:, blk])) + carry
        o_ref[:, blk] = cs
        carry = cs[:, CUMSUM_BLK - 1:CUMSUM_BLK]


def _cumsum_t(x_t):
    b, h, l = x_t.shape
    spec = pl.BlockSpec((None, h, l), lambda i: (i, 0, 0))
    return pl.pallas_call(
        _cumsum_kernel, grid=(b,), in_specs=[spec], out_specs=spec,
        out_shape=jax.ShapeDtypeStruct(x_t.shape, F32),
        compiler_params=_params("parallel"), name="cumsum",
    )(x_t)


def _cum_logf(lf_t):
    l = lf_t.shape[2]
    lp = -(-l // CUMSUM_BLK) * CUMSUM_BLK
    return _cumsum_t(jnp.pad(lf_t, ((0, 0), (0, 0), (0, lp - l))))[:, :, :l]


BIAS_PIECES = 3
LOG2E = 1.4426950408889634


def _bias_lane0(head):
    return ATTN_HEAD_DIM if head % 2 == 0 else 0


def _bias_placement():
    sq = np.zeros((BIAS_PIECES, ATTN_HEADS, ATTN_HEADS * LANES), np.float32)
    sk = np.zeros_like(sq)
    one_q = np.zeros((1, ATTN_HEADS * LANES), np.float32)
    one_k = np.zeros_like(one_q)
    for h in range(ATTN_HEADS):
        base = h * LANES + _bias_lane0(h)
        for piece in range(BIAS_PIECES):
            sk[piece, h, base + piece] = -1.0
            one_q[0, base + piece] = 1.0
            sq[piece, h, base + BIAS_PIECES + piece] = 1.0
            one_k[0, base + BIAS_PIECES + piece] = 1.0
    flat = lambda a: jnp.asarray(a.reshape(BIAS_PIECES * ATTN_HEADS, ATTN_HEADS * LANES), BF16)
    return flat(sq), flat(sk), jnp.asarray(one_q), jnp.asarray(one_k)


def _attn_prep_kernel(q_ref, k_ref, v_ref, c_ref, sq_ref, sk_ref, oneq_ref, onek_ref, qo_ref, ko_ref, kt_ref,
                      vt_ref):
    _prep_rows(q_ref[...], k_ref[...], v_ref[...], c_ref[...], sq_ref, sk_ref, oneq_ref, onek_ref,
               qo_ref, ko_ref, kt_ref, vt_ref, slice(None))


def _prep_rows(q, k, v, c, sq_ref, sk_ref, oneq_ref, onek_ref, qo_ref, ko_ref, kt_ref, vt_ref, rs, norm_out=None):
    max_sq = lambda x: jnp.max(jnp.sum(x * x, axis=1, keepdims=True), axis=0, keepdims=True)
    qn, kn = [], []
    pieces = jnp.concatenate(_split3(c * LOG2E), axis=1)
    bias_q = _dot(pieces, sq_ref[...]) + oneq_ref[...]
    bias_k = _dot(pieces, sk_ref[...]) + onek_ref[...]
    lane = lax.broadcasted_iota(jnp.int32, (1, LANES), 1)
    lo = lane < ATTN_HEAD_DIM
    for p in range(HEAD_PAIRS):
        ps = slice(p * LANES, (p + 1) * LANES)
        q2 = q[:, ps] * (ATTN_HEAD_DIM ** -0.5 * LOG2E)
        k2 = k[:, ps]
        for hh in range(2):
            h = 2 * p + hh
            hs = slice(h * LANES, (h + 1) * LANES)
            sel = lo if hh == 0 else jnp.logical_not(lo)
            qh = jnp.where(sel, q2, 0.0)
            kh = jnp.where(sel, k2, 0.0)
            qo_ref[h, rs, :] = (qh + bias_q[:, hs]).astype(BF16)
            ko_ref[h, rs, :] = (kh + bias_k[:, hs]).astype(BF16)
            if norm_out is not None:
                qn.append(max_sq(qh))
                kn.append(max_sq(kh))
        for src, dst in ((k2, kt_ref), (v[:, ps], vt_ref)):
            x_t = jnp.transpose(src)
            dst[2 * p, :, rs] = x_t[:ATTN_HEAD_DIM]
            dst[2 * p + 1, :, rs] = x_t[ATTN_HEAD_DIM:]
    if norm_out is not None:
        qn_ref, kn_ref, n = norm_out
        qn_ref[n:n + 1, :] = jnp.concatenate(qn, axis=1)
        kn_ref[n:n + 1, :] = jnp.concatenate(kn, axis=1)


QKV_HALVES = 2


def _qkv_prep_kernel(u_ref, w_ref, c_ref, sq_ref, sk_ref, oneq_ref, onek_ref, qo_ref, ko_ref, kt_ref, vt_ref,
                     qn_ref, kn_ref):
    rows = u_ref.shape[0] // QKV_HALVES
    for s in range(QKV_HALVES):
        rs = slice(s * rows, (s + 1) * rows)
        qkv = _dot(u_ref[rs, :], w_ref[...])
        q, k, v = (qkv[:, n * D_ATTN:(n + 1) * D_ATTN] for n in range(3))
        _prep_rows(q, k, v, c_ref[rs, :], sq_ref, sk_ref, oneq_ref, onek_ref, qo_ref, ko_ref, kt_ref, vt_ref, rs,
                   norm_out=(qn_ref, kn_ref, s))


def _qkv_prep(u, w_qkv, c_rows, consts, *, tile):
    nb, l, _ = c_rows.shape
    nt = l // tile
    sq, sk, one_q, one_k = consts
    c2 = lambda b, i: (0, 0)
    return pl.pallas_call(
        _qkv_prep_kernel,
        grid=(nb, nt),
        in_specs=[
            pl.BlockSpec((tile, D_MODEL), lambda b, i: (b * nt + i, 0)),
            pl.BlockSpec(w_qkv.shape, c2, pipeline_mode=pl.Buffered(1)),
            pl.BlockSpec((None, tile, ATTN_HEADS), lambda b, i: (b, i, 0)),
            pl.BlockSpec(sq.shape, c2), pl.BlockSpec(sk.shape, c2),
            pl.BlockSpec(one_q.shape, c2), pl.BlockSpec(one_k.shape, c2),
        ],
        out_specs=[
            pl.BlockSpec((None, ATTN_HEADS, tile, LANES), lambda b, i: (b, 0, i, 0)),
            pl.BlockSpec((None, ATTN_HEADS, tile, LANES), lambda b, i: (b, 0, i, 0)),
            pl.BlockSpec((None, ATTN_HEADS, ATTN_HEAD_DIM, tile), lambda b, i: (b, 0, 0, i)),
            pl.BlockSpec((None, ATTN_HEADS, ATTN_HEAD_DIM, tile), lambda b, i: (b, 0, 0, i)),
            pl.BlockSpec((None, None, QKV_HALVES, ATTN_HEADS), lambda b, i: (b, i, 0, 0)),
            pl.BlockSpec((None, None, QKV_HALVES, ATTN_HEADS), lambda b, i: (b, i, 0, 0)),
        ],
        out_shape=[
            jax.ShapeDtypeStruct((nb, ATTN_HEADS, l, LANES), BF16),
            jax.ShapeDtypeStruct((nb, ATTN_HEADS, l, LANES), BF16),
            jax.ShapeDtypeStruct((nb, ATTN_HEADS, ATTN_HEAD_DIM, l), F32),
            jax.ShapeDtypeStruct((nb, ATTN_HEADS, ATTN_HEAD_DIM, l), F32),
            jax.ShapeDtypeStruct((nb, nt, QKV_HALVES, ATTN_HEADS), F32),
            jax.ShapeDtypeStruct((nb, nt, QKV_HALVES, ATTN_HEADS), F32),
        ],
        compiler_params=_params("parallel", "parallel"),
        name="qkv_prep",
    )(u, w_qkv, c_rows, sq, sk, one_q, one_k)


def _attn_prep(proj3, c_rows, consts, *, nb, b_off, tile):
    l = c_rows.shape[1]
    col = lambda off: off // D_ATTN
    sq, sk, one_q, one_k = consts
    c2 = lambda b, i: (0, 0)
    return pl.pallas_call(
        _attn_prep_kernel,
        grid=(nb, l // tile),
        in_specs=[
            pl.BlockSpec((None, tile, D_ATTN), lambda b, i: (b + b_off, i, col(P_Q))),
            pl.BlockSpec((None, tile, D_ATTN), lambda b, i: (b + b_off, i, col(P_K))),
            pl.BlockSpec((None, tile, D_ATTN), lambda b, i: (b + b_off, i, col(P_V))),
            pl.BlockSpec((None, tile, ATTN_HEADS), lambda b, i: (b, i, 0)),
            pl.BlockSpec(sq.shape, c2), pl.BlockSpec(sk.shape, c2),
            pl.BlockSpec(one_q.shape, c2), pl.BlockSpec(one_k.shape, c2),
        ],
        out_specs=[
            pl.BlockSpec((None, ATTN_HEADS, tile, LANES), lambda b, i: (b, 0, i, 0)),
            pl.BlockSpec((None, ATTN_HEADS, tile, LANES), lambda b, i: (b, 0, i, 0)),
            pl.BlockSpec((None, ATTN_HEADS, ATTN_HEAD_DIM, tile), lambda b, i: (b, 0, 0, i)),
            pl.BlockSpec((None, ATTN_HEADS, ATTN_HEAD_DIM, tile), lambda b, i: (b, 0, 0, i)),
        ],
        out_shape=[
            jax.ShapeDtypeStruct((nb, ATTN_HEADS, l, LANES), BF16),
            jax.ShapeDtypeStruct((nb, ATTN_HEADS, l, LANES), BF16),
            jax.ShapeDtypeStruct((nb, ATTN_HEADS, ATTN_HEAD_DIM, l), F32),
            jax.ShapeDtypeStruct((nb, ATTN_HEADS, ATTN_HEAD_DIM, l), F32),
        ],
        compiler_params=_params("parallel", "parallel"),
        name="attn_prep",
    )(proj3, proj3, proj3, c_rows, sq, sk, one_q, one_k)


QBLK = 256
ACC_ROWS = ATTN_HEAD_DIM + 16


def _attn_prompt_kernel(it_ref, jt_ref, dead_ref, q_ref, k_ref, vt_ref, km_ref, vtm_ref, o_ref, m_sc, acc_sc,
                        *, meta_lo, meta_hi):
    t = pl.program_id(1)
    i = it_ref[t]
    j = jt_ref[t]
    dead0 = (pl.program_id(0) * pl.num_programs(1) + t) * DEAD_GROUPS

    def v_rows(v_t):
        return jnp.concatenate([v_t.astype(BF16), jnp.ones((ACC_ROWS - ATTN_HEAD_DIM, v_t.shape[1]), BF16)], axis=0)

    def softmax_stage(h, scores, first):
        m_old_all = None if first else m_sc[h]
        out = []
        for r, (s_t, v_aug) in enumerate(scores):
            qblk = s_t.shape[1]
            mx = jnp.max(s_t, axis=0, keepdims=True)
            if first:
                m_new, alpha = mx, None
            else:
                m_old = m_old_all[:, r * qblk:(r + 1) * qblk]
                m_new = jnp.maximum(m_old, mx)
                alpha = jnp.exp2(m_old - m_new)
            out.append((jnp.exp2(s_t - m_new).astype(BF16), v_aug, alpha, m_new))
        m_sc[h] = jnp.concatenate([o[3] for o in out], axis=1)
        return out

    def pv_stage(h, probs, first):
        acc_old_all = None if first else acc_sc[h]
        acc_out = []
        for r, (pr, v_aug, alpha, _) in enumerate(probs):
            qblk = pr.shape[1]
            pv = _dot(v_aug, pr)
            acc_out.append(pv if first else acc_old_all[:, r * qblk:(r + 1) * qblk] * alpha + pv)
        acc_sc[h] = jnp.concatenate(acc_out, axis=1)

    def run(score_fn, first, qblk, heads=range(ATTN_HEADS)):
        scores = lambda h: [score_fn(h, r, q_ref[h, r * qblk:(r + 1) * qblk, :]) for r in range(ATT_T // qblk)]
        h0, h1 = heads[0], heads[-1] + 1
        s_next = scores(h0)
        p_cur = softmax_stage(h0, s_next, first)
        s_next = scores(h0 + 1)
        for h in heads:
            s_cur = s_next
            if h + 2 < h1:
                s_next = scores(h + 2)
            p_prev = p_cur
            if h + 1 < h1:
                p_cur = softmax_stage(h + 1, s_cur, first)
            pv_stage(h, p_prev, first)

    @pl.when(j == 0)
    def _():
        krow = lax.broadcasted_iota(jnp.int32, (km_ref.shape[1], 1), 0)
        is_meta = jnp.logical_and(krow >= meta_lo, krow < meta_hi)
        run(lambda h, r, q: (jnp.where(is_meta, _dot_nt(km_ref[h], q), -jnp.inf), v_rows(vtm_ref[h])), True, ATT_T)

    for g in range(DEAD_GROUPS):
        @pl.when(jnp.logical_and(j < i, dead_ref[dead0 + g] == 0))
        def _():
            hg = ATTN_HEADS // DEAD_GROUPS
            run(lambda h, r, q: (_dot_nt(k_ref[h], q), v_rows(vt_ref[h])), False, ATT_T, range(g * hg, (g + 1) * hg))

    @pl.when(j == i)
    def _():
        def diag_scores(h, r, q):
            nk = (r + 1) * QBLK
            s_t = _dot_nt(k_ref[h, :nk, :], q)
            krow = lax.broadcasted_iota(jnp.int32, (nk, QBLK), 0)
            qcol = lax.broadcasted_iota(jnp.int32, (nk, QBLK), 1) + r * QBLK
            return jnp.where(krow <= qcol, s_t, -jnp.inf), v_rows(vt_ref[h, :, :nk])

        run(diag_scores, False, QBLK)
        d = ATTN_HEAD_DIM
        for p in range(HEAD_PAIRS):
            o_t = jnp.concatenate([acc_sc[h, :d, :] / acc_sc[h, d:d + 1, :] for h in (2 * p, 2 * p + 1)], axis=0)
            o_ref[:, p * LANES:(p + 1) * LANES] = jnp.transpose(o_t).astype(o_ref.dtype)


DEAD_GROUPS = 4
UNDERFLOW_LOG2 = 170.0
NORM_SLACK = 1.01


def _dead_tiles(qn2, kn2, c_rows):
    an = jnp.swapaxes(jnp.sqrt(jnp.max(qn2, axis=2)), 1, 2) * NORM_SLACK
    kn = jnp.swapaxes(jnp.sqrt(jnp.max(kn2, axis=2)), 1, 2) * NORM_SLACK
    c_first = c_rows[:, :, ::ATT_T]
    c_last = c_rows[:, :, ATT_T - 1::ATT_T]
    gap = (an[:, :, :, None] * (kn[:, :, None, :] + kn[:, :, :, None])
           + (c_first[:, :, :, None] - c_last[:, :, None, :]) * LOG2E)
    b, _, nt, _ = gap.shape
    groups = (gap < -UNDERFLOW_LOG2).reshape(b, DEAD_GROUPS, ATTN_HEADS // DEAD_GROUPS, nt, nt)
    return jnp.all(groups, axis=2)


def _attn_prompt(q_aug, k_aug, v_t, km_aug, vm_t, dead_ij, *, meta_lo):
    nb, _, l, _ = q_aug.shape
    nt = l // ATT_T
    it = np.array([i for i in range(nt) for _ in range(i + 1)], np.int32)
    jt = np.array([j for i in range(nt) for j in range(i + 1)], np.int32)
    dead = jnp.logical_and(dead_ij[:, :, it, jt], jnp.asarray(jt < it)[None, None])
    dead = jnp.swapaxes(dead, 1, 2).astype(jnp.int32).reshape(-1)
    mrows = km_aug.shape[2]
    grid_spec = pltpu.PrefetchScalarGridSpec(
        num_scalar_prefetch=3,
        grid=(nb, len(it)),
        in_specs=[
            pl.BlockSpec((None, ATTN_HEADS, ATT_T, LANES), lambda b, t, it, jt, dd: (b, 0, it[t], 0)),
            pl.BlockSpec((None, ATTN_HEADS, ATT_T, LANES), lambda b, t, it, jt, dd: (b, 0, jt[t], 0)),
            pl.BlockSpec((None, ATTN_HEADS, ATTN_HEAD_DIM, ATT_T), lambda b, t, it, jt, dd: (b, 0, 0, jt[t])),
            pl.BlockSpec((None, ATTN_HEADS, mrows, LANES), lambda b, t, it, jt, dd: (0, 0, 0, 0)),
            pl.BlockSpec((None, ATTN_HEADS, ATTN_HEAD_DIM, mrows), lambda b, t, it, jt, dd: (0, 0, 0, 0)),
        ],
        out_specs=pl.BlockSpec((None, ATT_T, D_ATTN), lambda b, t, it, jt, dd: (b, it[t], 0)),
        scratch_shapes=[
            pltpu.VMEM((ATTN_HEADS, 1, ATT_T), F32),
            pltpu.VMEM((ATTN_HEADS, ACC_ROWS, ATT_T), F32),
        ],
    )
    return pl.pallas_call(
        functools.partial(_attn_prompt_kernel, meta_lo=meta_lo, meta_hi=meta_lo + N_META),
        grid_spec=grid_spec,
        out_shape=jax.ShapeDtypeStruct((nb, l, D_ATTN), BF16),
        compiler_params=_params("parallel", "arbitrary"),
        name="attn_prompt",
    )(jnp.asarray(it), jnp.asarray(jt), dead, q_aug, k_aug, v_t, km_aug, vm_t)


SAMPLE_HEADS = 8


def _attn_sample_kernel(q_ref, kn_ref, vn_ref, km_ref, vm_ref, kct_ref, vct_ref, cq_ref, ckm_ref, ckc_ref,
                        ckn_ref, o_ref):
    T = CHUNK
    D = ATTN_HEAD_DIM
    extra = ACC_ROWS - D
    row = lax.broadcasted_iota(jnp.int32, (T, T), 0)
    col = lax.broadcasted_iota(jnp.int32, (T, T), 1)

    def v_cols(v):
        return jnp.concatenate([v.astype(BF16), jnp.ones((v.shape[0], extra), BF16)], axis=1)

    def v_rows(v_t):
        return jnp.concatenate([v_t.astype(BF16), jnp.ones((extra, v_t.shape[1]), BF16)], axis=0)

    scores = []
    for hh in range(SAMPLE_HEADS):
        hs = slice(hh * D, (hh + 1) * D)
        qh = (q_ref[:, hs] * (D ** -0.5 * LOG2E)).astype(BF16)
        t_m = _dot_nt(qh, km_ref[:, hs].astype(BF16)) - ckm_ref[hh:hh + 1, :] * LOG2E
        t_c = _dot(qh, kct_ref[hh].astype(BF16)) - ckc_ref[hh:hh + 1, :] * LOG2E
        t_n = _dot_nt(qh, kn_ref[:, hs].astype(BF16)) - ckn_ref[hh:hh + 1, :] * LOG2E
        scores.append((t_m, t_c, jnp.where(col <= row, t_n, -jnp.inf)))
    outs = []
    for hh, (t_m, t_c, t_n) in enumerate(scores):
        hs = slice(hh * D, (hh + 1) * D)
        cq = cq_ref[hh] * LOG2E
        m = cq + jnp.maximum(jnp.maximum(jnp.max(t_m, axis=1, keepdims=True), jnp.max(t_c, axis=1, keepdims=True)),
                             jnp.max(t_n, axis=1, keepdims=True))
        shift = cq - m
        acc = (_dot(jnp.exp2(t_m + shift).astype(BF16), v_cols(vm_ref[:, hs]))
               + _dot_nt(jnp.exp2(t_c + shift).astype(BF16), v_rows(vct_ref[hh]))
               + _dot(jnp.exp2(t_n + shift).astype(BF16), v_cols(vn_ref[:, hs])))
        outs.append(acc[:, :D] / acc[:, D:D + 1])
    o_ref[...] = jnp.concatenate(outs, axis=1).astype(o_ref.dtype)


def _attn_sample(proj3, proj2, meta_blk, cache_k, cache_v, cq, ckm_t, ckc_t, ckn_t, *, nb):
    past = cache_k.shape[3]
    hg = SAMPLE_HEADS
    wg = hg * ATTN_HEAD_DIM
    groups = ATTN_HEADS // hg
    lb = lambda off: off // wg
    grp = lambda off: (lambda b, p: (b, 0, lb(off) + p))
    ckspec = lambda n: pl.BlockSpec((None, hg, n), lambda b, p: (b * groups + p, 0, 0))
    return pl.pallas_call(
        _attn_sample_kernel,
        grid=(nb, groups),
        in_specs=[
            pl.BlockSpec((None, CHUNK, wg), grp(P_Q)),
            pl.BlockSpec((None, CHUNK, wg), grp(P_K)),
            pl.BlockSpec((None, CHUNK, wg), grp(P_V)),
            pl.BlockSpec((N_META, wg), lambda b, p: (meta_blk, lb(P_K) + p)),
            pl.BlockSpec((N_META, wg), lambda b, p: (meta_blk, lb(P_V) + p)),
            pl.BlockSpec((None, hg, ATTN_HEAD_DIM, past), lambda b, p: (b, p, 0, 0)),
            pl.BlockSpec((None, hg, ATTN_HEAD_DIM, past), lambda b, p: (b, p, 0, 0)),
            pl.BlockSpec((None, hg, CHUNK, 1), lambda b, p: (b, p, 0, 0)),
            ckspec(N_META), ckspec(past), ckspec(CHUNK),
        ],
        out_specs=pl.BlockSpec((None, CHUNK, wg), lambda b, p: (b, 0, p)),
        out_shape=jax.ShapeDtypeStruct((nb, CHUNK, D_ATTN), BF16),
        compiler_params=_params("parallel", "parallel"),
        name="attn_sample",
    )(proj3, proj3, proj3, proj2, proj2, cache_k, cache_v, cq, ckm_t, ckc_t, ckn_t)


def _merge_kernel(os_ref, oa_ref, gs_ref, ga_ref, h_ref, wbs_ref, wba_ref, wo_ref, o_ref):
    bs = _dot(os_ref[...], wbs_ref[...])
    ba = _dot(oa_ref[...], wba_ref[...])
    merged = jax.nn.sigmoid(gs_ref[...]) * bs + jax.nn.sigmoid(ga_ref[...]) * ba
    o_ref[...] = h_ref[...] + _dot(merged.astype(BF16), wo_ref[...])


def _merge(o_ssm, o_attn, proj, h, w_bs, w_ba, w_o, *, tm, gs_off, ga_off):
    rows = o_ssm.shape[0]
    const = lambda i: (0, 0)
    return pl.pallas_call(
        _merge_kernel,
        grid=(rows // tm,),
        in_specs=[
            pl.BlockSpec((tm, D_SSM), lambda i: (i, 0)),
            pl.BlockSpec((tm, D_ATTN), lambda i: (i, 0)),
            pl.BlockSpec((tm, D_MODEL), lambda i: (i, gs_off // D_MODEL)),
            pl.BlockSpec((tm, D_MODEL), lambda i: (i, ga_off // D_MODEL)),
            pl.BlockSpec((tm, D_MODEL), lambda i: (i, 0)),
            pl.BlockSpec((D_SSM, D_MODEL), const),
            pl.BlockSpec((D_ATTN, D_MODEL), const),
            pl.BlockSpec((D_MODEL, D_MODEL), const),
        ],
        out_specs=pl.BlockSpec((tm, D_MODEL), lambda i: (i, 0)),
        out_shape=jax.ShapeDtypeStruct((rows, D_MODEL), F32),
        compiler_params=_params("parallel"),
        name="merge",
    )(o_ssm, o_attn, proj, proj, h, w_bs, w_ba, w_o)


def kernel(x_prompt, x_sample, cache_k, cache_v, cache_logf, state_ssm, state_conv, meta_tokens, norm_ffn1, ffn1_w_in, ffn1_w_out, norm_mix, w_in, conv_w, conv_b, dt_bias, a_log, d_skip, f_bias, ssm_norm, w_br_ssm, w_br_attn, w_out, norm_ffn2, ffn2_w_in, ffn2_w_out, norm_final):
    assert norm_ffn1.shape[0] == 1, "single-layer trunk"
    bp, seq, _ = x_prompt.shape
    bs, dseq, _ = x_sample.shape
    assert dseq == CHUNK and seq % ATT_T == 0
    n_s = bs * dseq
    rows_small = n_s + 2 * CHUNK
    meta_row0 = n_s + CHUNK - N_META
    meta_blk = meta_row0 // N_META

    row = lambda v: v.reshape(1, -1).astype(F32)
    w = w_in[0]
    w_main = jnp.concatenate([w[:, OFF_XBC:OFF_DT], w[:, :OFF_XBC], w[:, OFF_Q:OFF_F], w[:, OFF_GS:]],
                             axis=1).astype(BF16)
    pad_small = LANES - SSM_HEADS - ATTN_HEADS
    w_small = jnp.concatenate([w[:, OFF_DT:OFF_Q], w[:, OFF_F:OFF_GS], jnp.zeros((D_MODEL, pad_small), F32)],
                              axis=1).astype(BF16)
    b_small = jnp.concatenate([dt_bias[0], f_bias[0], jnp.zeros((pad_small,), F32)]).reshape(1, LANES)
    w1i, w1o = ffn1_w_in[0].astype(BF16), ffn1_w_out[0].astype(BF16)
    w2i, w2o = ffn2_w_in[0].astype(BF16), ffn2_w_out[0].astype(BF16)
    w_bs, w_ba, w_o = w_br_ssm[0].astype(BF16), w_br_attn[0].astype(BF16), w_out[0].astype(BF16)
    expand = jnp.asarray(np.tile(np.repeat(np.eye(SSM_HEADS, dtype=np.float32), SSM_HEAD_DIM, axis=1), (3, 1)),
                         BF16)
    ssd_consts = (conv_w[0], row(conv_b[0]), row(a_log[0]), row(jnp.repeat(d_skip[0], SSM_HEAD_DIM)),
                  row(ssm_norm[0]), expand)

    def front(x, w_proj, tm_ffn, tm_proj):
        h1, u = _ffn(x, row(norm_ffn1[0]), w1i, w1o, row(norm_mix[0]), tm=tm_ffn, emit_h=True, u_dtype=BF16)
        return h1, u, _proj(u, w_proj, tm=tm_proj), _small(u, w_small, b_small, tm=tm_proj)

    w_qkv = w_main[:, P_Q:P_GS]
    w_prompt = jnp.concatenate([w_main[:, :P_Q], w_main[:, P_GS:]], axis=1)
    xp = x_prompt.reshape(bp * seq, D_MODEL)
    h1_p, u_p, pm_p, ps_p = front(xp, w_prompt, 512, 1024)
    x_small = jnp.concatenate([x_sample.reshape(n_s, D_MODEL), jnp.zeros((CHUNK - N_META, D_MODEL), F32),
                               meta_tokens, jnp.zeros((CHUNK, D_MODEL), F32)], axis=0)
    h1_s, _, pm_s, ps_s = front(x_small, w_main, rows_small // 2, rows_small // 2)

    pm_p3 = pm_p.reshape(bp, seq, w_prompt.shape[1])
    ps_p3 = ps_p.reshape(bp, seq, LANES)
    pm_s3 = pm_s.reshape(rows_small // CHUNK, CHUNK, P_TOTAL)
    ps_s3 = ps_s.reshape(rows_small // CHUNK, CHUNK, LANES)

    zero_state = jnp.zeros((1, D_STATE, D_SSM), F32)
    zero_hist = jnp.zeros((1, CONV_W - 1, D_CONV), F32)
    _, st_m, hist_m = _ssd(pm_s3, ps_s3, zero_state, zero_hist, *ssd_consts,
                           nb=1, b_off=bs, shared_init=True, n_pad=CHUNK - N_META, cps=1)
    o_ssm_p, st_p, hist_p = _ssd(pm_p3, ps_p3, st_m, hist_m, *ssd_consts,
                                 nb=bp, b_off=0, shared_init=True, n_pad=0, cps=SSD_CHUNKS_PER_STEP)
    st_s0 = jnp.transpose(state_ssm[0].reshape(bs, D_SSM, D_STATE), (0, 2, 1))
    o_ssm_s, st_s, hist_s = _ssd(pm_s3, ps_s3, st_s0, state_conv[0], *ssd_consts,
                                 nb=bs, b_off=0, shared_init=False, n_pad=0, cps=1)

    t_last = lambda a: jnp.swapaxes(a, -1, -2)
    lf_m = t_last(ps_s[meta_row0:meta_row0 + N_META, S_F:S_F + ATTN_HEADS])
    lf_p = t_last(ps_p3[:, :, S_F:S_F + ATTN_HEADS])
    lf_s = ps_s[:n_s, S_F:S_F + ATTN_HEADS].reshape(bs, dseq, ATTN_HEADS)
    lf_all_p = jnp.concatenate([jnp.broadcast_to(lf_m[None], (bp, ATTN_HEADS, N_META)), lf_p], axis=2)
    lf_all_s = jnp.concatenate([jnp.broadcast_to(lf_m[None], (bs, ATTN_HEADS, N_META)),
                                t_last(cache_logf[0].astype(F32)), t_last(lf_s)], axis=2)
    past = cache_logf.shape[2]
    c_p = _cum_logf(lf_all_p)
    c_s = _cum_logf(lf_all_s)

    prep_consts = _bias_placement()
    q_aug, k_aug, k_t, v_t, qn2, kn2 = _qkv_prep(u_p, w_qkv, t_last(c_p[:, :, N_META:]), prep_consts, tile=ATT_T)
    meta_tile = 2 * CHUNK
    meta_lo = meta_row0 - n_s
    c_meta = jnp.pad(t_last(c_p[:1, :, :N_META]), ((0, 0), (meta_lo, meta_tile - meta_lo - N_META), (0, 0)))
    _, km_aug, km_t, vm_t = _attn_prep(pm_s.reshape(rows_small // meta_tile, meta_tile, P_TOTAL), c_meta,
                                       prep_consts, nb=1, b_off=n_s // meta_tile, tile=meta_tile)
    o_attn_p = _attn_prompt(q_aug, k_aug, v_t, km_aug, vm_t, _dead_tiles(qn2, kn2, c_p[:, :, N_META:]),
                            meta_lo=meta_lo)
    per_pair = lambda c: c.reshape(bs * ATTN_HEADS // SAMPLE_HEADS, SAMPLE_HEADS, c.shape[2])
    feat_major = lambda a: jnp.transpose(a, (0, 2, 3, 1))
    o_attn_s = _attn_sample(pm_s3, pm_s, meta_blk, feat_major(cache_k[0]), feat_major(cache_v[0]),
                            c_s[:, :, N_META + past:, None],
                            per_pair(c_s[:, :, :N_META]), per_pair(c_s[:, :, N_META:N_META + past]),
                            per_pair(c_s[:, :, N_META + past:]), nb=bs)

    def back(o_ssm, o_attn, pm, h1, tm, gs_off, ga_off):
        h2 = _merge(o_ssm, o_attn, pm, h1, w_bs, w_ba, w_o, tm=tm, gs_off=gs_off, ga_off=ga_off)
        return _ffn(h2, row(norm_ffn2[0]), w2i, w2o, row(norm_final), tm=tm, emit_h=False, u_dtype=F32)[0]

    y_prompt = back(o_ssm_p.reshape(bp * seq, D_SSM), o_attn_p.reshape(bp * seq, D_ATTN), pm_p, h1_p, 512,
                    PP_GS, PP_GA)
    y_sample = back(o_ssm_s.reshape(n_s, D_SSM), o_attn_s.reshape(n_s, D_ATTN), pm_s, h1_s, 512, P_GS, P_GA)

    heads = lambda a: a.reshape(1, a.shape[0], a.shape[1], ATTN_HEADS, ATTN_HEAD_DIM)

    def with_meta(x_t, m_t):
        m_t = jnp.broadcast_to(m_t[:, :, :, meta_lo:meta_lo + N_META], (bp, ATTN_HEADS, ATTN_HEAD_DIM, N_META))
        return jnp.transpose(jnp.concatenate([m_t, x_t], axis=3), (0, 3, 1, 2))[None]

    k_prompt = with_meta(k_t, km_t)
    v_prompt = with_meta(v_t, vm_t)
    logf_prompt = t_last(lf_all_p)
    k_sample = heads(pm_s[:n_s, P_K:P_K + D_ATTN].reshape(bs, dseq, D_ATTN))
    v_sample = heads(pm_s[:n_s, P_V:P_V + D_ATTN].reshape(bs, dseq, D_ATTN))
    state_out = lambda st: jnp.transpose(st, (0, 2, 1)).reshape(1, st.shape[0], SSM_HEADS, SSM_HEAD_DIM, D_STATE)
    return (y_prompt.reshape(bp, seq, D_MODEL), y_sample.reshape(bs, dseq, D_MODEL),
            k_prompt, v_prompt, logf_prompt[None], state_out(st_p), hist_p[None],
            k_sample, v_sample, lf_s[None], state_out(st_s), hist_s[None])
```

```python
import functools

import numpy as np
import jax
import jax.numpy as jnp
from jax import lax
from jax.experimental import pallas as pl
from jax.experimental.pallas import tpu as pltpu

F32 = jnp.float32
BF16 = jnp.bfloat16

D_MODEL = 1024
N_META = 16
CHUNK = 64
EPS = 1e-6
D_SSM = 2048
SSM_HEADS = 32
SSM_HEAD_DIM = 64
SSM_GROUPS = 8
D_STATE = 128
CONV_W = 4
D_CONV = D_SSM + 2 * SSM_GROUPS * D_STATE
ATTN_HEADS = 16
ATTN_HEAD_DIM = 64
D_ATTN = ATTN_HEADS * ATTN_HEAD_DIM
D_FF = 2816
HEAD_PAIRS = ATTN_HEADS // 2
LANES = 128
GROUP_W = D_SSM // SSM_GROUPS

OFF_XBC = D_SSM
OFF_DT = OFF_XBC + D_CONV
OFF_Q = OFF_DT + SSM_HEADS
OFF_K = OFF_Q + D_ATTN
OFF_V = OFF_K + D_ATTN
OFF_F = OFF_V + D_ATTN
OFF_GS = OFF_F + ATTN_HEADS
OFF_GA = OFF_GS + D_MODEL

P_XBC = 0
P_Z = D_CONV
P_Q = P_Z + D_SSM
P_K = P_Q + D_ATTN
P_V = P_K + D_ATTN
P_GS = P_V + D_ATTN
P_GA = P_GS + D_MODEL
P_TOTAL = P_GA + D_MODEL
S_DT = 0
S_F = SSM_HEADS

VMEM_LIMIT = 56 * 1024 * 1024

PROJ_COL_TILES = 4
PP_GS = P_Q
PP_GA = PP_GS + D_MODEL
ATT_T = 512
CUMSUM_BLK = 256


def _dot(a, b):
    return jnp.dot(a, b, preferred_element_type=F32)


def _dot_nt(a, b):
    return lax.dot_general(a, b, (((1,), (1,)), ((), ())), preferred_element_type=F32)


def _dot_tn(a, b):
    return lax.dot_general(a, b, (((0,), (0,)), ((), ())), preferred_element_type=F32)


def _split3(x):
    hi = x.astype(BF16)
    r = x - hi.astype(F32)
    mid = r.astype(BF16)
    lo = (r - mid.astype(F32)).astype(BF16)
    return hi, mid, lo


def _rmsnorm(x, g):
    return x * lax.rsqrt(jnp.mean(x * x, axis=-1, keepdims=True) + EPS) * g


def _softplus(x):
    return jnp.maximum(x, 0.0) + jnp.log1p(jnp.exp(-jnp.abs(x)))


def _params(*sem):
    return pltpu.CompilerParams(dimension_semantics=sem, vmem_limit_bytes=VMEM_LIMIT)


FFN_HALVES = 2


def _ffn_kernel(x_ref, g1_ref, wi_ref, wo_ref, g2_ref, *out_refs, emit_h):
    h_ref, u_ref = out_refs if emit_h else (None,) + out_refs
    rows = x_ref.shape[0] // FFN_HALVES
    for s in range(FFN_HALVES):
        rs = slice(s * rows, (s + 1) * rows)
        x = x_ref[rs, :]
        xn = _rmsnorm(x, g1_ref[...]).astype(BF16)
        a = _dot(xn, wi_ref[:, :D_FF])
        b = _dot(xn, wi_ref[:, D_FF:])
        g = (a * jax.nn.sigmoid(a) * b).astype(BF16)
        h = x + 0.5 * _dot(g, wo_ref[...])
        if emit_h:
            h_ref[rs, :] = h
        u_ref[rs, :] = _rmsnorm(h, g2_ref[...]).astype(u_ref.dtype)


def _ffn(x, g1, w_in, w_out, g2, *, tm, emit_h, u_dtype):
    rows = x.shape[0]
    row_spec = pl.BlockSpec((tm, D_MODEL), lambda i: (i, 0))
    vec_spec = pl.BlockSpec((1, D_MODEL), lambda i: (0, 0))
    resident = lambda shape: pl.BlockSpec(shape, lambda i: (0, 0), pipeline_mode=pl.Buffered(1))
    out_shape = [jax.ShapeDtypeStruct((rows, D_MODEL), u_dtype)]
    out_specs = [row_spec]
    if emit_h:
        out_shape = [jax.ShapeDtypeStruct((rows, D_MODEL), F32)] + out_shape
        out_specs = [row_spec] + out_specs
    return pl.pallas_call(
        functools.partial(_ffn_kernel, emit_h=emit_h),
        grid=(rows // tm,),
        in_specs=[row_spec, vec_spec, resident((D_MODEL, 2 * D_FF)), resident((D_FF, D_MODEL)), vec_spec],
        out_specs=out_specs,
        out_shape=out_shape,
        compiler_params=_params("parallel"),
        name="ffn",
    )(x, g1, w_in, w_out, g2)


def _proj_kernel(u_ref, w_ref, o_ref):
    o_ref[...] = _dot(u_ref[...], w_ref[...])


def _proj(u, w, *, tm):
    rows = u.shape[0]
    width = w.shape[1]
    tn = width // PROJ_COL_TILES
    return pl.pallas_call(
        _proj_kernel,
        grid=(PROJ_COL_TILES, rows // tm),
        in_specs=[pl.BlockSpec((tm, D_MODEL), lambda j, i: (i, 0)),
                  pl.BlockSpec((D_MODEL, tn), lambda j, i: (0, j))],
        out_specs=pl.BlockSpec((tm, tn), lambda j, i: (i, j)),
        out_shape=jax.ShapeDtypeStruct((rows, width), F32),
        compiler_params=_params("parallel", "parallel"),
        name="proj",
    )(u, w)


def _small_kernel(u_ref, w_ref, b_ref, o_ref):
    x = _dot(u_ref[...], w_ref[...]) + b_ref[...]
    lane = lax.broadcasted_iota(jnp.int32, (1, LANES), 1)
    sgn = jnp.where(lane < S_F, 1.0, -1.0)
    o_ref[...] = sgn * _softplus(sgn * x)


def _small(u, w, b, *, tm):
    rows = u.shape[0]
    return pl.pallas_call(
        _small_kernel,
        grid=(rows // tm,),
        in_specs=[pl.BlockSpec((tm, D_MODEL), lambda i: (i, 0)),
                  pl.BlockSpec((D_MODEL, LANES), lambda i: (0, 0)),
                  pl.BlockSpec((1, LANES), lambda i: (0, 0))],
        out_specs=pl.BlockSpec((tm, LANES), lambda i: (i, 0)),
        out_shape=jax.ShapeDtypeStruct((rows, LANES), F32),
        compiler_params=_params("parallel"),
        name="small",
    )(u, w, b)


HIST0 = 8 - (CONV_W - 1)
SSD_CHUNKS_PER_STEP = 8


def _ssd_chunk(rs, xbc_ref, z_ref, dt_ref, cw_ref, cb_ref, alog_ref, dskip_ref, norm_ref, e_ref, o_ref,
               st_sc, xp_sc, y_sc, n_pad):
    T = CHUNK
    xraw = xbc_ref[rs, :]
    xp_sc[8:8 + T, :] = xraw
    cw = cw_ref[...]
    xp = xp_sc[...]
    conv = cb_ref[...]
    for j in range(CONV_W - 1):
        conv = conv + cw[j:j + 1] * pltpu.roll(xp, CONV_W - 1 - j, axis=0)[8:8 + T, :]
    conv = conv + cw[3:4] * xraw
    xp_sc[HIST0:8, :] = xraw[T - (CONV_W - 1):T, :]
    xc = conv * jax.nn.sigmoid(conv)
    dtv = dt_ref[rs, S_DT:S_DT + SSM_HEADS]
    if n_pad:
        valid = lax.broadcasted_iota(jnp.int32, (T, 1), 0) >= n_pad
        xc = jnp.where(valid, xc, 0.0)
        dtv = jnp.where(valid, dtv, 0.0)

    a = -jnp.exp(alog_ref[...])
    adt = a * dtv
    row = lax.broadcasted_iota(jnp.int32, (T, T), 0)
    col = lax.broadcasted_iota(jnp.int32, (T, T), 1)
    causal = col <= row
    tril = jnp.where(causal, 1.0, 0.0).astype(BF16)
    r32 = lax.broadcasted_iota(jnp.int32, (SSM_HEADS, SSM_HEADS), 0)
    c32 = lax.broadcasted_iota(jnp.int32, (SSM_HEADS, SSM_HEADS), 1)
    eye = jnp.where(r32 == c32, 1.0, 0.0).astype(BF16)
    expand3 = e_ref[...]

    def expand(v):
        return _dot(jnp.concatenate(_split3(v), axis=1), expand3)

    a_cs = sum(_dot(tril, piece) for piece in _split3(adt))
    a_cs_t = sum(_dot_nt(eye, piece) for piece in _split3(a_cs))
    a_last = a_cs[T - 1:T, :]
    dt_e = expand(dtv)
    wdec_e = expand(jnp.exp(a_last - a_cs) * dtv)
    eacs = expand(jnp.exp(a_cs))
    cdec = eacs[T - 1:T, :]

    xs = xc[:, :D_SSM]
    bm = xc[:, D_SSM:D_SSM + SSM_GROUPS * D_STATE].astype(BF16)
    cm = xc[:, D_SSM + SSM_GROUPS * D_STATE:].astype(BF16)
    xd = xs * dt_e
    xdw = (xs * wdec_e).astype(BF16)
    lane = lax.broadcasted_iota(jnp.int32, (1, D_SSM), 1)
    lo = (lane & SSM_HEAD_DIM) == 0
    xd_lo = jnp.where(lo, xd, 0.0).astype(BF16)
    xd_hi = jnp.where(lo, 0.0, xd).astype(BF16)
    lo2 = lax.broadcasted_iota(jnp.int32, (1, LANES), 1) < SSM_HEAD_DIM
    step2 = lax.broadcasted_iota(jnp.int32, (T, LANES), 1) & (SSM_HEAD_DIM - 1)
    causal2 = step2 <= lax.broadcasted_iota(jnp.int32, (T, LANES), 0)

    for g in range(SSM_GROUPS):
        gs = slice(g * GROUP_W, (g + 1) * GROUP_W)
        cmg = cm[:, g * D_STATE:(g + 1) * D_STATE]
        bmg = bm[:, g * D_STATE:(g + 1) * D_STATE]
        cb2 = _dot_nt(cmg, jnp.concatenate([bmg, bmg], axis=0))
        stg = st_sc[:, gs]
        yoff = _dot(cmg, stg.astype(BF16))
        for rr in range(2):
            pair = g * 2 + rr
            ps = slice(pair * LANES, (pair + 1) * LANES)
            ha, hb = 2 * pair, 2 * pair + 1
            seg = (jnp.where(lo2, a_cs[:, ha:ha + 1], a_cs[:, hb:hb + 1])
                   - jnp.concatenate([a_cs_t[ha:ha + 1, :], a_cs_t[hb:hb + 1, :]], axis=1))
            m2 = (cb2 * jnp.exp(jnp.where(causal2, seg, -jnp.inf))).astype(BF16)
            ydiag = _dot(m2, jnp.concatenate([xd_lo[:, ps], xd_hi[:, ps]], axis=0))
            y_sc[:, ps] = ydiag + yoff[:, rr * LANES:(rr + 1) * LANES] * eacs[:, ps]
        st_sc[:, gs] = stg * cdec[:, gs] + _dot_tn(bmg, xdw[:, gs])

    y = y_sc[...] + dskip_ref[...] * xs
    zz = z_ref[rs, :]
    y = y * (zz * jax.nn.sigmoid(zz))
    for g in range(SSM_GROUPS):
        gs = slice(g * GROUP_W, (g + 1) * GROUP_W)
        yg = y[:, gs]
        yg = yg * lax.rsqrt(jnp.mean(yg * yg, axis=-1, keepdims=True) + EPS)
        o_ref[rs, gs] = (yg * norm_ref[:, gs]).astype(o_ref.dtype)


def _ssd_kernel(xbc_ref, z_ref, dt_ref, s0_ref, h0_ref, cw_ref, cb_ref, alog_ref, dskip_ref, norm_ref,
                e_ref, o_ref, sT_ref, hT_ref, st_sc, xp_sc, y_sc, *, n_pad, cps):
    c = pl.program_id(1)

    @pl.when(c == 0)
    def _():
        st_sc[...] = s0_ref[...]
        xp_sc[HIST0:8, :] = h0_ref[...]

    for ci in range(cps):
        _ssd_chunk(slice(ci * CHUNK, (ci + 1) * CHUNK), xbc_ref, z_ref, dt_ref, cw_ref, cb_ref, alog_ref,
                   dskip_ref, norm_ref, e_ref, o_ref, st_sc, xp_sc, y_sc, n_pad)

    @pl.when(c == pl.num_programs(1) - 1)
    def _():
        sT_ref[...] = st_sc[...]
        hT_ref[...] = xp_sc[HIST0:8, :]


def _ssd(proj3, small3, s0, h0, cw, cb, alog, dskip, norm, expand, *, nb, b_off, shared_init, n_pad, cps):
    rows = cps * CHUNK
    nc = proj3.shape[1] // rows
    init = (lambda b, c: (0, 0, 0)) if shared_init else (lambda b, c: (b, 0, 0))
    const = lambda b, c: (0, 0)
    return pl.pallas_call(
        functools.partial(_ssd_kernel, n_pad=n_pad, cps=cps),
        grid=(nb, nc),
        in_specs=[
            pl.BlockSpec((None, rows, D_CONV), lambda b, c: (b + b_off, c, P_XBC // D_CONV)),
            pl.BlockSpec((None, rows, D_SSM), lambda b, c: (b + b_off, c, P_Z // D_SSM)),
            pl.BlockSpec((None, rows, LANES), lambda b, c: (b + b_off, c, 0)),
            pl.BlockSpec((None, D_STATE, D_SSM), init),
            pl.BlockSpec((None, CONV_W - 1, D_CONV), init),
            pl.BlockSpec((CONV_W, D_CONV), const),
            pl.BlockSpec((1, D_CONV), const),
            pl.BlockSpec((1, SSM_HEADS), const),
            pl.BlockSpec((1, D_SSM), const),
            pl.BlockSpec((1, D_SSM), const),
            pl.BlockSpec((3 * SSM_HEADS, D_SSM), const),
        ],
        out_specs=[
            pl.BlockSpec((None, rows, D_SSM), lambda b, c: (b, c, 0)),
            pl.BlockSpec((None, D_STATE, D_SSM), lambda b, c: (b, 0, 0)),
            pl.BlockSpec((None, CONV_W - 1, D_CONV), lambda b, c: (b, 0, 0)),
        ],
        out_shape=[
            jax.ShapeDtypeStruct((nb, nc * rows, D_SSM), BF16),
            jax.ShapeDtypeStruct((nb, D_STATE, D_SSM), F32),
            jax.ShapeDtypeStruct((nb, CONV_W - 1, D_CONV), F32),
        ],
        scratch_shapes=[
            pltpu.VMEM((D_STATE, D_SSM), F32),
            pltpu.VMEM((8 + CHUNK, D_CONV), F32),
            pltpu.VMEM((CHUNK, D_SSM), F32),
        ],
        compiler_params=_params("parallel", "arbitrary"),
        name="ssd",
    )(proj3, proj3, small3, s0, h0, cw, cb, alog, dskip, norm, expand)


def _cumsum_kernel(x_ref, o_ref):
    n = x_ref.shape[-1] // CUMSUM_BLK
    r = lax.broadcasted_iota(jnp.int32, (CUMSUM_BLK, CUMSUM_BLK), 0)
    c = lax.broadcasted_iota(jnp.int32, (CUMSUM_BLK, CUMSUM_BLK), 1)
    upper = jnp.where(r <= c, 1.0, 0.0).astype(BF16)
    carry = jnp.zeros((ATTN_HEADS, 1), F32)
    for i in range(n):
        blk = slice(i * CUMSUM_BLK, (i + 1) * CUMSUM_BLK)
        cs = sum(_dot(piece, upper) for piece in _split3(x_ref[:, blk])) + carry
        o_ref[:, blk] = cs
        carry = cs[:, CUMSUM_BLK - 1:CUMSUM_BLK]


def _cumsum_t(x_t):
    b, h, l = x_t.shape
    spec = pl.BlockSpec((None, h, l), lambda i: (i, 0, 0))
    return pl.pallas_call(
        _cumsum_kernel, grid=(b,), in_specs=[spec], out_specs=spec,
        out_shape=jax.ShapeDtypeStruct(x_t.shape, F32),
        compiler_params=_params("parallel"), name="cumsum",
    )(x_t)


def _cum_logf(lf_t):
    l = lf_t.shape[2]
    lp = -(-l // CUMSUM_BLK) * CUMSUM_BLK
    return _cumsum_t(jnp.pad(lf_t, ((0, 0), (0, 0), (0, lp - l))))[:, :, :l]


BIAS_PIECES = 3
LOG2E = 1.4426950408889634


def _bias_lane0(head):
    return ATTN_HEAD_DIM if head % 2 == 0 else 0


def _bias_placement():
    sq = np.zeros((BIAS_PIECES, ATTN_HEADS, ATTN_HEADS * LANES), np.float32)
    sk = np.zeros_like(sq)
    one_q = np.zeros((1, ATTN_HEADS * LANES), np.float32)
    one_k = np.zeros_like(one_q)
    for h in range(ATTN_HEADS):
        base = h * LANES + _bias_lane0(h)
        for piece in range(BIAS_PIECES):
            sk[piece, h, base + piece] = -1.0
            one_q[0, base + piece] = 1.0
            sq[piece, h, base + BIAS_PIECES + piece] = 1.0
            one_k[0, base + BIAS_PIECES + piece] = 1.0
    flat = lambda a: jnp.asarray(a.reshape(BIAS_PIECES * ATTN_HEADS, ATTN_HEADS * LANES), BF16)
    return flat(sq), flat(sk), jnp.asarray(one_q), jnp.asarray(one_k)


def _attn_prep_kernel(q_ref, k_ref, v_ref, c_ref, sq_ref, sk_ref, oneq_ref, onek_ref, qo_ref, ko_ref, kt_ref,
                      vt_ref):
    _prep_rows(q_ref[...], k_ref[...], v_ref[...], c_ref[...], sq_ref, sk_ref, oneq_ref, onek_ref,
               qo_ref, ko_ref, kt_ref, vt_ref, slice(None))


def _prep_rows(q, k, v, c, sq_ref, sk_ref, oneq_ref, onek_ref, qo_ref, ko_ref, kt_ref, vt_ref, rs, norm_out=None):
    max_sq = lambda x: jnp.max(jnp.sum(x * x, axis=1, keepdims=True), axis=0, keepdims=True)
    qn, kn = [], []
    pieces = jnp.concatenate(_split3(c * LOG2E), axis=1)
    bias_q = _dot(pieces, sq_ref[...]) + oneq_ref[...]
    bias_k = _dot(pieces, sk_ref[...]) + onek_ref[...]
    lane = lax.broadcasted_iota(jnp.int32, (1, LANES), 1)
    lo = lane < ATTN_HEAD_DIM
    for p in range(HEAD_PAIRS):
        ps = slice(p * LANES, (p + 1) * LANES)
        q2 = q[:, ps] * (ATTN_HEAD_DIM ** -0.5 * LOG2E)
        k2 = k[:, ps]
        for hh in range(2):
            h = 2 * p + hh
            hs = slice(h * LANES, (h + 1) * LANES)
            sel = lo if hh == 0 else jnp.logical_not(lo)
            qh = jnp.where(sel, q2, 0.0)
            kh = jnp.where(sel, k2, 0.0)
            qo_ref[h, rs, :] = (qh + bias_q[:, hs]).astype(BF16)
            ko_ref[h, rs, :] = (kh + bias_k[:, hs]).astype(BF16)
            if norm_out is not None:
                qn.append(max_sq(qh))
                kn.append(max_sq(kh))
        for src, dst in ((k2, kt_ref), (v[:, ps], vt_ref)):
            x_t = jnp.transpose(src)
            dst[2 * p, :, rs] = x_t[:ATTN_HEAD_DIM]
            dst[2 * p + 1, :, rs] = x_t[ATTN_HEAD_DIM:]
    if norm_out is not None:
        qn_ref, kn_ref, n = norm_out
        qn_ref[n:n + 1, :] = jnp.concatenate(qn, axis=1)
        kn_ref[n:n + 1, :] = jnp.concatenate(kn, axis=1)


QKV_HALVES = 2


def _qkv_prep_kernel(u_ref, w_ref, c_ref, sq_ref, sk_ref, oneq_ref, onek_ref, qo_ref, ko_ref, kt_ref, vt_ref,
                     qn_ref, kn_ref):
    rows = u_ref.shape[0] // QKV_HALVES
    for s in range(QKV_HALVES):
        rs = slice(s * rows, (s + 1) * rows)
        qkv = _dot(u_ref[rs, :], w_ref[...])
        q, k, v = (qkv[:, n * D_ATTN:(n + 1) * D_ATTN] for n in range(3))
        _prep_rows(q, k, v, c_ref[rs, :], sq_ref, sk_ref, oneq_ref, onek_ref, qo_ref, ko_ref, kt_ref, vt_ref, rs,
                   norm_out=(qn_ref, kn_ref, s))


def _qkv_prep(u, w_qkv, c_rows, consts, *, tile):
    nb, l, _ = c_rows.shape
    nt = l // tile
    sq, sk, one_q, one_k = consts
    c2 = lambda b, i: (0, 0)
    return pl.pallas_call(
        _qkv_prep_kernel,
        grid=(nb, nt),
        in_specs=[
            pl.BlockSpec((tile, D_MODEL), lambda b, i: (b * nt + i, 0)),
            pl.BlockSpec(w_qkv.shape, c2, pipeline_mode=pl.Buffered(1)),
            pl.BlockSpec((None, tile, ATTN_HEADS), lambda b, i: (b, i, 0)),
            pl.BlockSpec(sq.shape, c2), pl.BlockSpec(sk.shape, c2),
            pl.BlockSpec(one_q.shape, c2), pl.BlockSpec(one_k.shape, c2),
        ],
        out_specs=[
            pl.BlockSpec((None, ATTN_HEADS, tile, LANES), lambda b, i: (b, 0, i, 0)),
            pl.BlockSpec((None, ATTN_HEADS, tile, LANES), lambda b, i: (b, 0, i, 0)),
            pl.BlockSpec((None, ATTN_HEADS, ATTN_HEAD_DIM, tile), lambda b, i: (b, 0, 0, i)),
            pl.BlockSpec((None, ATTN_HEADS, ATTN_HEAD_DIM, tile), lambda b, i: (b, 0, 0, i)),
            pl.BlockSpec((None, None, QKV_HALVES, ATTN_HEADS), lambda b, i: (b, i, 0, 0)),
            pl.BlockSpec((None, None, QKV_HALVES, ATTN_HEADS), lambda b, i: (b, i, 0, 0)),
        ],
        out_shape=[
            jax.ShapeDtypeStruct((nb, ATTN_HEADS, l, LANES), BF16),
            jax.ShapeDtypeStruct((nb, ATTN_HEADS, l, LANES), BF16),
            jax.ShapeDtypeStruct((nb, ATTN_HEADS, ATTN_HEAD_DIM, l), F32),
            jax.ShapeDtypeStruct((nb, ATTN_HEADS, ATTN_HEAD_DIM, l), F32),
            jax.ShapeDtypeStruct((nb, nt, QKV_HALVES, ATTN_HEADS), F32),
            jax.ShapeDtypeStruct((nb, nt, QKV_HALVES, ATTN_HEADS), F32),
        ],
        compiler_params=_params("parallel", "parallel"),
        name="qkv_prep",
    )(u, w_qkv, c_rows, sq, sk, one_q, one_k)


def _attn_prep(proj3, c_rows, consts, *, nb, b_off, tile):
    l = c_rows.shape[1]
    col = lambda off: off // D_ATTN
    sq, sk, one_q, one_k = consts
    c2 = lambda b, i: (0, 0)
    return pl.pallas_call(
        _attn_prep_kernel,
        grid=(nb, l // tile),
        in_specs=[
            pl.BlockSpec((None, tile, D_ATTN), lambda b, i: (b + b_off, i, col(P_Q))),
            pl.BlockSpec((None, tile, D_ATTN), lambda b, i: (b + b_off, i, col(P_K))),
            pl.BlockSpec((None, tile, D_ATTN), lambda b, i: (b + b_off, i, col(P_V))),
            pl.BlockSpec((None, tile, ATTN_HEADS), lambda b, i: (b, i, 0)),
            pl.BlockSpec(sq.shape, c2), pl.BlockSpec(sk.shape, c2),
            pl.BlockSpec(one_q.shape, c2), pl.BlockSpec(one_k.shape, c2),
        ],
        out_specs=[
            pl.BlockSpec((None, ATTN_HEADS, tile, LANES), lambda b, i: (b, 0, i, 0)),
            pl.BlockSpec((None, ATTN_HEADS, tile, LANES), lambda b, i: (b, 0, i, 0)),
            pl.BlockSpec((None, ATTN_HEADS, ATTN_HEAD_DIM, tile), lambda b, i: (b, 0, 0, i)),
            pl.BlockSpec((None, ATTN_HEADS, ATTN_HEAD_DIM, tile), lambda b, i: (b, 0, 0, i)),
        ],
        out_shape=[
            jax.ShapeDtypeStruct((nb, ATTN_HEADS, l, LANES), BF16),
            jax.ShapeDtypeStruct((nb, ATTN_HEADS, l, LANES), BF16),
            jax.ShapeDtypeStruct((nb, ATTN_HEADS, ATTN_HEAD_DIM, l), F32),
            jax.ShapeDtypeStruct((nb, ATTN_HEADS, ATTN_HEAD_DIM, l), F32),
        ],
        compiler_params=_params("parallel", "parallel"),
        name="attn_prep",
    )(proj3, proj3, proj3, c_rows, sq, sk, one_q, one_k)


QBLK = 256
ACC_ROWS = ATTN_HEAD_DIM + 16


KV_RING = 3


def _attn_prompt_kernel(it_ref, jt_ref, dead_ref, q_ref, k_hbm, vt_hbm, km_ref, vtm_ref, o_ref, m_sc, acc_sc,
                        kbuf, vbuf, sem, *, meta_lo, meta_hi):
    t = pl.program_id(1)
    steps = pl.num_programs(1)
    n = pl.program_id(0) * steps + t
    total = pl.num_programs(0) * steps
    i = it_ref[t]
    j = jt_ref[t]
    dead0 = n * DEAD_GROUPS

    def kv_copies(s):
        slot = s % KV_RING
        rows = pl.ds(pl.multiple_of(jt_ref[s % steps] * ATT_T, ATT_T), ATT_T)
        return (pltpu.make_async_copy(k_hbm.at[s // steps, :, rows, :], kbuf.at[slot], sem.at[0, slot]),
                pltpu.make_async_copy(vt_hbm.at[s // steps, :, :, rows], vbuf.at[slot], sem.at[1, slot]))

    @pl.when(n == 0)
    def _():
        for s in range(KV_RING - 1):
            for cp in kv_copies(s):
                cp.start()

    @pl.when(n + KV_RING - 1 < total)
    def _():
        for cp in kv_copies(n + KV_RING - 1):
            cp.start()

    for cp in kv_copies(n):
        cp.wait()
    k_ref = kbuf.at[n % KV_RING]
    vt_ref = vbuf.at[n % KV_RING]

    def v_rows(v_t):
        return jnp.concatenate([v_t.astype(BF16), jnp.ones((ACC_ROWS - ATTN_HEAD_DIM, v_t.shape[1]), BF16)], axis=0)

    def softmax_stage(h, scores, first):
        m_old_all = None if first else m_sc[h]
        out = []
        for r, (s_t, v_aug) in enumerate(scores):
            qblk = s_t.shape[1]
            mx = jnp.max(s_t, axis=0, keepdims=True)
            if first:
                m_new, alpha = mx, None
            else:
                m_old = m_old_all[:, r * qblk:(r + 1) * qblk]
                m_new = jnp.maximum(m_old, mx)
                alpha = jnp.exp2(m_old - m_new)
            out.append((jnp.exp2(s_t - m_new).astype(BF16), v_aug, alpha, m_new))
        m_sc[h] = jnp.concatenate([o[3] for o in out], axis=1)
        return out

    def pv_stage(h, probs, first):
        acc_old_all = None if first else acc_sc[h]
        acc_out = []
        for r, (pr, v_aug, alpha, _) in enumerate(probs):
            qblk = pr.shape[1]
            pv = _dot(v_aug, pr)
            acc_out.append(pv if first else acc_old_all[:, r * qblk:(r + 1) * qblk] * alpha + pv)
        acc_sc[h] = jnp.concatenate(acc_out, axis=1)

    def run(score_fn, first, qblk, heads=range(ATTN_HEADS)):
        scores = lambda h: [score_fn(h, r, q_ref[h, r * qblk:(r + 1) * qblk, :]) for r in range(ATT_T // qblk)]
        h0, h1 = heads[0], heads[-1] + 1
        s_next = scores(h0)
        p_cur = softmax_stage(h0, s_next, first)
        s_next = scores(h0 + 1)
        for h in heads:
            s_cur = s_next
            if h + 2 < h1:
                s_next = scores(h + 2)
            p_prev = p_cur
            if h + 1 < h1:
                p_cur = softmax_stage(h + 1, s_cur, first)
            pv_stage(h, p_prev, first)

    @pl.when(j == 0)
    def _():
        krow = lax.broadcasted_iota(jnp.int32, (km_ref.shape[1], 1), 0)
        is_meta = jnp.logical_and(krow >= meta_lo, krow < meta_hi)
        run(lambda h, r, q: (jnp.where(is_meta, _dot_nt(km_ref[h], q), -jnp.inf), v_rows(vtm_ref[h])), True, ATT_T)

    for g in range(DEAD_GROUPS):
        @pl.when(jnp.logical_and(j < i, dead_ref[dead0 + g] == 0))
        def _():
            hg = ATTN_HEADS // DEAD_GROUPS
            run(lambda h, r, q: (_dot_nt(k_ref[h], q), v_rows(vt_ref[h])), False, ATT_T, range(g * hg, (g + 1) * hg))

    @pl.when(j == i)
    def _():
        def diag_scores(h, r, q):
            nk = (r + 1) * QBLK
            s_t = _dot_nt(k_ref[h, :nk, :], q)
            krow = lax.broadcasted_iota(jnp.int32, (nk, QBLK), 0)
            qcol = lax.broadcasted_iota(jnp.int32, (nk, QBLK), 1) + r * QBLK
            return jnp.where(krow <= qcol, s_t, -jnp.inf), v_rows(vt_ref[h, :, :nk])

        run(diag_scores, False, QBLK)
        d = ATTN_HEAD_DIM
        for p in range(HEAD_PAIRS):
            o_t = jnp.concatenate([acc_sc[h, :d, :] / acc_sc[h, d:d + 1, :] for h in (2 * p, 2 * p + 1)], axis=0)
            o_ref[:, p * LANES:(p + 1) * LANES] = jnp.transpose(o_t).astype(o_ref.dtype)


DEAD_GROUPS = 4
UNDERFLOW_LOG2 = 170.0
NORM_SLACK = 1.01


def _dead_tiles(qn2, kn2, c_rows):
    an = jnp.swapaxes(jnp.sqrt(jnp.max(qn2, axis=2)), 1, 2) * NORM_SLACK
    kn = jnp.swapaxes(jnp.sqrt(jnp.max(kn2, axis=2)), 1, 2) * NORM_SLACK
    c_first = c_rows[:, :, ::ATT_T]
    c_last = c_rows[:, :, ATT_T - 1::ATT_T]
    gap = (an[:, :, :, None] * (kn[:, :, None, :] + kn[:, :, :, None])
           + (c_first[:, :, :, None] - c_last[:, :, None, :]) * LOG2E)
    b, _, nt, _ = gap.shape
    groups = (gap < -UNDERFLOW_LOG2).reshape(b, DEAD_GROUPS, ATTN_HEADS // DEAD_GROUPS, nt, nt)
    return jnp.all(groups, axis=2)


def _attn_prompt(q_aug, k_aug, v_t, km_aug, vm_t, dead_ij, *, meta_lo):
    nb, _, l, _ = q_aug.shape
    nt = l // ATT_T
    it = np.array([i for i in range(nt) for _ in range(i + 1)], np.int32)
    jt = np.array([j for i in range(nt) for j in range(i + 1)], np.int32)
    dead = jnp.logical_and(dead_ij[:, :, it, jt], jnp.asarray(jt < it)[None, None])
    dead = jnp.swapaxes(dead, 1, 2).astype(jnp.int32).reshape(-1)
    mrows = km_aug.shape[2]
    grid_spec = pltpu.PrefetchScalarGridSpec(
        num_scalar_prefetch=3,
        grid=(nb, len(it)),
        in_specs=[
            pl.BlockSpec((None, ATTN_HEADS, ATT_T, LANES), lambda b, t, it, jt, dd: (b, 0, it[t], 0)),
            pl.BlockSpec(memory_space=pl.ANY),
            pl.BlockSpec(memory_space=pl.ANY),
            pl.BlockSpec((None, ATTN_HEADS, mrows, LANES), lambda b, t, it, jt, dd: (0, 0, 0, 0)),
            pl.BlockSpec((None, ATTN_HEADS, ATTN_HEAD_DIM, mrows), lambda b, t, it, jt, dd: (0, 0, 0, 0)),
        ],
        out_specs=pl.BlockSpec((None, ATT_T, D_ATTN), lambda b, t, it, jt, dd: (b, it[t], 0)),
        scratch_shapes=[
            pltpu.VMEM((ATTN_HEADS, 1, ATT_T), F32),
            pltpu.VMEM((ATTN_HEADS, ACC_ROWS, ATT_T), F32),
            pltpu.VMEM((KV_RING, ATTN_HEADS, ATT_T, LANES), BF16),
            pltpu.VMEM((KV_RING, ATTN_HEADS, ATTN_HEAD_DIM, ATT_T), F32),
            pltpu.SemaphoreType.DMA((2, KV_RING)),
        ],
    )
    return pl.pallas_call(
        functools.partial(_attn_prompt_kernel, meta_lo=meta_lo, meta_hi=meta_lo + N_META),
        grid_spec=grid_spec,
        out_shape=jax.ShapeDtypeStruct((nb, l, D_ATTN), BF16),
        compiler_params=_params("arbitrary", "arbitrary"),
        name="attn_prompt",
    )(jnp.asarray(it), jnp.asarray(jt), dead, q_aug, k_aug, v_t, km_aug, vm_t)


SAMPLE_HEADS = 8


def _attn_sample_kernel(q_ref, kn_ref, vn_ref, km_ref, vm_ref, kct_ref, vct_ref, cq_ref, ckm_ref, ckc_ref,
                        ckn_ref, o_ref):
    T = CHUNK
    D = ATTN_HEAD_DIM
    extra = ACC_ROWS - D
    row = lax.broadcasted_iota(jnp.int32, (T, T), 0)
    col = lax.broadcasted_iota(jnp.int32, (T, T), 1)

    def v_cols(v):
        return jnp.concatenate([v.astype(BF16), jnp.ones((v.shape[0], extra), BF16)], axis=1)

    def v_rows(v_t):
        return jnp.concatenate([v_t.astype(BF16), jnp.ones((extra, v_t.shape[1]), BF16)], axis=0)

    scores = []
    for hh in range(SAMPLE_HEADS):
        hs = slice(hh * D, (hh + 1) * D)
        qh = (q_ref[:, hs] * (D ** -0.5 * LOG2E)).astype(BF16)
        t_m = _dot_nt(qh, km_ref[:, hs].astype(BF16)) - ckm_ref[hh:hh + 1, :] * LOG2E
        t_c = _dot(qh, kct_ref[hh].astype(BF16)) - ckc_ref[hh:hh + 1, :] * LOG2E
        t_n = _dot_nt(qh, kn_ref[:, hs].astype(BF16)) - ckn_ref[hh:hh + 1, :] * LOG2E
        scores.append((t_m, t_c, jnp.where(col <= row, t_n, -jnp.inf)))
    outs = []
    for hh, (t_m, t_c, t_n) in enumerate(scores):
        hs = slice(hh * D, (hh + 1) * D)
        cq = cq_ref[hh] * LOG2E
        m = cq + jnp.maximum(jnp.maximum(jnp.max(t_m, axis=1, keepdims=True), jnp.max(t_c, axis=1, keepdims=True)),
                             jnp.max(t_n, axis=1, keepdims=True))
        shift = cq - m
        acc = (_dot(jnp.exp2(t_m + shift).astype(BF16), v_cols(vm_ref[:, hs]))
               + _dot_nt(jnp.exp2(t_c + shift).astype(BF16), v_rows(vct_ref[hh]))
               + _dot(jnp.exp2(t_n + shift).astype(BF16), v_cols(vn_ref[:, hs])))
        outs.append(acc[:, :D] / acc[:, D:D + 1])
    o_ref[...] = jnp.concatenate(outs, axis=1).astype(o_ref.dtype)


def _attn_sample(proj3, proj2, meta_blk, cache_k, cache_v, cq, ckm_t, ckc_t, ckn_t, *, nb):
    past = cache_k.shape[3]
    hg = SAMPLE_HEADS
    wg = hg * ATTN_HEAD_DIM
    groups = ATTN_HEADS // hg
    lb = lambda off: off // wg
    grp = lambda off: (lambda b, p: (b, 0, lb(off) + p))
    ckspec = lambda n: pl.BlockSpec((None, hg, n), lambda b, p: (b * groups + p, 0, 0))
    return pl.pallas_call(
        _attn_sample_kernel,
        grid=(nb, groups),
        in_specs=[
            pl.BlockSpec((None, CHUNK, wg), grp(P_Q)),
            pl.BlockSpec((None, CHUNK, wg), grp(P_K)),
            pl.BlockSpec((None, CHUNK, wg), grp(P_V)),
            pl.BlockSpec((N_META, wg), lambda b, p: (meta_blk, lb(P_K) + p)),
            pl.BlockSpec((N_META, wg), lambda b, p: (meta_blk, lb(P_V) + p)),
            pl.BlockSpec((None, hg, ATTN_HEAD_DIM, past), lambda b, p: (b, p, 0, 0)),
            pl.BlockSpec((None, hg, ATTN_HEAD_DIM, past), lambda b, p: (b, p, 0, 0)),
            pl.BlockSpec((None, hg, CHUNK, 1), lambda b, p: (b, p, 0, 0)),
            ckspec(N_META), ckspec(past), ckspec(CHUNK),
        ],
        out_specs=pl.BlockSpec((None, CHUNK, wg), lambda b, p: (b, 0, p)),
        out_shape=jax.ShapeDtypeStruct((nb, CHUNK, D_ATTN), BF16),
        compiler_params=_params("parallel", "parallel"),
        name="attn_sample",
    )(proj3, proj3, proj3, proj2, proj2, cache_k, cache_v, cq, ckm_t, ckc_t, ckn_t)


def _merge_kernel(os_ref, oa_ref, gs_ref, ga_ref, h_ref, wbs_ref, wba_ref, wo_ref, o_ref):
    bs = _dot(os_ref[...], wbs_ref[...])
    ba = _dot(oa_ref[...], wba_ref[...])
    merged = jax.nn.sigmoid(gs_ref[...]) * bs + jax.nn.sigmoid(ga_ref[...]) * ba
    o_ref[...] = h_ref[...] + _dot(merged.astype(BF16), wo_ref[...])


def _merge(o_ssm, o_attn, proj, h, w_bs, w_ba, w_o, *, tm, gs_off, ga_off):
    rows = o_ssm.shape[0]
    const = lambda i: (0, 0)
    return pl.pallas_call(
        _merge_kernel,
        grid=(rows // tm,),
        in_specs=[
            pl.BlockSpec((tm, D_SSM), lambda i: (i, 0)),
            pl.BlockSpec((tm, D_ATTN), lambda i: (i, 0)),
            pl.BlockSpec((tm, D_MODEL), lambda i: (i, gs_off // D_MODEL)),
            pl.BlockSpec((tm, D_MODEL), lambda i: (i, ga_off // D_MODEL)),
            pl.BlockSpec((tm, D_MODEL), lambda i: (i, 0)),
            pl.BlockSpec((D_SSM, D_MODEL), const),
            pl.BlockSpec((D_ATTN, D_MODEL), const),
            pl.BlockSpec((D_MODEL, D_MODEL), const),
        ],
        out_specs=pl.BlockSpec((tm, D_MODEL), lambda i: (i, 0)),
        out_shape=jax.ShapeDtypeStruct((rows, D_MODEL), F32),
        compiler_params=_params("parallel"),
        name="merge",
    )(o_ssm, o_attn, proj, proj, h, w_bs, w_ba, w_o)


def kernel(x_prompt, x_sample, cache_k, cache_v, cache_logf, state_ssm, state_conv, meta_tokens, norm_ffn1, ffn1_w_in, ffn1_w_out, norm_mix, w_in, conv_w, conv_b, dt_bias, a_log, d_skip, f_bias, ssm_norm, w_br_ssm, w_br_attn, w_out, norm_ffn2, ffn2_w_in, ffn2_w_out, norm_final):
    assert norm_ffn1.shape[0] == 1, "single-layer trunk"
    bp, seq, _ = x_prompt.shape
    bs, dseq, _ = x_sample.shape
    assert dseq == CHUNK and seq % ATT_T == 0
    n_s = bs * dseq
    rows_small = n_s + 2 * CHUNK
    meta_row0 = n_s + CHUNK - N_META
    meta_blk = meta_row0 // N_META

    row = lambda v: v.reshape(1, -1).astype(F32)
    w = w_in[0]
    w_main = jnp.concatenate([w[:, OFF_XBC:OFF_DT], w[:, :OFF_XBC], w[:, OFF_Q:OFF_F], w[:, OFF_GS:]],
                             axis=1).astype(BF16)
    pad_small = LANES - SSM_HEADS - ATTN_HEADS
    w_small = jnp.concatenate([w[:, OFF_DT:OFF_Q], w[:, OFF_F:OFF_GS], jnp.zeros((D_MODEL, pad_small), F32)],
                              axis=1).astype(BF16)
    b_small = jnp.concatenate([dt_bias[0], f_bias[0], jnp.zeros((pad_small,), F32)]).reshape(1, LANES)
    w1i, w1o = ffn1_w_in[0].astype(BF16), ffn1_w_out[0].astype(BF16)
    w2i, w2o = ffn2_w_in[0].astype(BF16), ffn2_w_out[0].astype(BF16)
    w_bs, w_ba, w_o = w_br_ssm[0].astype(BF16), w_br_attn[0].astype(BF16), w_out[0].astype(BF16)
    expand = jnp.asarray(np.tile(np.repeat(np.eye(SSM_HEADS, dtype=np.float32), SSM_HEAD_DIM, axis=1), (3, 1)),
                         BF16)
    ssd_consts = (conv_w[0], row(conv_b[0]), row(a_log[0]), row(jnp.repeat(d_skip[0], SSM_HEAD_DIM)),
                  row(ssm_norm[0]), expand)

    def front(x, w_proj, tm_ffn, tm_proj):
        h1, u = _ffn(x, row(norm_ffn1[0]), w1i, w1o, row(norm_mix[0]), tm=tm_ffn, emit_h=True, u_dtype=BF16)
        return h1, u, _proj(u, w_proj, tm=tm_proj), _small(u, w_small, b_small, tm=tm_proj)

    w_qkv = w_main[:, P_Q:P_GS]
    w_prompt = jnp.concatenate([w_main[:, :P_Q], w_main[:, P_GS:]], axis=1)
    xp = x_prompt.reshape(bp * seq, D_MODEL)
    h1_p, u_p, pm_p, ps_p = front(xp, w_prompt, 512, 1024)
    x_small = jnp.concatenate([x_sample.reshape(n_s, D_MODEL), jnp.zeros((CHUNK - N_META, D_MODEL), F32),
                               meta_tokens, jnp.zeros((CHUNK, D_MODEL), F32)], axis=0)
    h1_s, _, pm_s, ps_s = front(x_small, w_main, rows_small // 2, rows_small // 2)

    pm_p3 = pm_p.reshape(bp, seq, w_prompt.shape[1])
    ps_p3 = ps_p.reshape(bp, seq, LANES)
    pm_s3 = pm_s.reshape(rows_small // CHUNK, CHUNK, P_TOTAL)
    ps_s3 = ps_s.reshape(rows_small // CHUNK, CHUNK, LANES)

    zero_state = jnp.zeros((1, D_STATE, D_SSM), F32)
    zero_hist = jnp.zeros((1, CONV_W - 1, D_CONV), F32)
    _, st_m, hist_m = _ssd(pm_s3, ps_s3, zero_state, zero_hist, *ssd_consts,
                           nb=1, b_off=bs, shared_init=True, n_pad=CHUNK - N_META, cps=1)
    o_ssm_p, st_p, hist_p = _ssd(pm_p3, ps_p3, st_m, hist_m, *ssd_consts,
                                 nb=bp, b_off=0, shared_init=True, n_pad=0, cps=SSD_CHUNKS_PER_STEP)
    st_s0 = jnp.transpose(state_ssm[0].reshape(bs, D_SSM, D_STATE), (0, 2, 1))
    o_ssm_s, st_s, hist_s = _ssd(pm_s3, ps_s3, st_s0, state_conv[0], *ssd_consts,
                                 nb=bs, b_off=0, shared_init=False, n_pad=0, cps=1)

    t_last = lambda a: jnp.swapaxes(a, -1, -2)
    lf_m = t_last(ps_s[meta_row0:meta_row0 + N_META, S_F:S_F + ATTN_HEADS])
    lf_p = t_last(ps_p3[:, :, S_F:S_F + ATTN_HEADS])
    lf_s = ps_s[:n_s, S_F:S_F + ATTN_HEADS].reshape(bs, dseq, ATTN_HEADS)
    lf_all_p = jnp.concatenate([jnp.broadcast_to(lf_m[None], (bp, ATTN_HEADS, N_META)), lf_p], axis=2)
    lf_all_s = jnp.concatenate([jnp.broadcast_to(lf_m[None], (bs, ATTN_HEADS, N_META)),
                                t_last(cache_logf[0].astype(F32)), t_last(lf_s)], axis=2)
    past = cache_logf.shape[2]
    c_p = _cum_logf(lf_all_p)
    c_s = _cum_logf(lf_all_s)

    prep_consts = _bias_placement()
    q_aug, k_aug, k_t, v_t, qn2, kn2 = _qkv_prep(u_p, w_qkv, t_last(c_p[:, :, N_META:]), prep_consts, tile=ATT_T)
    meta_tile = 2 * CHUNK
    meta_lo = meta_row0 - n_s
    c_meta = jnp.pad(t_last(c_p[:1, :, :N_META]), ((0, 0), (meta_lo, meta_tile - meta_lo - N_META), (0, 0)))
    _, km_aug, km_t, vm_t = _attn_prep(pm_s.reshape(rows_small // meta_tile, meta_tile, P_TOTAL), c_meta,
                                       prep_consts, nb=1, b_off=n_s // meta_tile, tile=meta_tile)
    o_attn_p = _attn_prompt(q_aug, k_aug, v_t, km_aug, vm_t, _dead_tiles(qn2, kn2, c_p[:, :, N_META:]),
                            meta_lo=meta_lo)
    per_pair = lambda c: c.reshape(bs * ATTN_HEADS // SAMPLE_HEADS, SAMPLE_HEADS, c.shape[2])
    feat_major = lambda a: jnp.transpose(a, (0, 2, 3, 1))
    o_attn_s = _attn_sample(pm_s3, pm_s, meta_blk, feat_major(cache_k[0]), feat_major(cache_v[0]),
                            c_s[:, :, N_META + past:, None],
                            per_pair(c_s[:, :, :N_META]), per_pair(c_s[:, :, N_META:N_META + past]),
                            per_pair(c_s[:, :, N_META + past:]), nb=bs)

    def back(o_ssm, o_attn, pm, h1, tm, gs_off, ga_off):
        h2 = _merge(o_ssm, o_attn, pm, h1, w_bs, w_ba, w_o, tm=tm, gs_off=gs_off, ga_off=ga_off)
        return _ffn(h2, row(norm_ffn2[0]), w2i, w2o, row(norm_final), tm=tm, emit_h=False, u_dtype=F32)[0]

    y_prompt = back(o_ssm_p.reshape(bp * seq, D_SSM), o_attn_p.reshape(bp * seq, D_ATTN), pm_p, h1_p, 512,
                    PP_GS, PP_GA)
    y_sample = back(o_ssm_s.reshape(n_s, D_SSM), o_attn_s.reshape(n_s, D_ATTN), pm_s, h1_s, 512, P_GS, P_GA)

    heads = lambda a: a.reshape(1, a.shape[0], a.shape[1], ATTN_HEADS, ATTN_HEAD_DIM)

    def with_meta(x_t, m_t):
        m_t = jnp.broadcast_to(m_t[:, :, :, meta_lo:meta_lo + N_META], (bp, ATTN_HEADS, ATTN_HEAD_DIM, N_META))
        return jnp.transpose(jnp.concatenate([m_t, x_t], axis=3), (0, 3, 1, 2))[None]

    k_prompt = with_meta(k_t, km_t)
    v_prompt = with_meta(v_t, vm_t)
    logf_prompt = t_last(lf_all_p)
    k_sample = heads(pm_s[:n_s, P_K:P_K + D_ATTN].reshape(bs, dseq, D_ATTN))
    v_sample = heads(pm_s[:n_s, P_V:P_V + D_ATTN].reshape(bs, dseq, D_ATTN))
    state_out = lambda st: jnp.transpose(st, (0, 2, 1)).reshape(1, st.shape[0], SSM_HEADS, SSM_HEAD_DIM, D_STATE)
    return (y_prompt.reshape(bp, seq, D_MODEL), y_sample.reshape(bs, dseq, D_MODEL),
            k_prompt, v_prompt, logf_prompt[None], state_out(st_p), hist_p[None],
            k_sample, v_sample, lf_s[None], state_out(st_s), hist_s[None])
```
